```python
import math
import jax, jax.numpy as jnp
from jax import lax
import numpy as np

D_MODEL = 4096
BATCH = 2
SEQ = 8192
DEPTH = 2

HEAD_DIM = 128
CONV_CH = D_MODEL // 2
CONV_WIDTH = 31
NAT_HEADS = (D_MODEL // 2) // HEAD_DIM
NAT_WIDTH = NAT_HEADS * HEAD_DIM
NAT_WIN_ROWS = 8
NAT_WIN_COLS = 16
GRID_W = 64
DIL_CONFIGS = ((128, 1), (512, 4), (2048, 16))
N_DIL_GROUPS = len(DIL_CONFIGS)
DIL_HEADS = (D_MODEL // 2) // HEAD_DIM
DIL_WIDTH = DIL_HEADS * HEAD_DIM
T5_BUCKETS = 32
T5_MAX_DIST = 1024
N_GROUPS = 4
EXPERTS_PER_GROUP = 8
N_EXPERTS = N_GROUPS * EXPERTS_PER_GROUP
TOP_K = 2
EXPERT_HIDDEN = D_MODEL // 8
MOE_BLOCK = 128
RMS_EPS = 1e-6
LN_EPS = 1e-5
N_EVEN = (DEPTH + 1) // 2
N_ODD = DEPTH // 2
IN_AB = 2 * CONV_CH + 3 * NAT_WIDTH
IN_C = N_DIL_GROUPS * 3 * DIL_WIDTH

kernel_name = "hybrid_conv_nat_dilated_hmoe_encoder"


def rms_norm(x, g):
    x32 = x.astype(jnp.float32)
    y = x32 * lax.rsqrt(jnp.mean(x32 * x32, axis=-1, keepdims=True) + RMS_EPS) * g.astype(jnp.float32)
    return y.astype(x.dtype)


def conformer_conv(a_in, w_dw, b_dw, ln_g, ln_b):
    a = a_in[..., :CONV_CH] * jax.nn.sigmoid(a_in[..., CONV_CH:])
    pad = CONV_WIDTH // 2
    a = lax.conv_general_dilated(
        a, w_dw[:, None, :].astype(a.dtype), window_strides=(1,),
        padding=[(pad, pad)], dimension_numbers=("NWC", "WIO", "NWC"),
        feature_group_count=CONV_CH)
    a32 = a.astype(jnp.float32) + b_dw.astype(jnp.float32)
    mu = jnp.mean(a32, axis=-1, keepdims=True)
    var = jnp.mean(jnp.square(a32 - mu), axis=-1, keepdims=True)
    a32 = (a32 - mu) * lax.rsqrt(var + LN_EPS) * ln_g.astype(jnp.float32) + ln_b.astype(jnp.float32)
    return jax.nn.silu(a32).astype(a_in.dtype)


def neighbourhood_attention(q, k, v, rpb):
    B, T, _ = q.shape
    rows = T // GRID_W
    kr = min(NAT_WIN_ROWS, rows)
    kc = NAT_WIN_COLS

    def grid(t):
        return t.reshape(B, rows, GRID_W, NAT_HEADS, HEAD_DIM).transpose(0, 3, 1, 2, 4)

    qg = grid(q * (HEAD_DIM ** -0.5))
    kg = grid(k)
    vg = grid(v)
    cols = jnp.arange(GRID_W)
    col_start = jnp.clip(cols - kc // 2, 0, GRID_W - kc)
    col_mask = (cols[None, :] >= col_start[:, None]) & (cols[None, :] < col_start[:, None] + kc)
    dc_idx = jnp.clip(cols[None, :] - cols[:, None], -(kc - 1), kc - 1) + (NAT_WIN_COLS - 1)
    rpb_c = rpb[:, :, dc_idx].astype(jnp.float32)

    def row_block(r):
        r0 = jnp.clip(r - kr // 2, 0, rows - kr)
        q_r = lax.dynamic_index_in_dim(qg, r, axis=2, keepdims=False)
        k_r = lax.dynamic_slice_in_dim(kg, r0, kr, axis=2)
        v_r = lax.dynamic_slice_in_dim(vg, r0, kr, axis=2)
        s = jnp.einsum("bhqd,bhrkd->bhqrk", q_r, k_r, preferred_element_type=jnp.float32)
        dr_idx = r0 + jnp.arange(kr) - r + (NAT_WIN_ROWS - 1)
        bias = jnp.take(rpb_c, dr_idx, axis=1).transpose(0, 2, 1, 3)
        s = jnp.where(col_mask[None, None, :, None, :], s + bias[None], -jnp.inf)
        p = jax.nn.softmax(s.reshape(B, NAT_HEADS, GRID_W, kr * GRID_W), axis=-1).reshape(s.shape)
        return jnp.einsum("bhqrk,bhrkd->bhqd", p.astype(v_r.dtype), v_r)

    out = lax.map(row_block, jnp.arange(rows))
    return out.transpose(1, 0, 3, 2, 4).reshape(B, T, NAT_WIDTH)


def t5_bucket(rel):
    nb = T5_BUCKETS // 2
    max_exact = nb // 2
    n = jnp.abs(rel)
    sign = jnp.where(rel > 0, nb, 0)
    nf = jnp.maximum(n, 1).astype(jnp.float32)
    large = max_exact + (jnp.log(nf / max_exact) / math.log(T5_MAX_DIST / max_exact)
                         * (nb - max_exact)).astype(jnp.int32)
    large = jnp.minimum(large, nb - 1)
    return sign + jnp.where(n < max_exact, n, large)


def dilated_window_attention(q, k, v, bias_table, dilation, half):
    B, T, H, dh = q.shape
    L = T // dilation
    nblk = -(-L // half)
    lpad = nblk * half

    def strided(t):
        return t.reshape(B, L, dilation, H, dh).transpose(0, 2, 1, 3, 4)

    qs = strided(q * (HEAD_DIM ** -0.5))
    qb = jnp.pad(qs, ((0, 0), (0, 0), (0, lpad - L), (0, 0), (0, 0))).reshape(B, dilation, nblk, half, H, dh)

    def windows(t):
        tp = jnp.pad(strided(t), ((0, 0), (0, 0), (half, lpad - L + half), (0, 0), (0, 0)))
        tp = tp.reshape(B, dilation, nblk + 2, half, H, dh)
        return jnp.concatenate([tp[:, :, :-2], tp[:, :, 1:-1], tp[:, :, 2:]], axis=3)

    kw = windows(k)
    vw = windows(v)
    s = jnp.einsum("bsjqhd,bsjkhd->bsjhqk", qb, kw, preferred_element_type=jnp.float32)
    qi = jnp.arange(half)
    ki = jnp.arange(3 * half)
    delta = ki[None, :] - half - qi[:, None]
    band = jnp.abs(delta) <= half
    bias = bias_table[t5_bucket(delta * dilation)].transpose(2, 0, 1).astype(jnp.float32)
    key_pos = jnp.arange(nblk)[:, None] * half + ki[None, :] - half
    key_ok = (key_pos >= 0) & (key_pos < L)
    valid = band[None, :, :] & key_ok[:, None, :]
    s = jnp.where(valid[None, None, :, None], s + bias[None, None, None], -jnp.inf)
    m = jnp.max(s, axis=-1, keepdims=True)
    p = jnp.exp(s - m)
    den = jnp.sum(p, axis=-1, keepdims=True)
    o = jnp.einsum("bsjhqk,bsjkhd->bsjqhd", (p / den).astype(v.dtype), vw)
    lse = (m + jnp.log(den))[..., 0].transpose(0, 1, 2, 4, 3)

    def unstride(t):
        t = t.reshape((B, dilation, lpad) + t.shape[4:])[:, :, :L]
        perm = (0, 2, 1) + tuple(range(3, t.ndim))
        return t.transpose(perm).reshape((B, T) + t.shape[3:])

    return unstride(o), unstride(lse)


def hier_moe(x2, w_group, b_group, w_router, b_router, w_gate, w_up, w_down):
    N, D = x2.shape
    g_logits = jnp.matmul(x2, w_group, preferred_element_type=jnp.float32) + b_group.astype(jnp.float32)
    g_prob = jax.nn.softmax(g_logits, axis=-1)
    g_val, g_idx = lax.top_k(g_prob, 1)
    e_all = (jnp.matmul(x2, w_router, preferred_element_type=jnp.float32)
             .reshape(N, N_GROUPS, EXPERTS_PER_GROUP) + b_router.astype(jnp.float32))
    e_logits = jnp.take_along_axis(e_all, g_idx[:, :, None], axis=1)[:, 0]
    top_v, top_i = lax.top_k(e_logits, TOP_K)
    gate = g_val * jax.nn.softmax(top_v, axis=-1)
    expert = g_idx * EXPERTS_PER_GROUP + top_i

    n_assign = N * TOP_K
    e_flat = expert.reshape(-1)
    tok_flat = jnp.repeat(jnp.arange(N), TOP_K)
    gate_flat = gate.reshape(-1)
    order = jnp.argsort(e_flat)
    e_sorted = e_flat[order]
    counts = jax.ops.segment_sum(jnp.ones_like(e_flat), e_flat, num_segments=N_EXPERTS)
    starts = jnp.cumsum(counts) - counts
    pcounts = (counts + MOE_BLOCK - 1) // MOE_BLOCK * MOE_BLOCK
    pends = jnp.cumsum(pcounts)
    pstarts = pends - pcounts
    dest = pstarts[e_sorted] + jnp.arange(n_assign) - starts[e_sorted]
    n_blocks = -(-n_assign // MOE_BLOCK) + N_EXPERTS
    n_rows = n_blocks * MOE_BLOCK
    slot_tok = jnp.full((n_rows,), N, dtype=jnp.int32).at[dest].set(tok_flat[order].astype(jnp.int32))
    slot_gate = jnp.zeros((n_rows,), jnp.float32).at[dest].set(gate_flat[order])
    block_e = jnp.clip(jnp.searchsorted(pends, jnp.arange(n_blocks) * MOE_BLOCK, side="right"),
                       0, N_EXPERTS - 1)
    x_ext = jnp.concatenate([x2, jnp.zeros((1, D), x2.dtype)], axis=0)

    def expert_block(args):
        toks, e = args
        xb = x_ext[toks]
        hdn = jax.nn.silu(xb @ w_gate[e]) * (xb @ w_up[e])
        return hdn @ w_down[e]

    y_blocks = lax.map(expert_block, (slot_tok.reshape(n_blocks, MOE_BLOCK), block_e))
    y = jax.ops.segment_sum(y_blocks.reshape(n_rows, D).astype(jnp.float32) * slot_gate[:, None],
                            slot_tok, num_segments=N + 1)[:N]
    return y.astype(x2.dtype)


def setup_inputs(seed: int = 0) -> dict:
    key = jax.random.key(seed)
    ks = jax.random.split(key, 24)
    f32 = jnp.float32

    def nrm(k, shape, scale):
        return jax.random.normal(k, shape, f32) * scale

    D = D_MODEL
    return {
        "x": nrm(ks[0], (BATCH, SEQ, D), 1.0),
        "rms_mix": 1.0 + nrm(ks[1], (DEPTH, D), 0.02),
        "rms_ffn": 1.0 + nrm(ks[2], (DEPTH, D), 0.02),
        "rms_final": 1.0 + nrm(ks[3], (D,), 0.02),
        "w_in_ab": nrm(ks[4], (N_EVEN, D, IN_AB), D ** -0.5),
        "conv_dw_w": nrm(ks[5], (N_EVEN, CONV_WIDTH, CONV_CH), CONV_WIDTH ** -0.5),
        "conv_dw_b": nrm(ks[6], (N_EVEN, CONV_CH), 0.02),
        "conv_ln_g": 1.0 + nrm(ks[7], (N_EVEN, CONV_CH), 0.02),
        "conv_ln_b": nrm(ks[8], (N_EVEN, CONV_CH), 0.02),
        "nat_rpb": nrm(ks[9], (N_EVEN, NAT_HEADS, 2 * NAT_WIN_ROWS - 1, 2 * NAT_WIN_COLS - 1), 0.1),
        "w_out_ab": nrm(ks[10], (N_EVEN, CONV_CH + NAT_WIDTH, D), (CONV_CH + NAT_WIDTH) ** -0.5),
        "w_in_c": nrm(ks[11], (N_ODD, D, IN_C), D ** -0.5),
        "w_out_c": nrm(ks[12], (N_ODD, DIL_WIDTH, D), DIL_WIDTH ** -0.5),
        "t5_bias": nrm(ks[13], (T5_BUCKETS, N_DIL_GROUPS * DIL_HEADS), 0.1),
        "moe_w_group": nrm(ks[14], (DEPTH, D, N_GROUPS), D ** -0.5),
        "moe_b_group": nrm(ks[15], (DEPTH, N_GROUPS), 0.01),
        "moe_w_router": nrm(ks[16], (DEPTH, D, N_EXPERTS), D ** -0.5),
        "moe_b_router": nrm(ks[17], (DEPTH, N_GROUPS, EXPERTS_PER_GROUP), 0.01),
        "moe_w_gate": nrm(ks[18], (DEPTH, N_EXPERTS, D, EXPERT_HIDDEN), D ** -0.5),
        "moe_w_up": nrm(ks[19], (DEPTH, N_EXPERTS, D, EXPERT_HIDDEN), D ** -0.5),
        "moe_w_down": nrm(ks[20], (DEPTH, N_EXPERTS, EXPERT_HIDDEN, D), EXPERT_HIDDEN ** -0.5),
    }


def reference(x, rms_mix, rms_ffn, rms_final, w_in_ab, conv_dw_w, conv_dw_b, conv_ln_g, conv_ln_b,
              nat_rpb, w_out_ab, w_in_c, w_out_c, t5_bias, moe_w_group, moe_b_group, moe_w_router,
              moe_b_router, moe_w_gate, moe_w_up, moe_w_down):
    B, T, D = x.shape
    for layer in range(DEPTH):
        h = rms_norm(x, rms_mix[layer])
        if layer % 2 == 0:
            i = layer // 2
            proj = h @ w_in_ab[i]
            a_in = proj[..., :2 * CONV_CH]
            q, k, v = jnp.split(proj[..., 2 * CONV_CH:], 3, axis=-1)
            a_out = conformer_conv(a_in, conv_dw_w[i], conv_dw_b[i], conv_ln_g[i], conv_ln_b[i])
            b_out = neighbourhood_attention(q, k, v, nat_rpb[i])
            mix = jnp.concatenate([a_out, b_out.astype(a_out.dtype)], axis=-1) @ w_out_ab[i]
        else:
            i = layer // 2
            proj = (h @ w_in_c[i]).reshape(B, T, N_DIL_GROUPS, 3, DIL_HEADS, HEAD_DIM)
            outs = []
            lses = []
            for g, (win, dil) in enumerate(DIL_CONFIGS):
                o_g, l_g = dilated_window_attention(
                    proj[:, :, g, 0], proj[:, :, g, 1], proj[:, :, g, 2],
                    t5_bias[:, g * DIL_HEADS:(g + 1) * DIL_HEADS], dil, win // (2 * dil))
                outs.append(o_g)
                lses.append(l_g)
            alpha = jax.nn.softmax(jnp.stack(lses, axis=0), axis=0)
            o = jnp.einsum("gbth,gbthd->bthd", alpha, jnp.stack(outs, axis=0).astype(jnp.float32))
            mix = o.astype(h.dtype).reshape(B, T, DIL_WIDTH) @ w_out_c[i]
        x = x + mix.astype(x.dtype)
        h = rms_norm(x, rms_ffn[layer])
        y = hier_moe(h.reshape(B * T, D), moe_w_group[layer], moe_b_group[layer], moe_w_router[layer],
                     moe_b_router[layer], moe_w_gate[layer], moe_w_up[layer], moe_w_down[layer])
        x = x + y.reshape(B, T, D).astype(x.dtype)
    return rms_norm(x, rms_final)
```

```python
import functools
import math

import jax
import jax.numpy as jnp
from jax import lax
from jax.experimental import pallas as pl
from jax.experimental.pallas import tpu as pltpu

F32 = jnp.float32
BF16 = jnp.bfloat16

HEAD_DIM = 128
CONV_WIDTH = 31
CONV_PAD = CONV_WIDTH // 2
NAT_WIN_ROWS = 8
NAT_WIN_COLS = 16
GRID_W = 64
DIL_CONFIGS = ((128, 1), (512, 4), (2048, 16))
T5_BUCKETS = 32
T5_MAX_DIST = 1024
N_GROUPS = 4
EXPERTS_PER_GROUP = 8
N_EXPERTS = N_GROUPS * EXPERTS_PER_GROUP
TOP_K = 2
RMS_EPS = 1e-6
LN_EPS = 1e-5
NEG = -1e30

VMEM_LIMIT = 56 * 1024 * 1024
BF16_SUBLANES = 16

MM_TM = 1024
MM_TN = 512
NORM_TM = 256
CONV_TT = 128
CONV_RB = 32
CONV_CB = 512
NAT_QROWS = 8
NAT_KROWS = 16
DIL_TQ = 256
MOE_TM = 256
COMB_TM = 128
ROUTER_TM = 256


def _cparams(sem):
    return pltpu.CompilerParams(dimension_semantics=sem, vmem_limit_bytes=VMEM_LIMIT)


def _rmsnorm_kernel(x_ref, g_ref, o_ref):
    x = x_ref[...]
    ms = jnp.mean(x * x, axis=-1, keepdims=True)
    o_ref[...] = (x * lax.rsqrt(ms + RMS_EPS) * g_ref[...]).astype(o_ref.dtype)


def rmsnorm(x2, g, out_dtype):
    n, d = x2.shape
    return pl.pallas_call(
        _rmsnorm_kernel,
        grid=(n // NORM_TM,),
        in_specs=[pl.BlockSpec((NORM_TM, d), lambda i: (i, 0)),
                  pl.BlockSpec((1, d), lambda i: (0, 0))],
        out_specs=pl.BlockSpec((NORM_TM, d), lambda i: (i, 0)),
        out_shape=jax.ShapeDtypeStruct((n, d), out_dtype),
        compiler_params=_cparams(("arbitrary",)),
        name="rmsnorm",
    )(x2, g.reshape(1, d))


def _mm_kernel(*refs, n_x, has_res):
    x_refs = refs[:n_x]
    w_refs = refs[n_x:2 * n_x]
    pos = 2 * n_x
    r_ref = refs[pos] if has_res else None
    pos += int(has_res)
    o_ref, wbf_ref = refs[pos], refs[pos + 1]

    @pl.when(pl.program_id(1) == 0)
    def _():
        off = 0
        for w_ref in w_refs:
            kk = w_ref.shape[0]
            wbf_ref[off:off + kk, :] = w_ref[...].astype(BF16)
            off += kk

    acc = None
    off = 0
    for x_ref in x_refs:
        kk = x_ref.shape[1]
        part = jnp.dot(x_ref[...], wbf_ref[off:off + kk, :], preferred_element_type=F32)
        acc = part if acc is None else acc + part
        off += kk
    if has_res:
        acc = acc + r_ref[...]
    o_ref[...] = acc.astype(o_ref.dtype)


def matmul(xs, w, out_dtype, residual=None, name="matmul"):
    m = xs[0].shape[0]
    ks = [x.shape[1] for x in xs]
    ktot, nc = w.shape
    assert sum(ks) == ktot and all(k == ks[0] for k in ks)
    tm, tn = min(MM_TM, m), min(MM_TN, nc)
    assert m % tm == 0 and nc % tn == 0
    in_specs = [pl.BlockSpec((tm, k), lambda j, i: (i, 0)) for k in ks]
    in_specs += [pl.BlockSpec((k, tn), functools.partial(lambda j, i, p: (p, j), p=p))
                 for p, k in enumerate(ks)]
    args = list(xs) + [w] * len(xs)
    if residual is not None:
        in_specs.append(pl.BlockSpec((tm, tn), lambda j, i: (i, j)))
        args.append(residual)
    return pl.pallas_call(
        functools.partial(_mm_kernel, n_x=len(xs), has_res=residual is not None),
        grid=(nc // tn, m // tm),
        in_specs=in_specs,
        out_specs=pl.BlockSpec((tm, tn), lambda j, i: (i, j)),
        out_shape=jax.ShapeDtypeStruct((m, nc), out_dtype),
        scratch_shapes=[pltpu.VMEM((ktot, tn), BF16)],
        compiler_params=_cparams(("arbitrary", "arbitrary")),
        name=name,
    )(*args)


def _conv_kernel(ac_ref, gc_ref, ap_ref, gp_ref, an_ref, gn_ref, w_ref, b_ref, lg_ref, lb_ref,
                 o_ref, buf_ref, cv_ref, *, tt, ch):
    i = pl.program_id(1)
    last = pl.num_programs(1) - 1
    halo = BF16_SUBLANES

    def glu(a_ref, g_ref):
        return a_ref[0].astype(F32) * jax.nn.sigmoid(g_ref[0].astype(F32))

    buf_ref[0:halo, :] = jnp.where(i > 0, glu(ap_ref, gp_ref), 0.0)
    buf_ref[halo:halo + tt, :] = glu(ac_ref, gc_ref)
    buf_ref[halo + tt:2 * halo + tt, :] = jnp.where(i < last, glu(an_ref, gn_ref), 0.0)

    base = halo - CONV_PAD
    for cb in range(ch // CONV_CB):
        cs = slice(cb * CONV_CB, (cb + 1) * CONV_CB)
        for rb in range(tt // CONV_RB):
            r0 = rb * CONV_RB + base
            acc = jnp.zeros((CONV_RB, CONV_CB), F32)
            for j in range(CONV_WIDTH):
                acc = acc + w_ref[j:j + 1, cs] * buf_ref[r0 + j:r0 + j + CONV_RB, cs]
            cv_ref[rb * CONV_RB:(rb + 1) * CONV_RB, cs] = acc + b_ref[:, cs]

    a = cv_ref[...]
    mu = jnp.mean(a, axis=-1, keepdims=True)
    ctr = a - mu
    var = jnp.mean(ctr * ctr, axis=-1, keepdims=True)
    y = ctr * lax.rsqrt(var + LN_EPS) * lg_ref[...] + lb_ref[...]
    o_ref[0] = (y * jax.nn.sigmoid(y)).astype(o_ref.dtype)


def conformer_conv(proj3, w_dw, b_dw, ln_g, ln_b):
    b, t, _ = proj3.shape
    ch = w_dw.shape[1]
    tt, halo = CONV_TT, BF16_SUBLANES
    hb = tt // halo
    n_hb = t // halo
    cur = lambda col: pl.BlockSpec((1, tt, ch), lambda bi, i: (bi, i, col))
    prv = lambda col: pl.BlockSpec((1, halo, ch), lambda bi, i: (bi, jnp.maximum(i * hb - 1, 0), col))
    nxt = lambda col: pl.BlockSpec((1, halo, ch), lambda bi, i: (bi, jnp.minimum((i + 1) * hb, n_hb - 1), col))
    vec = lambda rows: pl.BlockSpec((rows, ch), lambda bi, i: (0, 0))
    return pl.pallas_call(
        functools.partial(_conv_kernel, tt=tt, ch=ch),
        grid=(b, t // tt),
        in_specs=[cur(0), cur(1), prv(0), prv(1), nxt(0), nxt(1),
                  vec(CONV_WIDTH), vec(1), vec(1), vec(1)],
        out_specs=pl.BlockSpec((1, tt, ch), lambda bi, i: (bi, i, 0)),
        out_shape=jax.ShapeDtypeStruct((b, t, ch), BF16),
        scratch_shapes=[pltpu.VMEM((tt + 2 * halo, ch), F32), pltpu.VMEM((tt, ch), F32)],
        compiler_params=_cparams(("arbitrary", "arbitrary")),
        name="conformer_conv",
    )(proj3, proj3, proj3, proj3, proj3, proj3, w_dw, b_dw.reshape(1, ch), ln_g.reshape(1, ch),
      ln_b.reshape(1, ch))


def _nat_bias_table(rpb):
    h = rpb.shape[0]
    kc = NAT_WIN_COLS
    cols = jnp.arange(GRID_W)
    col_start = jnp.clip(cols - kc // 2, 0, GRID_W - kc)
    col_mask = (cols[None, :] >= col_start[:, None]) & (cols[None, :] < col_start[:, None] + kc)
    dc_idx = jnp.clip(cols[None, :] - cols[:, None], -(kc - 1), kc - 1) + (kc - 1)
    bc = jnp.where(col_mask[None, None], rpb[:, :, dc_idx].astype(F32), NEG)
    blank = jnp.full((h, 1, GRID_W, GRID_W), NEG, F32)
    bc = jnp.concatenate([blank, bc, blank], axis=1)
    return jnp.concatenate([bc[:, :-1], bc[:, 1:]], axis=-1)


def _nat_kernel(q_ref, k_ref, v_ref, tab_ref, o_ref, bias_ref, *, rows):
    blk = pl.program_id(2)
    r_base = blk * NAT_QROWS
    ws = jnp.clip(r_base - NAT_WIN_ROWS // 2, 0, rows - NAT_KROWS)
    lane = lax.broadcasted_iota(jnp.int32, (GRID_W, 2 * GRID_W), 1)
    for qr in range(NAT_QROWS):
        r = r_base + qr
        r0 = jnp.clip(r - NAT_WIN_ROWS // 2, 0, rows - NAT_WIN_ROWS)
        for p in range(NAT_KROWS // 2):
            kr = ws + 2 * p
            entry = jnp.clip(kr - r + NAT_WIN_ROWS, 0, 2 * NAT_WIN_ROWS - 1)
            ok0 = ((kr >= r0) & (kr < r0 + NAT_WIN_ROWS)).astype(jnp.int32)
            ok1 = ((kr + 1 >= r0) & (kr + 1 < r0 + NAT_WIN_ROWS)).astype(jnp.int32)
            ok = jnp.where(lane < GRID_W, ok0, ok1) > 0
            bias_ref[qr * GRID_W:(qr + 1) * GRID_W, p * 2 * GRID_W:(p + 1) * 2 * GRID_W] = (
                jnp.where(ok, tab_ref[0, entry], NEG))

    start = pl.multiple_of(ws * GRID_W, 256)
    k = k_ref[0, pl.ds(start, NAT_KROWS * GRID_W), :]
    v = v_ref[0, pl.ds(start, NAT_KROWS * GRID_W), :]
    s = lax.dot_general(q_ref[0], k, (((1,), (1,)), ((), ())), preferred_element_type=F32)
    s = s * (HEAD_DIM ** -0.5) + bias_ref[...]
    m = jnp.max(s, axis=-1, keepdims=True)
    p = jnp.exp(s - m)
    den = jnp.sum(p, axis=-1, keepdims=True)
    o = jnp.dot(p.astype(BF16), v, preferred_element_type=F32)
    o_ref[0] = (o / den).astype(o_ref.dtype)


def neighbourhood_attention(proj3, rpb, col0):
    b, t, _ = proj3.shape
    nh = rpb.shape[0]
    rows = t // GRID_W
    assert rows >= NAT_KROWS and rows % NAT_QROWS == 0
    tq = NAT_QROWS * GRID_W
    tab = _nat_bias_table(rpb)
    return pl.pallas_call(
        functools.partial(_nat_kernel, rows=rows),
        grid=(nh, b, rows // NAT_QROWS),
        in_specs=[pl.BlockSpec((1, tq, HEAD_DIM), lambda h, bi, i: (bi, i, col0 + h)),
                  pl.BlockSpec((1, t, HEAD_DIM), lambda h, bi, i: (bi, 0, col0 + nh + h)),
                  pl.BlockSpec((1, t, HEAD_DIM), lambda h, bi, i: (bi, 0, col0 + 2 * nh + h)),
                  pl.BlockSpec((1, 2 * NAT_WIN_ROWS, GRID_W, 2 * GRID_W), lambda h, bi, i: (h, 0, 0, 0))],
        out_specs=pl.BlockSpec((1, tq, HEAD_DIM), lambda h, bi, i: (bi, i, h)),
        out_shape=jax.ShapeDtypeStruct((b, t, nh * HEAD_DIM), BF16),
        scratch_shapes=[pltpu.VMEM((tq, NAT_KROWS * GRID_W), F32)],
        compiler_params=_cparams(("arbitrary", "arbitrary", "arbitrary")),
        name="neighbourhood_attention",
    )(proj3, proj3, proj3, tab)


def _t5_bucket(rel):
    nb = T5_BUCKETS // 2
    max_exact = nb // 2
    n = jnp.abs(rel)
    sign = jnp.where(rel > 0, nb, 0)
    nf = jnp.maximum(n, 1).astype(F32)
    large = max_exact + (jnp.log(nf / max_exact) / math.log(T5_MAX_DIST / max_exact)
                         * (nb - max_exact)).astype(jnp.int32)
    large = jnp.minimum(large, nb - 1)
    return sign + jnp.where(n < max_exact, n, large)


def _dil_halo(g):
    win, dil = DIL_CONFIGS[g]
    return win // 2


def _dil_bias_table(t5_bias, g, nh):
    win, dil = DIL_CONFIGS[g]
    halo = _dil_halo(g)
    q = jnp.arange(DIL_TQ)[:, None]
    j = jnp.arange(DIL_TQ + 2 * halo)[None, :]
    delta = j - halo - q
    on = (delta % dil == 0) & (jnp.abs(delta) <= halo)
    vals = t5_bias[:, g * nh:(g + 1) * nh][_t5_bucket(delta)].astype(F32)
    return jnp.where(on[None], vals.transpose(2, 0, 1), NEG)


def _dil_kernel(q0_ref, q1_ref, q2_ref, b0_ref, b1_ref, b2_ref, proj_ref, o_ref,
                k0, v0, k1, v1, k2, v2, sem, *, t, nh):
    h, bi, i = pl.program_id(0), pl.program_id(1), pl.program_id(2)
    kv = ((k0, v0), (k1, v1), (k2, v2))

    def copies():
        out = []
        for g in range(3):
            halo = _dil_halo(g)
            for which in range(2):
                col = pl.multiple_of(((g * 3 + 1 + which) * nh + h) * HEAD_DIM, HEAD_DIM)
                out.append(pltpu.make_async_copy(
                    proj_ref.at[bi, :, pl.ds(col, HEAD_DIM)],
                    kv[g][which].at[pl.ds(halo, t), :],
                    sem.at[g * 2 + which]))
        return out

    @pl.when(i == 0)
    def _():
        cps = copies()
        for cp in cps:
            cp.start()
        for g in range(3):
            halo = _dil_halo(g)
            for ref in kv[g]:
                ref[0:halo, :] = jnp.zeros((halo, HEAD_DIM), BF16)
                ref[halo + t:2 * halo + t, :] = jnp.zeros((halo, HEAD_DIM), BF16)
        for cp in cps:
            cp.wait()

    t0 = pl.multiple_of(i * DIL_TQ, DIL_TQ)
    scores = []
    for g, (q_ref, b_ref) in enumerate(((q0_ref, b0_ref), (q1_ref, b1_ref), (q2_ref, b2_ref))):
        halo = _dil_halo(g)
        w = DIL_TQ + 2 * halo
        k = kv[g][0][pl.ds(t0, w), :]
        s = lax.dot_general(q_ref[0], k, (((1,), (1,)), ((), ())), preferred_element_type=F32)
        kpos = lax.broadcasted_iota(jnp.int32, (1, w), 1) + (t0 - halo)
        kmask = jnp.where((kpos >= 0) & (kpos < t), 0.0, NEG)
        scores.append(s * (HEAD_DIM ** -0.5) + b_ref[0] + kmask)

    m = functools.reduce(jnp.maximum, [jnp.max(s, axis=-1, keepdims=True) for s in scores])
    num = jnp.zeros((DIL_TQ, HEAD_DIM), F32)
    den = jnp.zeros((DIL_TQ, 1), F32)
    for g, s in enumerate(scores):
        halo = _dil_halo(g)
        p = jnp.exp(s - m)
        den = den + jnp.sum(p, axis=-1, keepdims=True)
        v = kv[g][1][pl.ds(t0, DIL_TQ + 2 * halo), :]
        num = num + jnp.dot(p.astype(BF16), v, preferred_element_type=F32)
    o_ref[0] = (num / den).astype(o_ref.dtype)


def dilated_attention(proj3, t5_bias, nh):
    b, t, _ = proj3.shape
    assert t % DIL_TQ == 0
    tabs = [_dil_bias_table(t5_bias, g, nh) for g in range(3)]
    qspec = lambda g: pl.BlockSpec((1, DIL_TQ, HEAD_DIM), lambda h, bi, i: (bi, i, g * 3 * nh + h))
    bspec = lambda g: pl.BlockSpec((1, DIL_TQ, DIL_TQ + 2 * _dil_halo(g)), lambda h, bi, i: (h, 0, 0))
    scratch = []
    for g in range(3):
        scratch += [pltpu.VMEM((t + 2 * _dil_halo(g), HEAD_DIM), BF16)] * 2
    scratch.append(pltpu.SemaphoreType.DMA((6,)))
    return pl.pallas_call(
        functools.partial(_dil_kernel, t=t, nh=nh),
        grid=(nh, b, t // DIL_TQ),
        in_specs=[qspec(0), qspec(1), qspec(2), bspec(0), bspec(1), bspec(2),
                  pl.BlockSpec(memory_space=pl.ANY)],
        out_specs=pl.BlockSpec((1, DIL_TQ, HEAD_DIM), lambda h, bi, i: (bi, i, h)),
        out_shape=jax.ShapeDtypeStruct((b, t, nh * HEAD_DIM), BF16),
        scratch_shapes=scratch,
        compiler_params=_cparams(("arbitrary", "arbitrary", "arbitrary")),
        name="dilated_attention",
    )(proj3, proj3, proj3, tabs[0], tabs[1], tabs[2], proj3)


ROUTER_LANES = 128


def _router_kernel(x_ref, g_ref, whi_ref, wlo_ref, b_ref, o_ref):
    x = x_ref[...]
    ms = jnp.mean(x * x, axis=-1, keepdims=True)
    h = x * lax.rsqrt(ms + RMS_EPS) * g_ref[...]
    h_hi = h.astype(BF16)
    h_lo = (h - h_hi.astype(F32)).astype(BF16)
    logits = (jnp.dot(h_hi, whi_ref[...], preferred_element_type=F32)
              + jnp.dot(h_lo, whi_ref[...], preferred_element_type=F32)
              + jnp.dot(h_hi, wlo_ref[...], preferred_element_type=F32)) + b_ref[...]

    lane = lax.broadcasted_iota(jnp.int32, logits.shape, 1)
    big = jnp.int32(ROUTER_LANES)

    def first_argmax(vals):
        top = jnp.max(vals, axis=-1, keepdims=True)
        return top, jnp.min(jnp.where(vals == top, lane, big), axis=-1, keepdims=True)

    gl = jnp.where(lane < N_GROUPS, logits, NEG)
    g_top, g_idx = first_argmax(gl)
    g_val = 1.0 / jnp.sum(jnp.exp(gl - g_top), axis=-1, keepdims=True)
    in_group = (lane >= N_GROUPS) & (lane < N_GROUPS + N_EXPERTS) & (
        lax.shift_right_arithmetic(lane - N_GROUPS, jnp.int32(3)) == g_idx)
    el = jnp.where(in_group, logits, NEG)
    v1, i1 = first_argmax(el)
    el2 = jnp.where(lane == i1, NEG, el)
    v2, i2 = first_argmax(el2)
    e21 = jnp.exp(v2 - v1)
    gate1 = g_val / (1.0 + e21)
    gate2 = g_val * e21 / (1.0 + e21)
    out = jnp.where(lane == 0, gate1, 0.0)
    out = jnp.where(lane == 1, gate2, out)
    out = jnp.where(lane == 2, (i1 - N_GROUPS).astype(F32), out)
    out = jnp.where(lane == 3, (i2 - N_GROUPS).astype(F32), out)
    o_ref[...] = out


def router(x2, gain, w_group, b_group, w_router, b_router):
    n, d = x2.shape
    wcat = jnp.concatenate([w_group, w_router], axis=1).astype(F32)
    wcat = jnp.pad(wcat, ((0, 0), (0, ROUTER_LANES - wcat.shape[1])))
    w_hi = wcat.astype(BF16)
    w_lo = (wcat - w_hi.astype(F32)).astype(BF16)
    bias = jnp.concatenate([b_group.reshape(-1), b_router.reshape(-1)]).astype(F32)
    bias = jnp.pad(bias, (0, ROUTER_LANES - bias.shape[0])).reshape(1, ROUTER_LANES)
    tm = ROUTER_TM
    return pl.pallas_call(
        _router_kernel,
        grid=(n // tm,),
        in_specs=[pl.BlockSpec((tm, d), lambda i: (i, 0)),
                  pl.BlockSpec((1, d), lambda i: (0, 0)),
                  pl.BlockSpec((d, ROUTER_LANES), lambda i: (0, 0)),
                  pl.BlockSpec((d, ROUTER_LANES), lambda i: (0, 0)),
                  pl.BlockSpec((1, ROUTER_LANES), lambda i: (0, 0))],
        out_specs=pl.BlockSpec((tm, ROUTER_LANES), lambda i: (i, 0)),
        out_shape=jax.ShapeDtypeStruct((n, ROUTER_LANES), F32),
        compiler_params=_cparams(("arbitrary",)),
        name="moe_router",
    )(x2, gain.reshape(1, d), w_hi, w_lo, bias)


def dispatch(route, n):
    gate = route[:, 0:TOP_K].reshape(-1)
    e_flat = route[:, TOP_K:2 * TOP_K].astype(jnp.int32).reshape(-1)
    n_assign = n * TOP_K
    n_blocks = n_assign // MOE_TM + N_EXPERTS
    n_rows = n_blocks * MOE_TM
    order = jnp.argsort(e_flat, stable=True).astype(jnp.int32)
    e_sorted = e_flat[order]
    counts = jnp.zeros((N_EXPERTS,), jnp.int32).at[e_flat].add(1)
    starts = jnp.cumsum(counts) - counts
    pcounts = (counts + MOE_TM - 1) // MOE_TM * MOE_TM
    pends = jnp.cumsum(pcounts)
    pstarts = pends - pcounts
    dest = pstarts[e_sorted] + jnp.arange(n_assign, dtype=jnp.int32) - starts[e_sorted]
    slot_tok = jnp.zeros((n_rows,), jnp.int32).at[dest].set(order // TOP_K)
    slot_gate = jnp.zeros((n_rows,), F32).at[dest].set(gate[order])
    slot_of = jnp.zeros((n_assign,), jnp.int32).at[order].set(dest)
    blk_start = jnp.arange(n_blocks, dtype=jnp.int32) * MOE_TM
    block_e = jnp.clip(jnp.searchsorted(pends, blk_start, side="right"), 0, N_EXPERTS - 1).astype(jnp.int32)
    block_on = (blk_start < pends[-1]).astype(jnp.int32)
    return slot_tok, slot_gate, slot_of, block_e, block_on


def _expert_kernel(tok_ref, be_ref, on_ref, x_hbm, gate_ref, gain_ref, wg_ref, wu_ref, wd_ref,
                   o_ref, xbuf, sem):
    b = pl.program_id(0)
    nb = pl.num_programs(0)
    slot = b % 2
    tm = MOE_TM

    def gather(blk, slot_):
        def body(s, carry):
            tok = tok_ref[blk * tm + s]
            pltpu.make_async_copy(x_hbm.at[pl.ds(tok, 1), :], xbuf.at[slot_, pl.ds(s, 1), :],
                                  sem.at[slot_]).start()
            return carry
        lax.fori_loop(0, tm, body, 0)

    @pl.when(b == 0)
    def _():
        gather(0, 0)

    @pl.when(b + 1 < nb)
    def _():
        gather(b + 1, 1 - slot)

    pltpu.make_async_copy(x_hbm.at[pl.ds(0, tm), :], xbuf.at[slot], sem.at[slot]).wait()

    @pl.when(on_ref[b] > 0)
    def _():
        x = xbuf[slot]
        ms = jnp.mean(x * x, axis=-1, keepdims=True)
        h = (x * lax.rsqrt(ms + RMS_EPS) * gain_ref[...]).astype(BF16)
        g = jnp.dot(h, wg_ref[0], preferred_element_type=F32)
        u = jnp.dot(h, wu_ref[0], preferred_element_type=F32)
        hdn = (g * jax.nn.sigmoid(g) * u).astype(BF16)
        y = jnp.dot(hdn, wd_ref[0], preferred_element_type=F32)
        o_ref[...] = y * gate_ref[...]

    @pl.when(on_ref[b] == 0)
    def _():
        o_ref[...] = jnp.zeros(o_ref.shape, o_ref.dtype)


def expert_blocks(x2, gain, slot_tok, slot_gate, block_e, block_on, w_gate, w_up, w_down):
    n, d = x2.shape
    n_blocks = block_e.shape[0]
    hid = w_gate.shape[-1]
    tm = MOE_TM
    grid_spec = pltpu.PrefetchScalarGridSpec(
        num_scalar_prefetch=3,
        grid=(n_blocks,),
        in_specs=[pl.BlockSpec(memory_space=pl.ANY),
                  pl.BlockSpec((tm, 1), lambda b, tok, be, on: (b, 0)),
                  pl.BlockSpec((1, d), lambda b, tok, be, on: (0, 0)),
                  pl.BlockSpec((1, d, hid), lambda b, tok, be, on: (be[b], 0, 0)),
                  pl.BlockSpec((1, d, hid), lambda b, tok, be, on: (be[b], 0, 0)),
                  pl.BlockSpec((1, hid, d), lambda b, tok, be, on: (be[b], 0, 0))],
        out_specs=pl.BlockSpec((tm, d), lambda b, tok, be, on: (b, 0)),
        scratch_shapes=[pltpu.VMEM((2, tm, d), F32), pltpu.SemaphoreType.DMA((2,))],
    )
    return pl.pallas_call(
        _expert_kernel,
        grid_spec=grid_spec,
        out_shape=jax.ShapeDtypeStruct((n_blocks * tm, d), F32),
        compiler_params=_cparams(("arbitrary",)),
        name="moe_experts",
    )(slot_tok, block_e, block_on, x2, slot_gate.reshape(-1, 1), gain.reshape(1, d),
      w_gate, w_up, w_down)


def _combine_kernel(slot_ref, x_ref, gain_ref, y_hbm, *rest, emit_x):
    if emit_x:
        xo_ref, ho_ref, ybuf, sem = rest
    else:
        ho_ref, ybuf, sem = rest
        xo_ref = None
    i = pl.program_id(0)
    ni = pl.num_programs(0)
    slot = i % 2
    tm = COMB_TM

    def gather(blk, slot_):
        def body(s, carry):
            for k in range(TOP_K):
                row = slot_ref[(blk * tm + s) * TOP_K + k]
                pltpu.make_async_copy(y_hbm.at[pl.ds(row, 1), :], ybuf.at[slot_, k, pl.ds(s, 1), :],
                                      sem.at[slot_]).start()
            return carry
        lax.fori_loop(0, tm, body, 0)

    @pl.when(i == 0)
    def _():
        gather(0, 0)

    @pl.when(i + 1 < ni)
    def _():
        gather(i + 1, 1 - slot)

    for k in range(TOP_K):
        pltpu.make_async_copy(y_hbm.at[pl.ds(0, tm), :], ybuf.at[slot, k], sem.at[slot]).wait()

    x = x_ref[...] + (ybuf[slot, 0] + ybuf[slot, 1])
    if emit_x:
        xo_ref[...] = x
    ms = jnp.mean(x * x, axis=-1, keepdims=True)
    ho_ref[...] = (x * lax.rsqrt(ms + RMS_EPS) * gain_ref[...]).astype(ho_ref.dtype)


def combine(x2, y_rows, slot_of, gain, emit_x, h_dtype):
    n, d = x2.shape
    tm = COMB_TM
    row = lambda i, slots: (i, 0)
    out_shape = [jax.ShapeDtypeStruct((n, d), h_dtype)]
    out_specs = [pl.BlockSpec((tm, d), row)]
    if emit_x:
        out_shape.insert(0, jax.ShapeDtypeStruct((n, d), F32))
        out_specs.insert(0, pl.BlockSpec((tm, d), row))
    grid_spec = pltpu.PrefetchScalarGridSpec(
        num_scalar_prefetch=1,
        grid=(n // tm,),
        in_specs=[pl.BlockSpec((tm, d), row),
                  pl.BlockSpec((1, d), lambda i, slots: (0, 0)),
                  pl.BlockSpec(memory_space=pl.ANY)],
        out_specs=out_specs,
        scratch_shapes=[pltpu.VMEM((2, TOP_K, tm, d), F32), pltpu.SemaphoreType.DMA((2,))],
    )
    return pl.pallas_call(
        functools.partial(_combine_kernel, emit_x=emit_x),
        grid_spec=grid_spec,
        out_shape=out_shape,
        compiler_params=_cparams(("arbitrary",)),
        name="moe_combine",
    )(slot_of, x2, gain.reshape(1, d), y_rows)


def moe_layer(x2, ffn_gain, next_gain, emit_x, h_dtype, w_group, b_group, w_router, b_router,
              w_gate, w_up, w_down):
    n = x2.shape[0]
    route = router(x2, ffn_gain, w_group, b_group, w_router, b_router)
    slot_tok, slot_gate, slot_of, block_e, block_on = dispatch(route, n)
    y_rows = expert_blocks(x2, ffn_gain, slot_tok, slot_gate, block_e, block_on,
                           w_gate.astype(BF16), w_up.astype(BF16), w_down.astype(BF16))
    return combine(x2, y_rows, slot_of, next_gain, emit_x, h_dtype)


def kernel(x, rms_mix, rms_ffn, rms_final, w_in_ab, conv_dw_w, conv_dw_b, conv_ln_g, conv_ln_b, nat_rpb,
           w_out_ab, w_in_c, w_out_c, t5_bias, moe_w_group, moe_b_group, moe_w_router, moe_b_router,
           moe_w_gate, moe_w_up, moe_w_down):
    b, t, d = x.shape
    n = b * t
    x2 = x.reshape(n, d)
    conv_ch = conv_dw_w.shape[-1]
    nat_heads = nat_rpb.shape[1]
    dil_heads = w_out_c.shape[1] // HEAD_DIM

    h = rmsnorm(x2, rms_mix[0], BF16)
    proj = matmul([h], w_in_ab[0], BF16, name="in_proj_ab").reshape(b, t, -1)
    a_out = conformer_conv(proj, conv_dw_w[0], conv_dw_b[0], conv_ln_g[0], conv_ln_b[0])
    b_out = neighbourhood_attention(proj, nat_rpb[0], 2 * conv_ch // HEAD_DIM)
    x2 = matmul([a_out.reshape(n, -1), b_out.reshape(n, -1)], w_out_ab[0], F32, residual=x2,
                name="out_proj_ab")
    x2, h = moe_layer(x2, rms_ffn[0], rms_mix[1], True, BF16, moe_w_group[0], moe_b_group[0],
                      moe_w_router[0], moe_b_router[0], moe_w_gate[0], moe_w_up[0], moe_w_down[0])

    proj = matmul([h], w_in_c[0], BF16, name="in_proj_c").reshape(b, t, -1)
    o = dilated_attention(proj, t5_bias, dil_heads)
    x2 = matmul([o.reshape(n, -1)], w_out_c[0], F32, residual=x2, name="out_proj_c")
    (y,) = moe_layer(x2, rms_ffn[1], rms_final, False, x.dtype, moe_w_group[1], moe_b_group[1],
                     moe_w_router[1], moe_b_router[1], moe_w_gate[1], moe_w_up[1], moe_w_down[1])
    return y.reshape(b, t, d)
```

```python
import functools
import math

import jax
import jax.numpy as jnp
from jax import lax
from jax.experimental import pallas as pl
from jax.experimental.pallas import tpu as pltpu

F32 = jnp.float32
BF16 = jnp.bfloat16

HEAD_DIM = 128
CONV_WIDTH = 31
CONV_PAD = CONV_WIDTH // 2
NAT_WIN_ROWS = 8
NAT_WIN_COLS = 16
GRID_W = 64
DIL_CONFIGS = ((128, 1), (512, 4), (2048, 16))
T5_BUCKETS = 32
T5_MAX_DIST = 1024
N_GROUPS = 4
EXPERTS_PER_GROUP = 8
N_EXPERTS = N_GROUPS * EXPERTS_PER_GROUP
TOP_K = 2
RMS_EPS = 1e-6
LN_EPS = 1e-5
NEG = -1e30

VMEM_LIMIT = 56 * 1024 * 1024
BF16_SUBLANES = 16

MM_TM = 1024
MM_TN = 512
NORM_TM = 256
CONV_TT = 128
CONV_RB = 32
CONV_CB = 512
NAT_QROWS = 8
NAT_KROWS = 16
DIL_TQ = 256
MOE_TM = 256
SCAT_TM = 512
COMB_TM = 128
ROUTER_TM = 256


def _cparams(sem):
    return pltpu.CompilerParams(dimension_semantics=sem, vmem_limit_bytes=VMEM_LIMIT)


def _rmsnorm_kernel(x_ref, g_ref, o_ref):
    x = x_ref[...]
    ms = jnp.mean(x * x, axis=-1, keepdims=True)
    o_ref[...] = (x * lax.rsqrt(ms + RMS_EPS) * g_ref[...]).astype(o_ref.dtype)


def rmsnorm(x2, g, out_dtype):
    n, d = x2.shape
    return pl.pallas_call(
        _rmsnorm_kernel,
        grid=(n // NORM_TM,),
        in_specs=[pl.BlockSpec((NORM_TM, d), lambda i: (i, 0)),
                  pl.BlockSpec((1, d), lambda i: (0, 0))],
        out_specs=pl.BlockSpec((NORM_TM, d), lambda i: (i, 0)),
        out_shape=jax.ShapeDtypeStruct((n, d), out_dtype),
        compiler_params=_cparams(("arbitrary",)),
        name="rmsnorm",
    )(x2, g.reshape(1, d))


def _mm_kernel(*refs, n_x, has_res):
    x_refs = refs[:n_x]
    w_refs = refs[n_x:2 * n_x]
    pos = 2 * n_x
    r_ref = refs[pos] if has_res else None
    pos += int(has_res)
    o_ref, wbf_ref = refs[pos], refs[pos + 1]

    @pl.when(pl.program_id(1) == 0)
    def _():
        off = 0
        for w_ref in w_refs:
            kk = w_ref.shape[0]
            wbf_ref[off:off + kk, :] = w_ref[...].astype(BF16)
            off += kk

    acc = None
    off = 0
    for x_ref in x_refs:
        kk = x_ref.shape[1]
        part = jnp.dot(x_ref[...], wbf_ref[off:off + kk, :], preferred_element_type=F32)
        acc = part if acc is None else acc + part
        off += kk
    if has_res:
        acc = acc + r_ref[...]
    o_ref[...] = acc.astype(o_ref.dtype)


def matmul(xs, w, out_dtype, residual=None, name="matmul"):
    m = xs[0].shape[0]
    ks = [x.shape[1] for x in xs]
    ktot, nc = w.shape
    assert sum(ks) == ktot and all(k == ks[0] for k in ks)
    tm, tn = min(MM_TM, m), min(MM_TN, nc)
    assert m % tm == 0 and nc % tn == 0
    in_specs = [pl.BlockSpec((tm, k), lambda j, i: (i, 0)) for k in ks]
    in_specs += [pl.BlockSpec((k, tn), functools.partial(lambda j, i, p: (p, j), p=p))
                 for p, k in enumerate(ks)]
    args = list(xs) + [w] * len(xs)
    if residual is not None:
        in_specs.append(pl.BlockSpec((tm, tn), lambda j, i: (i, j)))
        args.append(residual)
    return pl.pallas_call(
        functools.partial(_mm_kernel, n_x=len(xs), has_res=residual is not None),
        grid=(nc // tn, m // tm),
        in_specs=in_specs,
        out_specs=pl.BlockSpec((tm, tn), lambda j, i: (i, j)),
        out_shape=jax.ShapeDtypeStruct((m, nc), out_dtype),
        scratch_shapes=[pltpu.VMEM((ktot, tn), BF16)],
        compiler_params=_cparams(("arbitrary", "arbitrary")),
        name=name,
    )(*args)


def _conv_kernel(ac_ref, gc_ref, ap_ref, gp_ref, an_ref, gn_ref, w_ref, b_ref, lg_ref, lb_ref,
                 o_ref, buf_ref, cv_ref, *, tt, ch):
    i = pl.program_id(1)
    last = pl.num_programs(1) - 1
    halo = BF16_SUBLANES

    def glu(a_ref, g_ref):
        return a_ref[0].astype(F32) * jax.nn.sigmoid(g_ref[0].astype(F32))

    buf_ref[0:halo, :] = jnp.where(i > 0, glu(ap_ref, gp_ref), 0.0)
    buf_ref[halo:halo + tt, :] = glu(ac_ref, gc_ref)
    buf_ref[halo + tt:2 * halo + tt, :] = jnp.where(i < last, glu(an_ref, gn_ref), 0.0)

    base = halo - CONV_PAD
    for cb in range(ch // CONV_CB):
        cs = slice(cb * CONV_CB, (cb + 1) * CONV_CB)
        for rb in range(tt // CONV_RB):
            r0 = rb * CONV_RB + base
            acc = jnp.zeros((CONV_RB, CONV_CB), F32)
            for j in range(CONV_WIDTH):
                acc = acc + w_ref[j:j + 1, cs] * buf_ref[r0 + j:r0 + j + CONV_RB, cs]
            cv_ref[rb * CONV_RB:(rb + 1) * CONV_RB, cs] = acc + b_ref[:, cs]

    a = cv_ref[...]
    mu = jnp.mean(a, axis=-1, keepdims=True)
    ctr = a - mu
    var = jnp.mean(ctr * ctr, axis=-1, keepdims=True)
    y = ctr * lax.rsqrt(var + LN_EPS) * lg_ref[...] + lb_ref[...]
    o_ref[0] = (y * jax.nn.sigmoid(y)).astype(o_ref.dtype)


def conformer_conv(proj3, w_dw, b_dw, ln_g, ln_b):
    b, t, _ = proj3.shape
    ch = w_dw.shape[1]
    tt, halo = CONV_TT, BF16_SUBLANES
    hb = tt // halo
    n_hb = t // halo
    cur = lambda col: pl.BlockSpec((1, tt, ch), lambda bi, i: (bi, i, col))
    prv = lambda col: pl.BlockSpec((1, halo, ch), lambda bi, i: (bi, jnp.maximum(i * hb - 1, 0), col))
    nxt = lambda col: pl.BlockSpec((1, halo, ch), lambda bi, i: (bi, jnp.minimum((i + 1) * hb, n_hb - 1), col))
    vec = lambda rows: pl.BlockSpec((rows, ch), lambda bi, i: (0, 0))
    return pl.pallas_call(
        functools.partial(_conv_kernel, tt=tt, ch=ch),
        grid=(b, t // tt),
        in_specs=[cur(0), cur(1), prv(0), prv(1), nxt(0), nxt(1),
                  vec(CONV_WIDTH), vec(1), vec(1), vec(1)],
        out_specs=pl.BlockSpec((1, tt, ch), lambda bi, i: (bi, i, 0)),
        out_shape=jax.ShapeDtypeStruct((b, t, ch), BF16),
        scratch_shapes=[pltpu.VMEM((tt + 2 * halo, ch), F32), pltpu.VMEM((tt, ch), F32)],
        compiler_params=_cparams(("arbitrary", "arbitrary")),
        name="conformer_conv",
    )(proj3, proj3, proj3, proj3, proj3, proj3, w_dw, b_dw.reshape(1, ch), ln_g.reshape(1, ch),
      ln_b.reshape(1, ch))


def _nat_bias_table(rpb):
    h = rpb.shape[0]
    kc = NAT_WIN_COLS
    cols = jnp.arange(GRID_W)
    col_start = jnp.clip(cols - kc // 2, 0, GRID_W - kc)
    col_mask = (cols[None, :] >= col_start[:, None]) & (cols[None, :] < col_start[:, None] + kc)
    dc_idx = jnp.clip(cols[None, :] - cols[:, None], -(kc - 1), kc - 1) + (kc - 1)
    bc = jnp.where(col_mask[None, None], rpb[:, :, dc_idx].astype(F32), NEG)
    blank = jnp.full((h, 1, GRID_W, GRID_W), NEG, F32)
    bc = jnp.concatenate([blank, bc, blank], axis=1)
    return jnp.concatenate([bc[:, :-1], bc[:, 1:]], axis=-1)


def _nat_kernel(q_ref, k_ref, v_ref, tab_ref, o_ref, bias_ref, *, rows):
    blk = pl.program_id(2)
    r_base = blk * NAT_QROWS
    ws = jnp.clip(r_base - NAT_WIN_ROWS // 2, 0, rows - NAT_KROWS)
    lane = lax.broadcasted_iota(jnp.int32, (GRID_W, 2 * GRID_W), 1)
    for qr in range(NAT_QROWS):
        r = r_base + qr
        r0 = jnp.clip(r - NAT_WIN_ROWS // 2, 0, rows - NAT_WIN_ROWS)
        for p in range(NAT_KROWS // 2):
            kr = ws + 2 * p
            entry = jnp.clip(kr - r + NAT_WIN_ROWS, 0, 2 * NAT_WIN_ROWS - 1)
            ok0 = ((kr >= r0) & (kr < r0 + NAT_WIN_ROWS)).astype(jnp.int32)
            ok1 = ((kr + 1 >= r0) & (kr + 1 < r0 + NAT_WIN_ROWS)).astype(jnp.int32)
            ok = jnp.where(lane < GRID_W, ok0, ok1) > 0
            bias_ref[qr * GRID_W:(qr + 1) * GRID_W, p * 2 * GRID_W:(p + 1) * 2 * GRID_W] = (
                jnp.where(ok, tab_ref[0, entry], NEG))

    start = pl.multiple_of(ws * GRID_W, 256)
    k = k_ref[0, pl.ds(start, NAT_KROWS * GRID_W), :]
    v = v_ref[0, pl.ds(start, NAT_KROWS * GRID_W), :]
    s = lax.dot_general(q_ref[0], k, (((1,), (1,)), ((), ())), preferred_element_type=F32)
    s = s * (HEAD_DIM ** -0.5) + bias_ref[...]
    m = jnp.max(s, axis=-1, keepdims=True)
    p = jnp.exp(s - m)
    den = jnp.sum(p, axis=-1, keepdims=True)
    o = jnp.dot(p.astype(BF16), v, preferred_element_type=F32)
    o_ref[0] = (o / den).astype(o_ref.dtype)


def neighbourhood_attention(proj3, rpb, col0):
    b, t, _ = proj3.shape
    nh = rpb.shape[0]
    rows = t // GRID_W
    assert rows >= NAT_KROWS and rows % NAT_QROWS == 0
    tq = NAT_QROWS * GRID_W
    tab = _nat_bias_table(rpb)
    return pl.pallas_call(
        functools.partial(_nat_kernel, rows=rows),
        grid=(nh, b, rows // NAT_QROWS),
        in_specs=[pl.BlockSpec((1, tq, HEAD_DIM), lambda h, bi, i: (bi, i, col0 + h)),
                  pl.BlockSpec((1, t, HEAD_DIM), lambda h, bi, i: (bi, 0, col0 + nh + h)),
                  pl.BlockSpec((1, t, HEAD_DIM), lambda h, bi, i: (bi, 0, col0 + 2 * nh + h)),
                  pl.BlockSpec((1, 2 * NAT_WIN_ROWS, GRID_W, 2 * GRID_W), lambda h, bi, i: (h, 0, 0, 0))],
        out_specs=pl.BlockSpec((1, tq, HEAD_DIM), lambda h, bi, i: (bi, i, h)),
        out_shape=jax.ShapeDtypeStruct((b, t, nh * HEAD_DIM), BF16),
        scratch_shapes=[pltpu.VMEM((tq, NAT_KROWS * GRID_W), F32)],
        compiler_params=_cparams(("arbitrary", "arbitrary", "arbitrary")),
        name="neighbourhood_attention",
    )(proj3, proj3, proj3, tab)


def _t5_bucket(rel):
    nb = T5_BUCKETS // 2
    max_exact = nb // 2
    n = jnp.abs(rel)
    sign = jnp.where(rel > 0, nb, 0)
    nf = jnp.maximum(n, 1).astype(F32)
    large = max_exact + (jnp.log(nf / max_exact) / math.log(T5_MAX_DIST / max_exact)
                         * (nb - max_exact)).astype(jnp.int32)
    large = jnp.minimum(large, nb - 1)
    return sign + jnp.where(n < max_exact, n, large)


def _dil_halo(g):
    win, dil = DIL_CONFIGS[g]
    return win // 2


def _dil_bias_rows(t5_bias, g, nh):
    win, dil = DIL_CONFIGS[g]
    halo = _dil_halo(g)
    delta = jnp.arange(DIL_TQ + 2 * halo) - halo
    on = (delta % dil == 0) & (jnp.abs(delta) <= halo)
    vals = t5_bias[:, g * nh:(g + 1) * nh][_t5_bucket(delta)].astype(F32)
    return jnp.where(on[None], vals.T, NEG)[:, None, :]


def _dil_kernel(q0_ref, q1_ref, q2_ref, u0_ref, u1_ref, u2_ref, proj_ref, o_ref,
                k0, v0, k1, v1, k2, v2, b0_ref, b1_ref, b2_ref, sem, *, t, nh):
    h, bi, i = pl.program_id(0), pl.program_id(1), pl.program_id(2)
    kv = ((k0, v0), (k1, v1), (k2, v2))

    @pl.when((bi == 0) & (i == 0))
    def _():
        for u_ref, b_ref in ((u0_ref, b0_ref), (u1_ref, b1_ref), (u2_ref, b2_ref)):
            rows = jnp.broadcast_to(u_ref[0], b_ref.shape)
            b_ref[...] = pltpu.roll(rows, 0, 1, stride=1, stride_axis=0)

    def copies():
        out = []
        for g in range(3):
            halo = _dil_halo(g)
            for which in range(2):
                col = pl.multiple_of(((g * 3 + 1 + which) * nh + h) * HEAD_DIM, HEAD_DIM)
                out.append(pltpu.make_async_copy(
                    proj_ref.at[bi, :, pl.ds(col, HEAD_DIM)],
                    kv[g][which].at[pl.ds(halo, t), :],
                    sem.at[g * 2 + which]))
        return out

    @pl.when(i == 0)
    def _():
        cps = copies()
        for cp in cps:
            cp.start()
        for g in range(3):
            halo = _dil_halo(g)
            for ref in kv[g]:
                ref[0:halo, :] = jnp.zeros((halo, HEAD_DIM), BF16)
                ref[halo + t:2 * halo + t, :] = jnp.zeros((halo, HEAD_DIM), BF16)
        for cp in cps:
            cp.wait()

    t0 = pl.multiple_of(i * DIL_TQ, DIL_TQ)
    scores = []
    for g, (q_ref, b_ref) in enumerate(((q0_ref, b0_ref), (q1_ref, b1_ref), (q2_ref, b2_ref))):
        halo = _dil_halo(g)
        w = DIL_TQ + 2 * halo
        k = kv[g][0][pl.ds(t0, w), :]
        s = lax.dot_general(q_ref[0], k, (((1,), (1,)), ((), ())), preferred_element_type=F32)
        kpos = lax.broadcasted_iota(jnp.int32, (1, w), 1) + (t0 - halo)
        kmask = jnp.where((kpos >= 0) & (kpos < t), 0.0, NEG)
        scores.append(s * (HEAD_DIM ** -0.5) + b_ref[...] + kmask)

    m = functools.reduce(jnp.maximum, [jnp.max(s, axis=-1, keepdims=True) for s in scores])
    num = jnp.zeros((DIL_TQ, HEAD_DIM), F32)
    den = jnp.zeros((DIL_TQ, 1), F32)
    for g, s in enumerate(scores):
        halo = _dil_halo(g)
        p = jnp.exp(s - m)
        den = den + jnp.sum(p, axis=-1, keepdims=True)
        v = kv[g][1][pl.ds(t0, DIL_TQ + 2 * halo), :]
        num = num + jnp.dot(p.astype(BF16), v, preferred_element_type=F32)
    o_ref[0] = (num / den).astype(o_ref.dtype)


def dilated_attention(proj3, t5_bias, nh):
    b, t, _ = proj3.shape
    assert t % DIL_TQ == 0
    tabs = [_dil_bias_rows(t5_bias, g, nh) for g in range(3)]
    qspec = lambda g: pl.BlockSpec((1, DIL_TQ, HEAD_DIM), lambda h, bi, i: (bi, i, g * 3 * nh + h))
    bspec = lambda g: pl.BlockSpec((1, 1, DIL_TQ + 2 * _dil_halo(g)), lambda h, bi, i: (h, 0, 0))
    scratch = []
    for g in range(3):
        scratch += [pltpu.VMEM((t + 2 * _dil_halo(g), HEAD_DIM), BF16)] * 2
    scratch += [pltpu.VMEM((DIL_TQ, DIL_TQ + 2 * _dil_halo(g)), F32) for g in range(3)]
    scratch.append(pltpu.SemaphoreType.DMA((6,)))
    return pl.pallas_call(
        functools.partial(_dil_kernel, t=t, nh=nh),
        grid=(nh, b, t // DIL_TQ),
        in_specs=[qspec(0), qspec(1), qspec(2), bspec(0), bspec(1), bspec(2),
                  pl.BlockSpec(memory_space=pl.ANY)],
        out_specs=pl.BlockSpec((1, DIL_TQ, HEAD_DIM), lambda h, bi, i: (bi, i, h)),
        out_shape=jax.ShapeDtypeStruct((b, t, nh * HEAD_DIM), BF16),
        scratch_shapes=scratch,
        compiler_params=_cparams(("arbitrary", "arbitrary", "arbitrary")),
        name="dilated_attention",
    )(proj3, proj3, proj3, tabs[0], tabs[1], tabs[2], proj3)


ROUTER_LANES = 128


def _pack_pair(lo, hi):
    lo_bits = lax.bitcast_convert_type(lo.astype(BF16).astype(F32), jnp.uint32)
    hi_bits = lax.bitcast_convert_type(hi.astype(BF16).astype(F32), jnp.uint32)
    return lax.shift_right_logical(lo_bits, jnp.uint32(16)) | (hi_bits & jnp.uint32(0xFFFF0000))


def _unpack_pair(word):
    lo = lax.bitcast_convert_type(lax.shift_left(word, jnp.uint32(16)), F32)
    hi = lax.bitcast_convert_type(word & jnp.uint32(0xFFFF0000), F32)
    return lo, hi


def _router_kernel(x_ref, g_ref, whi_ref, wlo_ref, b_ref, o_ref, hp_ref, cnt_ref, carry_ref):
    step = pl.program_id(0)
    half = x_ref.shape[1] // 2
    x = x_ref[...]
    ms = jnp.mean(x * x, axis=-1, keepdims=True)
    h = x * lax.rsqrt(ms + RMS_EPS) * g_ref[...]
    hp_ref[...] = _pack_pair(h[:, :half], h[:, half:])
    h_hi = h.astype(BF16)
    h_lo = (h - h_hi.astype(F32)).astype(BF16)
    logits = (jnp.dot(h_hi, whi_ref[...], preferred_element_type=F32)
              + jnp.dot(h_lo, whi_ref[...], preferred_element_type=F32)
              + jnp.dot(h_hi, wlo_ref[...], preferred_element_type=F32)) + b_ref[...]

    lane = lax.broadcasted_iota(jnp.int32, logits.shape, 1)
    big = jnp.int32(ROUTER_LANES)

    def first_argmax(vals):
        top = jnp.max(vals, axis=-1, keepdims=True)
        return top, jnp.min(jnp.where(vals == top, lane, big), axis=-1, keepdims=True)

    gl = jnp.where(lane < N_GROUPS, logits, NEG)
    g_top, g_idx = first_argmax(gl)
    g_val = 1.0 / jnp.sum(jnp.exp(gl - g_top), axis=-1, keepdims=True)
    in_group = (lane >= N_GROUPS) & (lane < N_GROUPS + N_EXPERTS) & (
        lax.shift_right_arithmetic(lane - N_GROUPS, jnp.int32(3)) == g_idx)
    el = jnp.where(in_group, logits, NEG)
    v1, i1 = first_argmax(el)
    el2 = jnp.where(lane == i1, NEG, el)
    v2, i2 = first_argmax(el2)
    e21 = jnp.exp(v2 - v1)
    gate1 = g_val / (1.0 + e21)
    gate2 = g_val * e21 / (1.0 + e21)

    @pl.when(step == 0)
    def _():
        carry_ref[...] = jnp.zeros(carry_ref.shape, F32)

    e1, e2 = i1 - N_GROUPS, i2 - N_GROUPS
    oh1 = (lane == e1).astype(F32)
    oh2 = (lane == e2).astype(F32)
    both = oh1 + oh2
    tm = x.shape[0]
    tri = (lax.broadcasted_iota(jnp.int32, (tm, tm), 1) < lax.broadcasted_iota(jnp.int32, (tm, tm), 0))
    before = jnp.dot(tri.astype(BF16), both.astype(BF16), preferred_element_type=F32) + carry_ref[...]
    rank1 = jnp.sum(oh1 * before, axis=-1, keepdims=True)
    rank2 = jnp.sum(oh2 * before, axis=-1, keepdims=True)
    carry_ref[...] = carry_ref[...] + jnp.sum(both, axis=0, keepdims=True)
    cnt_ref[...] = jnp.broadcast_to(carry_ref[...], cnt_ref.shape)

    out = jnp.where(lane == 0, gate1, 0.0)
    out = jnp.where(lane == 1, gate2, out)
    out = jnp.where(lane == 2, e1.astype(F32), out)
    out = jnp.where(lane == 3, e2.astype(F32), out)
    out = jnp.where(lane == 4, rank1, out)
    out = jnp.where(lane == 5, rank2, out)
    o_ref[...] = out


def router(x2, gain, w_group, b_group, w_router, b_router):
    n, d = x2.shape
    wcat = jnp.concatenate([w_group, w_router], axis=1).astype(F32)
    wcat = jnp.pad(wcat, ((0, 0), (0, ROUTER_LANES - wcat.shape[1])))
    w_hi = wcat.astype(BF16)
    w_lo = (wcat - w_hi.astype(F32)).astype(BF16)
    bias = jnp.concatenate([b_group.reshape(-1), b_router.reshape(-1)]).astype(F32)
    bias = jnp.pad(bias, (0, ROUTER_LANES - bias.shape[0])).reshape(1, ROUTER_LANES)
    tm = ROUTER_TM
    return pl.pallas_call(
        _router_kernel,
        grid=(n // tm,),
        in_specs=[pl.BlockSpec((tm, d), lambda i: (i, 0)),
                  pl.BlockSpec((1, d), lambda i: (0, 0)),
                  pl.BlockSpec((d, ROUTER_LANES), lambda i: (0, 0)),
                  pl.BlockSpec((d, ROUTER_LANES), lambda i: (0, 0)),
                  pl.BlockSpec((1, ROUTER_LANES), lambda i: (0, 0))],
        out_specs=[pl.BlockSpec((tm, ROUTER_LANES), lambda i: (i, 0)),
                   pl.BlockSpec((tm, d // 2), lambda i: (i, 0)),
                   pl.BlockSpec((8, ROUTER_LANES), lambda i: (0, 0))],
        out_shape=[jax.ShapeDtypeStruct((n, ROUTER_LANES), F32),
                   jax.ShapeDtypeStruct((n, d // 2), jnp.uint32),
                   jax.ShapeDtypeStruct((8, ROUTER_LANES), F32)],
        scratch_shapes=[pltpu.VMEM((1, ROUTER_LANES), F32)],
        compiler_params=_cparams(("arbitrary",)),
        name="moe_router",
    )(x2, gain.reshape(1, d), w_hi, w_lo, bias)


def dispatch(route, counts, n):
    n_blocks = n * TOP_K // MOE_TM + N_EXPERTS
    counts = counts[0, :N_EXPERTS].astype(jnp.int32)
    pcounts = (counts + MOE_TM - 1) // MOE_TM * MOE_TM
    pends = jnp.cumsum(pcounts)
    pstarts = pends - pcounts
    expert = route[:, 2:2 + TOP_K].astype(jnp.int32)
    rank = route[:, 2 + TOP_K:2 + 2 * TOP_K].astype(jnp.int32)
    ids = jnp.arange(N_EXPERTS, dtype=jnp.int32)
    base = jnp.sum(jnp.where(expert[..., None] == ids, pstarts, 0), axis=-1)
    slot_of = (base + rank).reshape(-1)
    blk_start = jnp.arange(n_blocks, dtype=jnp.int32) * MOE_TM
    block_e = jnp.minimum(jnp.sum(blk_start[:, None] >= pends[None, :], axis=-1), N_EXPERTS - 1)
    n_on = (pends[-1] // MOE_TM).reshape(1)
    pad_from = pstarts + counts
    return slot_of, block_e.astype(jnp.int32), n_on.astype(jnp.int32), pad_from, pends


def _scatter_kernel(slot_ref, padfrom_ref, pend_ref, hp_hbm, zero_hbm, xs_hbm, sem, zsem):
    i = pl.program_id(0)
    tm = SCAT_TM
    n_blocks = xs_hbm.shape[0] // MOE_TM

    def pad_rows(fn):
        for e in range(N_EXPERTS):
            def body(r, carry):
                fn(pltpu.make_async_copy(zero_hbm.at[pl.ds(0, 1), :], xs_hbm.at[pl.ds(r, 1), :], zsem))
                return carry
            lax.fori_loop(padfrom_ref[e], pend_ref[e], body, 0)

        def blk(b, carry):
            start = pl.multiple_of(b * MOE_TM, MOE_TM)
            fn(pltpu.make_async_copy(zero_hbm, xs_hbm.at[pl.ds(start, MOE_TM), :], zsem))
            return carry
        lax.fori_loop(pend_ref[N_EXPERTS - 1] // MOE_TM, n_blocks, blk, 0)

    @pl.when(i == 0)
    def _():
        pad_rows(lambda cp: cp.start())

    def body(s, carry):
        t = i * tm + s
        for k in range(TOP_K):
            row = slot_ref[t * TOP_K + k]
            pltpu.make_async_copy(hp_hbm.at[pl.ds(t, 1), :], xs_hbm.at[pl.ds(row, 1), :], sem).start()
        return carry
    lax.fori_loop(0, tm, body, 0, unroll=8)
    pltpu.make_async_copy(hp_hbm.at[pl.ds(0, tm * TOP_K), :], xs_hbm.at[pl.ds(0, tm * TOP_K), :], sem).wait()

    @pl.when(i == pl.num_programs(0) - 1)
    def _():
        pad_rows(lambda cp: cp.wait())


def scatter_rows(hp, slot_of, pad_from, pends, n_rows):
    n, w = hp.shape
    grid_spec = pltpu.PrefetchScalarGridSpec(
        num_scalar_prefetch=3,
        grid=(n // SCAT_TM,),
        in_specs=[pl.BlockSpec(memory_space=pl.ANY), pl.BlockSpec(memory_space=pl.ANY)],
        out_specs=pl.BlockSpec(memory_space=pl.ANY),
        scratch_shapes=[pltpu.SemaphoreType.DMA(()), pltpu.SemaphoreType.DMA(())],
    )
    return pl.pallas_call(
        _scatter_kernel,
        grid_spec=grid_spec,
        out_shape=jax.ShapeDtypeStruct((n_rows, w), hp.dtype),
        compiler_params=_cparams(("arbitrary",)),
        name="moe_scatter",
    )(slot_of, pad_from, pends, hp, jnp.zeros((MOE_TM, w), hp.dtype))


def _expert_kernel(be_ref, non_ref, xs_ref, wg_ref, wu_ref, wd_ref, o_ref):
    on = pl.program_id(0) < non_ref[0]

    @pl.when(jnp.logical_not(on))
    def _():
        o_ref[...] = jnp.zeros(o_ref.shape, o_ref.dtype)

    @pl.when(on)
    def _():
        lo, hi = _unpack_pair(xs_ref[...])
        lo, hi = lo.astype(BF16), hi.astype(BF16)
        half = lo.shape[1]

        def proj(w_ref):
            return (jnp.dot(lo, w_ref[0, 0, :half], preferred_element_type=F32)
                    + jnp.dot(hi, w_ref[0, 0, half:], preferred_element_type=F32))

        g = proj(wg_ref)
        u = proj(wu_ref)
        hdn = (g * jax.nn.sigmoid(g) * u).astype(BF16)
        y = jnp.dot(hdn, wd_ref[0, 0], preferred_element_type=F32)
        o_ref[...] = _pack_pair(y[:, :half], y[:, half:])


def expert_blocks(xs, block_e, n_on, layer, w_gate, w_up, w_down):
    n_rows, half = xs.shape
    _, _, d, hid = w_gate.shape
    tm = MOE_TM
    wspec = lambda shape: pl.BlockSpec((1, 1) + shape, lambda b, be, non: (layer, be[b], 0, 0))
    grid_spec = pltpu.PrefetchScalarGridSpec(
        num_scalar_prefetch=2,
        grid=(n_rows // tm,),
        in_specs=[pl.BlockSpec((tm, half), lambda b, be, non: (b, 0)),
                  wspec((d, hid)), wspec((d, hid)), wspec((hid, d))],
        out_specs=pl.BlockSpec((tm, half), lambda b, be, non: (b, 0)),
    )
    return pl.pallas_call(
        _expert_kernel,
        grid_spec=grid_spec,
        out_shape=jax.ShapeDtypeStruct((n_rows, half), xs.dtype),
        compiler_params=_cparams(("arbitrary",)),
        name="moe_experts",
    )(block_e, n_on, xs, w_gate, w_up, w_down)


def _combine_kernel(slot_ref, x_ref, r_ref, gain_ref, y_hbm, *rest, emit_x):
    if emit_x:
        xo_ref, ho_ref, ybuf, sem = rest
    else:
        ho_ref, ybuf, sem = rest
        xo_ref = None
    i = pl.program_id(0)
    ni = pl.num_programs(0)
    slot = i % 2
    tm = COMB_TM

    def gather(blk, slot_):
        def body(s, carry):
            for k in range(TOP_K):
                row = slot_ref[(blk * tm + s) * TOP_K + k]
                pltpu.make_async_copy(y_hbm.at[pl.ds(row, 1), :], ybuf.at[slot_, k, pl.ds(s, 1), :],
                                      sem.at[slot_]).start()
            return carry
        lax.fori_loop(0, tm, body, 0, unroll=8)

    @pl.when(i == 0)
    def _():
        gather(0, 0)

    @pl.when(i + 1 < ni)
    def _():
        gather(i + 1, 1 - slot)

    for k in range(TOP_K):
        pltpu.make_async_copy(y_hbm.at[pl.ds(0, tm), :], ybuf.at[slot, k], sem.at[slot]).wait()

    half = x_ref.shape[1] // 2
    g0, g1 = r_ref[:, 0:1], r_ref[:, 1:2]
    y0_lo, y0_hi = _unpack_pair(ybuf[slot, 0])
    y1_lo, y1_hi = _unpack_pair(ybuf[slot, 1])
    x_lo = x_ref[:, :half] + (g0 * y0_lo + g1 * y1_lo)
    x_hi = x_ref[:, half:] + (g0 * y0_hi + g1 * y1_hi)
    if emit_x:
        xo_ref[:, :half] = x_lo
        xo_ref[:, half:] = x_hi
    ssq = jnp.sum(x_lo * x_lo, axis=-1, keepdims=True) + jnp.sum(x_hi * x_hi, axis=-1, keepdims=True)
    inv = lax.rsqrt(ssq / x_ref.shape[1] + RMS_EPS)
    ho_ref[:, :half] = (x_lo * inv * gain_ref[:, :half]).astype(ho_ref.dtype)
    ho_ref[:, half:] = (x_hi * inv * gain_ref[:, half:]).astype(ho_ref.dtype)


def combine(x2, route, y_rows, slot_of, gain, emit_x, h_dtype):
    n, d = x2.shape
    tm = COMB_TM
    row = lambda i, slots: (i, 0)
    out_shape = [jax.ShapeDtypeStruct((n, d), h_dtype)]
    out_specs = [pl.BlockSpec((tm, d), row)]
    if emit_x:
        out_shape.insert(0, jax.ShapeDtypeStruct((n, d), F32))
        out_specs.insert(0, pl.BlockSpec((tm, d), row))
    grid_spec = pltpu.PrefetchScalarGridSpec(
        num_scalar_prefetch=1,
        grid=(n // tm,),
        in_specs=[pl.BlockSpec((tm, d), row),
                  pl.BlockSpec((tm, ROUTER_LANES), row),
                  pl.BlockSpec((1, d), lambda i, slots: (0, 0)),
                  pl.BlockSpec(memory_space=pl.ANY)],
        out_specs=out_specs,
        scratch_shapes=[pltpu.VMEM((2, TOP_K, tm, d // 2), y_rows.dtype), pltpu.SemaphoreType.DMA((2,))],
    )
    return pl.pallas_call(
        functools.partial(_combine_kernel, emit_x=emit_x),
        grid_spec=grid_spec,
        out_shape=out_shape,
        compiler_params=_cparams(("arbitrary",)),
        name="moe_combine",
    )(slot_of, x2, route, gain.reshape(1, d), y_rows)


def moe_layer(x2, layer, ffn_gain, next_gain, emit_x, h_dtype, w_group, b_group, w_router, b_router,
              w_gate, w_up, w_down):
    n = x2.shape[0]
    route, hp, counts = router(x2, ffn_gain, w_group, b_group, w_router, b_router)
    slot_of, block_e, n_on, pad_from, pends = dispatch(route, counts, n)
    xs = scatter_rows(hp, slot_of, pad_from, pends, block_e.shape[0] * MOE_TM)
    y_rows = expert_blocks(xs, block_e, n_on, layer, w_gate, w_up, w_down)
    return combine(x2, route, y_rows, slot_of, next_gain, emit_x, h_dtype)


def kernel(x, rms_mix, rms_ffn, rms_final, w_in_ab, conv_dw_w, conv_dw_b, conv_ln_g, conv_ln_b, nat_rpb,
           w_out_ab, w_in_c, w_out_c, t5_bias, moe_w_group, moe_b_group, moe_w_router, moe_b_router,
           moe_w_gate, moe_w_up, moe_w_down):
    b, t, d = x.shape
    n = b * t
    x2 = x.reshape(n, d)
    conv_ch = conv_dw_w.shape[-1]
    nat_heads = nat_rpb.shape[1]
    dil_heads = w_out_c.shape[1] // HEAD_DIM

    h = rmsnorm(x2, rms_mix[0], BF16)
    proj = matmul([h], w_in_ab[0], BF16, name="in_proj_ab").reshape(b, t, -1)
    a_out = conformer_conv(proj, conv_dw_w[0], conv_dw_b[0], conv_ln_g[0], conv_ln_b[0])
    b_out = neighbourhood_attention(proj, nat_rpb[0], 2 * conv_ch // HEAD_DIM)
    x2 = matmul([a_out.reshape(n, -1), b_out.reshape(n, -1)], w_out_ab[0], F32, residual=x2,
                name="out_proj_ab")
    wg, wu, wd = moe_w_gate.astype(BF16), moe_w_up.astype(BF16), moe_w_down.astype(BF16)
    x2, h = moe_layer(x2, 0, rms_ffn[0], rms_mix[1], True, BF16, moe_w_group[0], moe_b_group[0],
                      moe_w_router[0], moe_b_router[0], wg, wu, wd)

    proj = matmul([h], w_in_c[0], BF16, name="in_proj_c").reshape(b, t, -1)
    o = dilated_attention(proj, t5_bias, dil_heads)
    x2 = matmul([o.reshape(n, -1)], w_out_c[0], F32, residual=x2, name="out_proj_c")
    (y,) = moe_layer(x2, 1, rms_ffn[1], rms_final, False, x.dtype, moe_w_group[1], moe_b_group[1],
                     moe_w_router[1], moe_b_router[1], wg, wu, wd)
    return y.reshape(b, t, d)
```

```python
import functools
import math

import jax
import jax.numpy as jnp
from jax import lax
from jax.experimental import pallas as pl
from jax.experimental.pallas import tpu as pltpu

F32 = jnp.float32
BF16 = jnp.bfloat16

HEAD_DIM = 128
CONV_WIDTH = 31
CONV_PAD = CONV_WIDTH // 2
NAT_WIN_ROWS = 8
NAT_WIN_COLS = 16
GRID_W = 64
DIL_CONFIGS = ((128, 1), (512, 4), (2048, 16))
T5_BUCKETS = 32
T5_MAX_DIST = 1024
N_GROUPS = 4
EXPERTS_PER_GROUP = 8
N_EXPERTS = N_GROUPS * EXPERTS_PER_GROUP
TOP_K = 2
RMS_EPS = 1e-6
LN_EPS = 1e-5
NEG = -1e30

VMEM_LIMIT = 56 * 1024 * 1024
SUBLANES = 8
BF16_SUBLANES = 16

MM_TM = 1024
MM_TN = 512
NORM_TM = 256
CONV_TT = 128
CONV_RB = 32
CONV_CB = 512
NAT_QROWS = 8
NAT_KROWS = 16
DIL_TQ = 256
MOE_TM = 256
SCAT_TM = 512
COMB_TM = 128
ROUTER_TM = 256


def _cparams(sem):
    return pltpu.CompilerParams(dimension_semantics=sem, vmem_limit_bytes=VMEM_LIMIT)


def _rmsnorm_kernel(x_ref, g_ref, o_ref):
    x = x_ref[...]
    ms = jnp.mean(x * x, axis=-1, keepdims=True)
    o_ref[...] = (x * lax.rsqrt(ms + RMS_EPS) * g_ref[...]).astype(o_ref.dtype)


def rmsnorm(x2, g, out_dtype):
    n, d = x2.shape
    return pl.pallas_call(
        _rmsnorm_kernel,
        grid=(n // NORM_TM,),
        in_specs=[pl.BlockSpec((NORM_TM, d), lambda i: (i, 0)),
                  pl.BlockSpec((1, d), lambda i: (0, 0))],
        out_specs=pl.BlockSpec((NORM_TM, d), lambda i: (i, 0)),
        out_shape=jax.ShapeDtypeStruct((n, d), out_dtype),
        compiler_params=_cparams(("arbitrary",)),
        name="rmsnorm",
    )(x2, g.reshape(1, d))


def _mm_kernel(*refs, n_x, has_res):
    x_refs = refs[:n_x]
    w_refs = refs[n_x:2 * n_x]
    pos = 2 * n_x
    r_ref = refs[pos] if has_res else None
    pos += int(has_res)
    o_ref, wbf_ref = refs[pos], refs[pos + 1]

    @pl.when(pl.program_id(1) == 0)
    def _():
        off = 0
        for w_ref in w_refs:
            kk = w_ref.shape[0]
            wbf_ref[off:off + kk, :] = w_ref[...].astype(BF16)
            off += kk

    acc = None
    off = 0
    for x_ref in x_refs:
        kk = x_ref.shape[1]
        part = jnp.dot(x_ref[...], wbf_ref[off:off + kk, :], preferred_element_type=F32)
        acc = part if acc is None else acc + part
        off += kk
    if has_res:
        acc = acc + r_ref[...]
    o_ref[...] = acc.astype(o_ref.dtype)


def matmul(xs, w, out_dtype, residual=None, name="matmul"):
    m = xs[0].shape[0]
    ks = [x.shape[1] for x in xs]
    ktot, nc = w.shape
    assert sum(ks) == ktot and all(k == ks[0] for k in ks)
    tm, tn = min(MM_TM, m), min(MM_TN, nc)
    assert m % tm == 0 and nc % tn == 0
    in_specs = [pl.BlockSpec((tm, k), lambda j, i: (i, 0)) for k in ks]
    in_specs += [pl.BlockSpec((k, tn), functools.partial(lambda j, i, p: (p, j), p=p))
                 for p, k in enumerate(ks)]
    args = list(xs) + [w] * len(xs)
    if residual is not None:
        in_specs.append(pl.BlockSpec((tm, tn), lambda j, i: (i, j)))
        args.append(residual)
    return pl.pallas_call(
        functools.partial(_mm_kernel, n_x=len(xs), has_res=residual is not None),
        grid=(nc // tn, m // tm),
        in_specs=in_specs,
        out_specs=pl.BlockSpec((tm, tn), lambda j, i: (i, j)),
        out_shape=jax.ShapeDtypeStruct((m, nc), out_dtype),
        scratch_shapes=[pltpu.VMEM((ktot, tn), BF16)],
        compiler_params=_cparams(("arbitrary", "arbitrary")),
        name=name,
    )(*args)


def _conv_kernel(ac_ref, gc_ref, ap_ref, gp_ref, an_ref, gn_ref, w_ref, b_ref, lg_ref, lb_ref,
                 o_ref, buf_ref, cv_ref, *, tt, ch):
    i = pl.program_id(1)
    last = pl.num_programs(1) - 1
    halo = BF16_SUBLANES

    def glu(a_ref, g_ref):
        return a_ref[0].astype(F32) * jax.nn.sigmoid(g_ref[0].astype(F32))

    buf_ref[0, 0:halo, :] = jnp.where(i > 0, glu(ap_ref, gp_ref), 0.0)
    buf_ref[0, halo:halo + tt, :] = glu(ac_ref, gc_ref)
    buf_ref[0, halo + tt:2 * halo + tt, :] = jnp.where(i < last, glu(an_ref, gn_ref), 0.0)
    span = tt + 2 * halo - SUBLANES
    for s in range(1, SUBLANES):
        buf_ref[s, 0:span, :] = buf_ref[0, s:s + span, :]

    base = halo - CONV_PAD
    for cb in range(ch // CONV_CB):
        cs = slice(cb * CONV_CB, (cb + 1) * CONV_CB)
        for rb in range(tt // CONV_RB):
            accs = [jnp.zeros((SUBLANES, CONV_CB), F32) for _ in range(CONV_RB // SUBLANES)]
            for j in range(CONV_WIDTH):
                shift = (base + j) % SUBLANES
                r0 = rb * CONV_RB + base + j - shift
                wj = w_ref[j, :, cs]
                for k in range(len(accs)):
                    rk = r0 + k * SUBLANES
                    accs[k] = accs[k] + wj * buf_ref[shift, rk:rk + SUBLANES, cs]
            for k, acc in enumerate(accs):
                rk = rb * CONV_RB + k * SUBLANES
                cv_ref[rk:rk + SUBLANES, cs] = acc + b_ref[:, cs]

    a = cv_ref[...]
    mu = jnp.mean(a, axis=-1, keepdims=True)
    ctr = a - mu
    var = jnp.mean(ctr * ctr, axis=-1, keepdims=True)
    y = ctr * lax.rsqrt(var + LN_EPS) * lg_ref[...] + lb_ref[...]
    o_ref[0] = (y * jax.nn.sigmoid(y)).astype(o_ref.dtype)


def conformer_conv(proj3, w_dw, b_dw, ln_g, ln_b):
    b, t, _ = proj3.shape
    ch = w_dw.shape[1]
    tt, halo = CONV_TT, BF16_SUBLANES
    hb = tt // halo
    n_hb = t // halo
    cur = lambda col: pl.BlockSpec((1, tt, ch), lambda bi, i: (bi, i, col))
    prv = lambda col: pl.BlockSpec((1, halo, ch), lambda bi, i: (bi, jnp.maximum(i * hb - 1, 0), col))
    nxt = lambda col: pl.BlockSpec((1, halo, ch), lambda bi, i: (bi, jnp.minimum((i + 1) * hb, n_hb - 1), col))
    vec = lambda rows: pl.BlockSpec((rows, ch), lambda bi, i: (0, 0))
    return pl.pallas_call(
        functools.partial(_conv_kernel, tt=tt, ch=ch),
        grid=(b, t // tt),
        in_specs=[cur(0), cur(1), prv(0), prv(1), nxt(0), nxt(1),
                  pl.BlockSpec((CONV_WIDTH, SUBLANES, ch), lambda bi, i: (0, 0, 0)), vec(1), vec(1), vec(1)],
        out_specs=pl.BlockSpec((1, tt, ch), lambda bi, i: (bi, i, 0)),
        out_shape=jax.ShapeDtypeStruct((b, t, ch), BF16),
        scratch_shapes=[pltpu.VMEM((SUBLANES, tt + 2 * halo, ch), F32), pltpu.VMEM((tt, ch), F32)],
        compiler_params=_cparams(("arbitrary", "arbitrary")),
        name="conformer_conv",
    )(proj3, proj3, proj3, proj3, proj3, proj3,
      jnp.broadcast_to(w_dw[:, None, :], (CONV_WIDTH, SUBLANES, ch)), b_dw.reshape(1, ch), ln_g.reshape(1, ch),
      ln_b.reshape(1, ch))


def _nat_bias_table(rpb):
    h = rpb.shape[0]
    kc = NAT_WIN_COLS
    cols = jnp.arange(GRID_W)
    col_start = jnp.clip(cols - kc // 2, 0, GRID_W - kc)
    col_mask = (cols[None, :] >= col_start[:, None]) & (cols[None, :] < col_start[:, None] + kc)
    dc_idx = jnp.clip(cols[None, :] - cols[:, None], -(kc - 1), kc - 1) + (kc - 1)
    bc = jnp.where(col_mask[None, None], rpb[:, :, dc_idx].astype(F32), NEG)
    blank = jnp.full((h, 1, GRID_W, GRID_W), NEG, F32)
    bc = jnp.concatenate([blank, bc, blank], axis=1)
    return jnp.concatenate([bc[:, :-1], bc[:, 1:]], axis=-1)


def _nat_kernel(q_ref, k_ref, v_ref, tab_ref, o_ref, bias_ref, *, rows):
    blk = pl.program_id(2)
    r_base = blk * NAT_QROWS
    ws = jnp.clip(r_base - NAT_WIN_ROWS // 2, 0, rows - NAT_KROWS)
    lane = lax.broadcasted_iota(jnp.int32, (GRID_W, 2 * GRID_W), 1)
    for qr in range(NAT_QROWS):
        r = r_base + qr
        r0 = jnp.clip(r - NAT_WIN_ROWS // 2, 0, rows - NAT_WIN_ROWS)
        for p in range(NAT_KROWS // 2):
            kr = ws + 2 * p
            entry = jnp.clip(kr - r + NAT_WIN_ROWS, 0, 2 * NAT_WIN_ROWS - 1)
            ok0 = ((kr >= r0) & (kr < r0 + NAT_WIN_ROWS)).astype(jnp.int32)
            ok1 = ((kr + 1 >= r0) & (kr + 1 < r0 + NAT_WIN_ROWS)).astype(jnp.int32)
            ok = jnp.where(lane < GRID_W, ok0, ok1) > 0
            bias_ref[qr * GRID_W:(qr + 1) * GRID_W, p * 2 * GRID_W:(p + 1) * 2 * GRID_W] = (
                jnp.where(ok, tab_ref[0, entry], NEG))

    start = pl.multiple_of(ws * GRID_W, 256)
    k = k_ref[0, pl.ds(start, NAT_KROWS * GRID_W), :]
    v = v_ref[0, pl.ds(start, NAT_KROWS * GRID_W), :]
    s = lax.dot_general(q_ref[0], k, (((1,), (1,)), ((), ())), preferred_element_type=F32)
    s = s * (HEAD_DIM ** -0.5) + bias_ref[...]
    m = jnp.max(s, axis=-1, keepdims=True)
    p = jnp.exp(s - m)
    den = jnp.sum(p, axis=-1, keepdims=True)
    o = jnp.dot(p.astype(BF16), v, preferred_element_type=F32)
    o_ref[0] = (o / den).astype(o_ref.dtype)


def neighbourhood_attention(proj3, rpb, col0):
    b, t, _ = proj3.shape
    nh = rpb.shape[0]
    rows = t // GRID_W
    assert rows >= NAT_KROWS and rows % NAT_QROWS == 0
    tq = NAT_QROWS * GRID_W
    tab = _nat_bias_table(rpb)
    return pl.pallas_call(
        functools.partial(_nat_kernel, rows=rows),
        grid=(nh, b, rows // NAT_QROWS),
        in_specs=[pl.BlockSpec((1, tq, HEAD_DIM), lambda h, bi, i: (bi, i, col0 + h)),
                  pl.BlockSpec((1, t, HEAD_DIM), lambda h, bi, i: (bi, 0, col0 + nh + h)),
                  pl.BlockSpec((1, t, HEAD_DIM), lambda h, bi, i: (bi, 0, col0 + 2 * nh + h)),
                  pl.BlockSpec((1, 2 * NAT_WIN_ROWS, GRID_W, 2 * GRID_W), lambda h, bi, i: (h, 0, 0, 0))],
        out_specs=pl.BlockSpec((1, tq, HEAD_DIM), lambda h, bi, i: (bi, i, h)),
        out_shape=jax.ShapeDtypeStruct((b, t, nh * HEAD_DIM), BF16),
        scratch_shapes=[pltpu.VMEM((tq, NAT_KROWS * GRID_W), F32)],
        compiler_params=_cparams(("arbitrary", "arbitrary", "arbitrary")),
        name="neighbourhood_attention",
    )(proj3, proj3, proj3, tab)


def _t5_bucket(rel):
    nb = T5_BUCKETS // 2
    max_exact = nb // 2
    n = jnp.abs(rel)
    sign = jnp.where(rel > 0, nb, 0)
    nf = jnp.maximum(n, 1).astype(F32)
    large = max_exact + (jnp.log(nf / max_exact) / math.log(T5_MAX_DIST / max_exact)
                         * (nb - max_exact)).astype(jnp.int32)
    large = jnp.minimum(large, nb - 1)
    return sign + jnp.where(n < max_exact, n, large)


def _dil_halo(g):
    win, dil = DIL_CONFIGS[g]
    return win // 2


def _dil_bias_rows(t5_bias, g, nh):
    win, dil = DIL_CONFIGS[g]
    halo = _dil_halo(g)
    delta = jnp.arange(DIL_TQ + 2 * halo) - halo
    on = (delta % dil == 0) & (jnp.abs(delta) <= halo)
    vals = t5_bias[:, g * nh:(g + 1) * nh][_t5_bucket(delta)].astype(F32)
    return jnp.where(on[None], vals.T, NEG)[:, None, :]


def _dil_kernel(q0_ref, q1_ref, q2_ref, u0_ref, u1_ref, u2_ref, proj_ref, o_ref,
                k0, v0, k1, v1, k2, v2, b0_ref, b1_ref, b2_ref, sem, *, t, nh):
    h, bi, i = pl.program_id(0), pl.program_id(1), pl.program_id(2)
    kv = ((k0, v0), (k1, v1), (k2, v2))

    @pl.when((bi == 0) & (i == 0))
    def _():
        for u_ref, b_ref in ((u0_ref, b0_ref), (u1_ref, b1_ref), (u2_ref, b2_ref)):
            rows = jnp.broadcast_to(u_ref[0], b_ref.shape)
            b_ref[...] = pltpu.roll(rows, 0, 1, stride=1, stride_axis=0)

    def copies():
        out = []
        for g in range(3):
            halo = _dil_halo(g)
            for which in range(2):
                col = pl.multiple_of(((g * 3 + 1 + which) * nh + h) * HEAD_DIM, HEAD_DIM)
                out.append(pltpu.make_async_copy(
                    proj_ref.at[bi, :, pl.ds(col, HEAD_DIM)],
                    kv[g][which].at[pl.ds(halo, t), :],
                    sem.at[g * 2 + which]))
        return out

    @pl.when(i == 0)
    def _():
        cps = copies()
        for cp in cps:
            cp.start()
        for g in range(3):
            halo = _dil_halo(g)
            for ref in kv[g]:
                ref[0:halo, :] = jnp.zeros((halo, HEAD_DIM), BF16)
                ref[halo + t:2 * halo + t, :] = jnp.zeros((halo, HEAD_DIM), BF16)
        for cp in cps:
            cp.wait()

    t0 = pl.multiple_of(i * DIL_TQ, DIL_TQ)
    scores = []
    for g, (q_ref, b_ref) in enumerate(((q0_ref, b0_ref), (q1_ref, b1_ref), (q2_ref, b2_ref))):
        halo = _dil_halo(g)
        w = DIL_TQ + 2 * halo
        k = kv[g][0][pl.ds(t0, w), :]
        s = lax.dot_general(q_ref[0], k, (((1,), (1,)), ((), ())), preferred_element_type=F32)
        kpos = lax.broadcasted_iota(jnp.int32, (1, w), 1) + (t0 - halo)
        kmask = jnp.where((kpos >= 0) & (kpos < t), 0.0, NEG)
        scores.append(s * (HEAD_DIM ** -0.5) + b_ref[...] + kmask)

    m = functools.reduce(jnp.maximum, [jnp.max(s, axis=-1, keepdims=True) for s in scores])
    num = jnp.zeros((DIL_TQ, HEAD_DIM), F32)
    den = jnp.zeros((DIL_TQ, 1), F32)
    for g, s in enumerate(scores):
        halo = _dil_halo(g)
        p = jnp.exp(s - m)
        den = den + jnp.sum(p, axis=-1, keepdims=True)
        v = kv[g][1][pl.ds(t0, DIL_TQ + 2 * halo), :]
        num = num + jnp.dot(p.astype(BF16), v, preferred_element_type=F32)
    o_ref[0] = (num / den).astype(o_ref.dtype)


def dilated_attention(proj3, t5_bias, nh):
    b, t, _ = proj3.shape
    assert t % DIL_TQ == 0
    tabs = [_dil_bias_rows(t5_bias, g, nh) for g in range(3)]
    qspec = lambda g: pl.BlockSpec((1, DIL_TQ, HEAD_DIM), lambda h, bi, i: (bi, i, g * 3 * nh + h))
    bspec = lambda g: pl.BlockSpec((1, 1, DIL_TQ + 2 * _dil_halo(g)), lambda h, bi, i: (h, 0, 0))
    scratch = []
    for g in range(3):
        scratch += [pltpu.VMEM((t + 2 * _dil_halo(g), HEAD_DIM), BF16)] * 2
    scratch += [pltpu.VMEM((DIL_TQ, DIL_TQ + 2 * _dil_halo(g)), F32) for g in range(3)]
    scratch.append(pltpu.SemaphoreType.DMA((6,)))
    return pl.pallas_call(
        functools.partial(_dil_kernel, t=t, nh=nh),
        grid=(nh, b, t // DIL_TQ),
        in_specs=[qspec(0), qspec(1), qspec(2), bspec(0), bspec(1), bspec(2),
                  pl.BlockSpec(memory_space=pl.ANY)],
        out_specs=pl.BlockSpec((1, DIL_TQ, HEAD_DIM), lambda h, bi, i: (bi, i, h)),
        out_shape=jax.ShapeDtypeStruct((b, t, nh * HEAD_DIM), BF16),
        scratch_shapes=scratch,
        compiler_params=_cparams(("arbitrary", "arbitrary", "arbitrary")),
        name="dilated_attention",
    )(proj3, proj3, proj3, tabs[0], tabs[1], tabs[2], proj3)


ROUTER_LANES = 128


def _pack_pair(lo, hi):
    lo_bits = lax.bitcast_convert_type(lo.astype(BF16).astype(F32), jnp.uint32)
    hi_bits = lax.bitcast_convert_type(hi.astype(BF16).astype(F32), jnp.uint32)
    return lax.shift_right_logical(lo_bits, jnp.uint32(16)) | (hi_bits & jnp.uint32(0xFFFF0000))


def _unpack_pair(word):
    lo = lax.bitcast_convert_type(lax.shift_left(word, jnp.uint32(16)), F32)
    hi = lax.bitcast_convert_type(word & jnp.uint32(0xFFFF0000), F32)
    return lo, hi


def _router_kernel(x_ref, g_ref, whi_ref, wlo_ref, b_ref, o_ref, hp_ref, cnt_ref, carry_ref):
    step = pl.program_id(0)
    half = x_ref.shape[1] // 2
    x = x_ref[...]
    ms = jnp.mean(x * x, axis=-1, keepdims=True)
    h = x * lax.rsqrt(ms + RMS_EPS) * g_ref[...]
    hp_ref[...] = _pack_pair(h[:, :half], h[:, half:])
    h_hi = h.astype(BF16)
    h_lo = (h - h_hi.astype(F32)).astype(BF16)
    logits = (jnp.dot(h_hi, whi_ref[...], preferred_element_type=F32)
              + jnp.dot(h_lo, whi_ref[...], preferred_element_type=F32)
              + jnp.dot(h_hi, wlo_ref[...], preferred_element_type=F32)) + b_ref[...]

    lane = lax.broadcasted_iota(jnp.int32, logits.shape, 1)
    big = jnp.int32(ROUTER_LANES)

    def first_argmax(vals):
        top = jnp.max(vals, axis=-1, keepdims=True)
        return top, jnp.min(jnp.where(vals == top, lane, big), axis=-1, keepdims=True)

    gl = jnp.where(lane < N_GROUPS, logits, NEG)
    g_top, g_idx = first_argmax(gl)
    g_val = 1.0 / jnp.sum(jnp.exp(gl - g_top), axis=-1, keepdims=True)
    in_group = (lane >= N_GROUPS) & (lane < N_GROUPS + N_EXPERTS) & (
        lax.shift_right_arithmetic(lane - N_GROUPS, jnp.int32(3)) == g_idx)
    el = jnp.where(in_group, logits, NEG)
    v1, i1 = first_argmax(el)
    el2 = jnp.where(lane == i1, NEG, el)
    v2, i2 = first_argmax(el2)
    e21 = jnp.exp(v2 - v1)
    gate1 = g_val / (1.0 + e21)
    gate2 = g_val * e21 / (1.0 + e21)

    @pl.when(step == 0)
    def _():
        carry_ref[...] = jnp.zeros(carry_ref.shape, F32)

    e1, e2 = i1 - N_GROUPS, i2 - N_GROUPS
    oh1 = (lane == e1).astype(F32)
    oh2 = (lane == e2).astype(F32)
    both = oh1 + oh2
    tm = x.shape[0]
    tri = (lax.broadcasted_iota(jnp.int32, (tm, tm), 1) < lax.broadcasted_iota(jnp.int32, (tm, tm), 0))
    before = jnp.dot(tri.astype(BF16), both.astype(BF16), preferred_element_type=F32) + carry_ref[...]
    rank1 = jnp.sum(oh1 * before, axis=-1, keepdims=True)
    rank2 = jnp.sum(oh2 * before, axis=-1, keepdims=True)
    carry_ref[...] = carry_ref[...] + jnp.sum(both, axis=0, keepdims=True)
    cnt_ref[...] = jnp.broadcast_to(carry_ref[...], cnt_ref.shape)

    out = jnp.where(lane == 0, gate1, 0.0)
    out = jnp.where(lane == 1, gate2, out)
    out = jnp.where(lane == 2, e1.astype(F32), out)
    out = jnp.where(lane == 3, e2.astype(F32), out)
    out = jnp.where(lane == 4, rank1, out)
    out = jnp.where(lane == 5, rank2, out)
    o_ref[...] = out


def router(x2, gain, w_group, b_group, w_router, b_router):
    n, d = x2.shape
    wcat = jnp.concatenate([w_group, w_router], axis=1).astype(F32)
    wcat = jnp.pad(wcat, ((0, 0), (0, ROUTER_LANES - wcat.shape[1])))
    w_hi = wcat.astype(BF16)
    w_lo = (wcat - w_hi.astype(F32)).astype(BF16)
    bias = jnp.concatenate([b_group.reshape(-1), b_router.reshape(-1)]).astype(F32)
    bias = jnp.pad(bias, (0, ROUTER_LANES - bias.shape[0])).reshape(1, ROUTER_LANES)
    tm = ROUTER_TM
    return pl.pallas_call(
        _router_kernel,
        grid=(n // tm,),
        in_specs=[pl.BlockSpec((tm, d), lambda i: (i, 0)),
                  pl.BlockSpec((1, d), lambda i: (0, 0)),
                  pl.BlockSpec((d, ROUTER_LANES), lambda i: (0, 0)),
                  pl.BlockSpec((d, ROUTER_LANES), lambda i: (0, 0)),
                  pl.BlockSpec((1, ROUTER_LANES), lambda i: (0, 0))],
        out_specs=[pl.BlockSpec((tm, ROUTER_LANES), lambda i: (i, 0)),
                   pl.BlockSpec((tm, d // 2), lambda i: (i, 0)),
                   pl.BlockSpec((8, ROUTER_LANES), lambda i: (0, 0))],
        out_shape=[jax.ShapeDtypeStruct((n, ROUTER_LANES), F32),
                   jax.ShapeDtypeStruct((n, d // 2), jnp.uint32),
                   jax.ShapeDtypeStruct((8, ROUTER_LANES), F32)],
        scratch_shapes=[pltpu.VMEM((1, ROUTER_LANES), F32)],
        compiler_params=_cparams(("arbitrary",)),
        name="moe_router",
    )(x2, gain.reshape(1, d), w_hi, w_lo, bias)


def dispatch(route, counts, n):
    n_blocks = n * TOP_K // MOE_TM + N_EXPERTS
    counts = counts[0, :N_EXPERTS].astype(jnp.int32)
    pcounts = (counts + MOE_TM - 1) // MOE_TM * MOE_TM
    pends = jnp.cumsum(pcounts)
    pstarts = pends - pcounts
    expert = route[:, 2:2 + TOP_K].astype(jnp.int32)
    rank = route[:, 2 + TOP_K:2 + 2 * TOP_K].astype(jnp.int32)
    ids = jnp.arange(N_EXPERTS, dtype=jnp.int32)
    base = jnp.sum(jnp.where(expert[..., None] == ids, pstarts, 0), axis=-1)
    slot_of = (base + rank).reshape(-1)
    blk_start = jnp.arange(n_blocks, dtype=jnp.int32) * MOE_TM
    block_e = jnp.minimum(jnp.sum(blk_start[:, None] >= pends[None, :], axis=-1), N_EXPERTS - 1)
    n_on = (pends[-1] // MOE_TM).reshape(1)
    pad_from = pstarts + counts
    return slot_of, block_e.astype(jnp.int32), n_on.astype(jnp.int32), pad_from, pends


def _scatter_kernel(slot_ref, padfrom_ref, pend_ref, hp_ref, xs_hbm, zero_ref, sem, zsem):
    i = pl.program_id(0)
    tm = SCAT_TM
    n_blocks = xs_hbm.shape[0] // MOE_TM

    def pad_rows(fn):
        for e in range(N_EXPERTS):
            def body(r, carry):
                fn(pltpu.make_async_copy(zero_ref.at[pl.ds(0, 1), :], xs_hbm.at[pl.ds(r, 1), :], zsem))
                return carry
            lax.fori_loop(padfrom_ref[e], pend_ref[e], body, 0)

        def blk(b, carry):
            start = pl.multiple_of(b * MOE_TM, MOE_TM)
            fn(pltpu.make_async_copy(zero_ref, xs_hbm.at[pl.ds(start, MOE_TM), :], zsem))
            return carry
        lax.fori_loop(pend_ref[N_EXPERTS - 1] // MOE_TM, n_blocks, blk, 0)

    @pl.when(i == 0)
    def _():
        zero_ref[...] = jnp.zeros(zero_ref.shape, zero_ref.dtype)
        pad_rows(lambda cp: cp.start())

    def body(s, carry):
        for k in range(TOP_K):
            row = slot_ref[(i * tm + s) * TOP_K + k]
            pltpu.make_async_copy(hp_ref.at[pl.ds(s, 1), :], xs_hbm.at[pl.ds(row, 1), :], sem).start()
        return carry
    lax.fori_loop(0, tm, body, 0, unroll=8)
    for k in range(TOP_K):
        pltpu.make_async_copy(hp_ref, xs_hbm.at[pl.ds(0, tm), :], sem).wait()

    @pl.when(i == pl.num_programs(0) - 1)
    def _():
        pad_rows(lambda cp: cp.wait())


def scatter_rows(hp, slot_of, pad_from, pends, n_rows):
    n, w = hp.shape
    grid_spec = pltpu.PrefetchScalarGridSpec(
        num_scalar_prefetch=3,
        grid=(n // SCAT_TM,),
        in_specs=[pl.BlockSpec((SCAT_TM, w), lambda i, slots, pf, pe: (i, 0))],
        out_specs=pl.BlockSpec(memory_space=pl.ANY),
        scratch_shapes=[pltpu.VMEM((MOE_TM, w), hp.dtype),
                        pltpu.SemaphoreType.DMA(()), pltpu.SemaphoreType.DMA(())],
    )
    return pl.pallas_call(
        _scatter_kernel,
        grid_spec=grid_spec,
        out_shape=jax.ShapeDtypeStruct((n_rows, w), hp.dtype),
        compiler_params=_cparams(("arbitrary",)),
        name="moe_scatter",
    )(slot_of, pad_from, pends, hp)


def _expert_kernel(be_ref, non_ref, xs_ref, wg_ref, wu_ref, wd_ref, o_ref):
    on = pl.program_id(0) < non_ref[0]

    @pl.when(jnp.logical_not(on))
    def _():
        o_ref[...] = jnp.zeros(o_ref.shape, o_ref.dtype)

    @pl.when(on)
    def _():
        lo, hi = _unpack_pair(xs_ref[...])
        lo, hi = lo.astype(BF16), hi.astype(BF16)
        half = lo.shape[1]

        def proj(w_ref):
            return (jnp.dot(lo, w_ref[0, 0, :half], preferred_element_type=F32)
                    + jnp.dot(hi, w_ref[0, 0, half:], preferred_element_type=F32))

        g = proj(wg_ref)
        u = proj(wu_ref)
        hdn = (g * jax.nn.sigmoid(g) * u).astype(BF16)
        y = jnp.dot(hdn, wd_ref[0, 0], preferred_element_type=F32)
        o_ref[...] = _pack_pair(y[:, :half], y[:, half:])


def expert_blocks(xs, block_e, n_on, layer, w_gate, w_up, w_down):
    n_rows, half = xs.shape
    _, _, d, hid = w_gate.shape
    tm = MOE_TM
    wspec = lambda shape: pl.BlockSpec((1, 1) + shape, lambda b, be, non: (layer, be[b], 0, 0))
    grid_spec = pltpu.PrefetchScalarGridSpec(
        num_scalar_prefetch=2,
        grid=(n_rows // tm,),
        in_specs=[pl.BlockSpec((tm, half), lambda b, be, non: (b, 0)),
                  wspec((d, hid)), wspec((d, hid)), wspec((hid, d))],
        out_specs=pl.BlockSpec((tm, half), lambda b, be, non: (b, 0)),
    )
    return pl.pallas_call(
        _expert_kernel,
        grid_spec=grid_spec,
        out_shape=jax.ShapeDtypeStruct((n_rows, half), xs.dtype),
        compiler_params=_cparams(("arbitrary",)),
        name="moe_experts",
    )(block_e, n_on, xs, w_gate, w_up, w_down)


def _combine_kernel(slot_ref, x_ref, r_ref, gain_ref, y_hbm, *rest, emit_x):
    if emit_x:
        xo_ref, ho_ref, ybuf, sem = rest
    else:
        ho_ref, ybuf, sem = rest
        xo_ref = None
    i = pl.program_id(0)
    ni = pl.num_programs(0)
    slot = i % 2
    tm = COMB_TM

    def gather(blk, slot_):
        def body(s, carry):
            for k in range(TOP_K):
                row = slot_ref[(blk * tm + s) * TOP_K + k]
                pltpu.make_async_copy(y_hbm.at[pl.ds(row, 1), :], ybuf.at[slot_, k, pl.ds(s, 1), :],
                                      sem.at[slot_]).start()
            return carry
        lax.fori_loop(0, tm, body, 0, unroll=8)

    @pl.when(i == 0)
    def _():
        gather(0, 0)

    @pl.when(i + 1 < ni)
    def _():
        gather(i + 1, 1 - slot)

    for k in range(TOP_K):
        pltpu.make_async_copy(y_hbm.at[pl.ds(0, tm), :], ybuf.at[slot, k], sem.at[slot]).wait()

    half = x_ref.shape[1] // 2
    g0, g1 = r_ref[:, 0:1], r_ref[:, 1:2]
    y0_lo, y0_hi = _unpack_pair(ybuf[slot, 0])
    y1_lo, y1_hi = _unpack_pair(ybuf[slot, 1])
    x_lo = x_ref[:, :half] + (g0 * y0_lo + g1 * y1_lo)
    x_hi = x_ref[:, half:] + (g0 * y0_hi + g1 * y1_hi)
    if emit_x:
        xo_ref[:, :half] = x_lo
        xo_ref[:, half:] = x_hi
    ssq = jnp.sum(x_lo * x_lo, axis=-1, keepdims=True) + jnp.sum(x_hi * x_hi, axis=-1, keepdims=True)
    inv = lax.rsqrt(ssq / x_ref.shape[1] + RMS_EPS)
    ho_ref[:, :half] = (x_lo * inv * gain_ref[:, :half]).astype(ho_ref.dtype)
    ho_ref[:, half:] = (x_hi * inv * gain_ref[:, half:]).astype(ho_ref.dtype)


def combine(x2, route, y_rows, slot_of, gain, emit_x, h_dtype):
    n, d = x2.shape
    tm = COMB_TM
    row = lambda i, slots: (i, 0)
    out_shape = [jax.ShapeDtypeStruct((n, d), h_dtype)]
    out_specs = [pl.BlockSpec((tm, d), row)]
    if emit_x:
        out_shape.insert(0, jax.ShapeDtypeStruct((n, d), F32))
        out_specs.insert(0, pl.BlockSpec((tm, d), row))
    grid_spec = pltpu.PrefetchScalarGridSpec(
        num_scalar_prefetch=1,
        grid=(n // tm,),
        in_specs=[pl.BlockSpec((tm, d), row),
                  pl.BlockSpec((tm, ROUTER_LANES), row),
                  pl.BlockSpec((1, d), lambda i, slots: (0, 0)),
                  pl.BlockSpec(memory_space=pl.ANY)],
        out_specs=out_specs,
        scratch_shapes=[pltpu.VMEM((2, TOP_K, tm, d // 2), y_rows.dtype), pltpu.SemaphoreType.DMA((2,))],
    )
    return pl.pallas_call(
        functools.partial(_combine_kernel, emit_x=emit_x),
        grid_spec=grid_spec,
        out_shape=out_shape,
        compiler_params=_cparams(("arbitrary",)),
        name="moe_combine",
    )(slot_of, x2, route, gain.reshape(1, d), y_rows)


def moe_layer(x2, layer, ffn_gain, next_gain, emit_x, h_dtype, w_group, b_group, w_router, b_router,
              w_gate, w_up, w_down):
    n = x2.shape[0]
    route, hp, counts = router(x2, ffn_gain, w_group, b_group, w_router, b_router)
    slot_of, block_e, n_on, pad_from, pends = dispatch(route, counts, n)
    xs = scatter_rows(hp, slot_of, pad_from, pends, block_e.shape[0] * MOE_TM)
    y_rows = expert_blocks(xs, block_e, n_on, layer, w_gate, w_up, w_down)
    return combine(x2, route, y_rows, slot_of, next_gain, emit_x, h_dtype)


def kernel(x, rms_mix, rms_ffn, rms_final, w_in_ab, conv_dw_w, conv_dw_b, conv_ln_g, conv_ln_b, nat_rpb,
           w_out_ab, w_in_c, w_out_c, t5_bias, moe_w_group, moe_b_group, moe_w_router, moe_b_router,
           moe_w_gate, moe_w_up, moe_w_down):
    b, t, d = x.shape
    n = b * t
    x2 = x.reshape(n, d)
    conv_ch = conv_dw_w.shape[-1]
    nat_heads = nat_rpb.shape[1]
    dil_heads = w_out_c.shape[1] // HEAD_DIM

    h = rmsnorm(x2, rms_mix[0], BF16)
    proj = matmul([h], w_in_ab[0], BF16, name="in_proj_ab").reshape(b, t, -1)
    a_out = conformer_conv(proj, conv_dw_w[0], conv_dw_b[0], conv_ln_g[0], conv_ln_b[0])
    b_out = neighbourhood_attention(proj, nat_rpb[0], 2 * conv_ch // HEAD_DIM)
    x2 = matmul([a_out.reshape(n, -1), b_out.reshape(n, -1)], w_out_ab[0], F32, residual=x2,
                name="out_proj_ab")
    wg, wu, wd = moe_w_gate.astype(BF16), moe_w_up.astype(BF16), moe_w_down.astype(BF16)
    x2, h = moe_layer(x2, 0, rms_ffn[0], rms_mix[1], True, BF16, moe_w_group[0], moe_b_group[0],
                      moe_w_router[0], moe_b_router[0], wg, wu, wd)

    proj = matmul([h], w_in_c[0], BF16, name="in_proj_c").reshape(b, t, -1)
    o = dilated_attention(proj, t5_bias, dil_heads)
    x2 = matmul([o.reshape(n, -1)], w_out_c[0], F32, residual=x2, name="out_proj_c")
    (y,) = moe_layer(x2, 1, rms_ffn[1], rms_final, False, x.dtype, moe_w_group[1], moe_b_group[1],
                     moe_w_router[1], moe_b_router[1], wg, wu, wd)
    return y.reshape(b, t, d)
```

```python
import functools
import math

import jax
import jax.numpy as jnp
from jax import lax
from jax.experimental import pallas as pl
from jax.experimental.pallas import tpu as pltpu

F32 = jnp.float32
BF16 = jnp.bfloat16

HEAD_DIM = 128
CONV_WIDTH = 31
CONV_PAD = CONV_WIDTH // 2
NAT_WIN_ROWS = 8
NAT_WIN_COLS = 16
GRID_W = 64
DIL_CONFIGS = ((128, 1), (512, 4), (2048, 16))
T5_BUCKETS = 32
T5_MAX_DIST = 1024
N_GROUPS = 4
EXPERTS_PER_GROUP = 8
N_EXPERTS = N_GROUPS * EXPERTS_PER_GROUP
TOP_K = 2
RMS_EPS = 1e-6
LN_EPS = 1e-5
NEG = -1e30

VMEM_LIMIT = 56 * 1024 * 1024
SUBLANES = 8
BF16_SUBLANES = 16

MM_TM = 1024
MM_TN = 512
NORM_TM = 256
CONV_TT = 128
CONV_RB = 32
CONV_CB = 512
NAT_QROWS = 8
NAT_KROWS = 16
DIL_TQ = 1024
DIL_MQ = 256
DIL_HALF = 64
PERM_BLOCK = 256
MOE_TM = 256
SCAT_TM = 512
COMB_TM = 128
ROUTER_TM = 256


def _cparams(sem):
    return pltpu.CompilerParams(dimension_semantics=sem, vmem_limit_bytes=VMEM_LIMIT)


def _rmsnorm_kernel(x_ref, g_ref, o_ref):
    x = x_ref[...]
    ms = jnp.mean(x * x, axis=-1, keepdims=True)
    o_ref[...] = (x * lax.rsqrt(ms + RMS_EPS) * g_ref[...]).astype(o_ref.dtype)


def rmsnorm(x2, g, out_dtype):
    n, d = x2.shape
    return pl.pallas_call(
        _rmsnorm_kernel,
        grid=(n // NORM_TM,),
        in_specs=[pl.BlockSpec((NORM_TM, d), lambda i: (i, 0)),
                  pl.BlockSpec((1, d), lambda i: (0, 0))],
        out_specs=pl.BlockSpec((NORM_TM, d), lambda i: (i, 0)),
        out_shape=jax.ShapeDtypeStruct((n, d), out_dtype),
        compiler_params=_cparams(("arbitrary",)),
        name="rmsnorm",
    )(x2, g.reshape(1, d))


def _row_perm_matrix(dil):
    n = PERM_BLOCK // dil
    out = jnp.arange(PERM_BLOCK)
    src = dil * (out % n) + out // n
    return (src[:, None] == jnp.arange(PERM_BLOCK)[None, :]).astype(BF16)


def _mm_kernel(*refs, n_x, has_res, perm_dils, tiles_per_group):
    x_refs = refs[:n_x]
    w_refs = refs[n_x:2 * n_x]
    pos = 2 * n_x
    r_ref = refs[pos] if has_res else None
    pos += int(has_res)
    n_perm = sum(d > 1 for d in perm_dils)
    p_refs = refs[pos:pos + n_perm]
    pos += n_perm
    o_ref, wbf_ref = refs[pos], refs[pos + 1]

    @pl.when(pl.program_id(1) == 0)
    def _():
        off = 0
        for w_ref in w_refs:
            kk = w_ref.shape[0]
            wbf_ref[off:off + kk, :] = w_ref[...].astype(BF16)
            off += kk

    acc = None
    off = 0
    for x_ref in x_refs:
        kk = x_ref.shape[1]
        part = jnp.dot(x_ref[...], wbf_ref[off:off + kk, :], preferred_element_type=F32)
        acc = part if acc is None else acc + part
        off += kk
    if has_res:
        acc = acc + r_ref[...]
    out = acc.astype(o_ref.dtype)
    if not p_refs:
        o_ref[...] = out
        return

    group = pl.program_id(0) // tiles_per_group
    p_iter = iter(p_refs)
    for g, dil in enumerate(perm_dils):
        p_ref = next(p_iter) if dil > 1 else None

        @pl.when(group == g)
        def _(p_ref=p_ref):
            if p_ref is None:
                o_ref[...] = out
            else:
                for blk in range(out.shape[0] // PERM_BLOCK):
                    rows = slice(blk * PERM_BLOCK, (blk + 1) * PERM_BLOCK)
                    o_ref[rows, :] = jnp.dot(p_ref[...], out[rows, :],
                                             preferred_element_type=F32).astype(o_ref.dtype)


def matmul(xs, w, out_dtype, residual=None, perm_dils=(), name="matmul"):
    m = xs[0].shape[0]
    ks = [x.shape[1] for x in xs]
    ktot, nc = w.shape
    assert sum(ks) == ktot and all(k == ks[0] for k in ks)
    tm, tn = min(MM_TM, m), min(MM_TN, nc)
    assert m % tm == 0 and nc % tn == 0
    in_specs = [pl.BlockSpec((tm, k), lambda j, i: (i, 0)) for k in ks]
    in_specs += [pl.BlockSpec((k, tn), functools.partial(lambda j, i, p: (p, j), p=p))
                 for p, k in enumerate(ks)]
    args = list(xs) + [w] * len(xs)
    if residual is not None:
        in_specs.append(pl.BlockSpec((tm, tn), lambda j, i: (i, j)))
        args.append(residual)
    tiles_per_group = 0
    if perm_dils:
        assert out_dtype == BF16 and tm % PERM_BLOCK == 0 and (nc // tn) % len(perm_dils) == 0
        tiles_per_group = (nc // tn) // len(perm_dils)
        for dil in perm_dils:
            if dil > 1:
                in_specs.append(pl.BlockSpec((PERM_BLOCK, PERM_BLOCK), lambda j, i: (0, 0)))
                args.append(_row_perm_matrix(dil))
    return pl.pallas_call(
        functools.partial(_mm_kernel, n_x=len(xs), has_res=residual is not None,
                          perm_dils=tuple(perm_dils), tiles_per_group=tiles_per_group),
        grid=(nc // tn, m // tm),
        in_specs=in_specs,
        out_specs=pl.BlockSpec((tm, tn), lambda j, i: (i, j)),
        out_shape=jax.ShapeDtypeStruct((m, nc), out_dtype),
        scratch_shapes=[pltpu.VMEM((ktot, tn), BF16)],
        compiler_params=_cparams(("arbitrary", "arbitrary")),
        name=name,
    )(*args)


def _conv_kernel(ac_ref, gc_ref, ap_ref, gp_ref, an_ref, gn_ref, w_ref, b_ref, lg_ref, lb_ref,
                 o_ref, buf_ref, cv_ref, *, tt, ch):
    i = pl.program_id(1)
    last = pl.num_programs(1) - 1
    halo = BF16_SUBLANES

    def glu(a_ref, g_ref):
        return a_ref[0].astype(F32) * jax.nn.sigmoid(g_ref[0].astype(F32))

    buf_ref[0, 0:halo, :] = jnp.where(i > 0, glu(ap_ref, gp_ref), 0.0)
    buf_ref[0, halo:halo + tt, :] = glu(ac_ref, gc_ref)
    buf_ref[0, halo + tt:2 * halo + tt, :] = jnp.where(i < last, glu(an_ref, gn_ref), 0.0)
    span = tt + 2 * halo - SUBLANES
    for s in range(1, SUBLANES):
        buf_ref[s, 0:span, :] = buf_ref[0, s:s + span, :]

    base = halo - CONV_PAD
    for cb in range(ch // CONV_CB):
        cs = slice(cb * CONV_CB, (cb + 1) * CONV_CB)
        for rb in range(tt // CONV_RB):
            accs = [jnp.zeros((SUBLANES, CONV_CB), F32) for _ in range(CONV_RB // SUBLANES)]
            for j in range(CONV_WIDTH):
                shift = (base + j) % SUBLANES
                r0 = rb * CONV_RB + base + j - shift
                wj = w_ref[j, :, cs]
                for k in range(len(accs)):
                    rk = r0 + k * SUBLANES
                    accs[k] = accs[k] + wj * buf_ref[shift, rk:rk + SUBLANES, cs]
            for k, acc in enumerate(accs):
                rk = rb * CONV_RB + k * SUBLANES
                cv_ref[rk:rk + SUBLANES, cs] = acc + b_ref[:, cs]

    a = cv_ref[...]
    mu = jnp.mean(a, axis=-1, keepdims=True)
    ctr = a - mu
    var = jnp.mean(ctr * ctr, axis=-1, keepdims=True)
    y = ctr * lax.rsqrt(var + LN_EPS) * lg_ref[...] + lb_ref[...]
    o_ref[0] = (y * jax.nn.sigmoid(y)).astype(o_ref.dtype)


def conformer_conv(proj3, w_dw, b_dw, ln_g, ln_b):
    b, t, _ = proj3.shape
    ch = w_dw.shape[1]
    tt, halo = CONV_TT, BF16_SUBLANES
    hb = tt // halo
    n_hb = t // halo
    cur = lambda col: pl.BlockSpec((1, tt, ch), lambda bi, i: (bi, i, col))
    prv = lambda col: pl.BlockSpec((1, halo, ch), lambda bi, i: (bi, jnp.maximum(i * hb - 1, 0), col))
    nxt = lambda col: pl.BlockSpec((1, halo, ch), lambda bi, i: (bi, jnp.minimum((i + 1) * hb, n_hb - 1), col))
    vec = lambda rows: pl.BlockSpec((rows, ch), lambda bi, i: (0, 0))
    return pl.pallas_call(
        functools.partial(_conv_kernel, tt=tt, ch=ch),
        grid=(b, t // tt),
        in_specs=[cur(0), cur(1), prv(0), prv(1), nxt(0), nxt(1),
                  pl.BlockSpec((CONV_WIDTH, SUBLANES, ch), lambda bi, i: (0, 0, 0)), vec(1), vec(1), vec(1)],
        out_specs=pl.BlockSpec((1, tt, ch), lambda bi, i: (bi, i, 0)),
        out_shape=jax.ShapeDtypeStruct((b, t, ch), BF16),
        scratch_shapes=[pltpu.VMEM((SUBLANES, tt + 2 * halo, ch), F32), pltpu.VMEM((tt, ch), F32)],
        compiler_params=_cparams(("arbitrary", "arbitrary")),
        name="conformer_conv",
    )(proj3, proj3, proj3, proj3, proj3, proj3,
      jnp.broadcast_to(w_dw[:, None, :], (CONV_WIDTH, SUBLANES, ch)), b_dw.reshape(1, ch), ln_g.reshape(1, ch),
      ln_b.reshape(1, ch))


def _nat_bias_table(rpb):
    h = rpb.shape[0]
    kc = NAT_WIN_COLS
    cols = jnp.arange(GRID_W)
    col_start = jnp.clip(cols - kc // 2, 0, GRID_W - kc)
    col_mask = (cols[None, :] >= col_start[:, None]) & (cols[None, :] < col_start[:, None] + kc)
    dc_idx = jnp.clip(cols[None, :] - cols[:, None], -(kc - 1), kc - 1) + (kc - 1)
    bc = jnp.where(col_mask[None, None], rpb[:, :, dc_idx].astype(F32), NEG)
    blank = jnp.full((h, 1, GRID_W, GRID_W), NEG, F32)
    bc = jnp.concatenate([blank, bc, blank], axis=1)
    return jnp.concatenate([bc[:, :-1], bc[:, 1:]], axis=-1)


def _nat_kernel(q_ref, k_ref, v_ref, tab_ref, o_ref, bias_ref, *, rows):
    blk = pl.program_id(2)
    r_base = blk * NAT_QROWS
    ws = jnp.clip(r_base - NAT_WIN_ROWS // 2, 0, rows - NAT_KROWS)
    lane = lax.broadcasted_iota(jnp.int32, (GRID_W, 2 * GRID_W), 1)

    @pl.when((blk <= 1) | (blk == pl.num_programs(2) - 1))
    def _():
        for qr in range(NAT_QROWS):
            r = r_base + qr
            r0 = jnp.clip(r - NAT_WIN_ROWS // 2, 0, rows - NAT_WIN_ROWS)
            for p in range(NAT_KROWS // 2):
                kr = ws + 2 * p
                entry = jnp.clip(kr - r + NAT_WIN_ROWS, 0, 2 * NAT_WIN_ROWS - 1)
                ok0 = ((kr >= r0) & (kr < r0 + NAT_WIN_ROWS)).astype(jnp.int32)
                ok1 = ((kr + 1 >= r0) & (kr + 1 < r0 + NAT_WIN_ROWS)).astype(jnp.int32)
                ok = jnp.where(lane < GRID_W, ok0, ok1) > 0
                bias_ref[qr * GRID_W:(qr + 1) * GRID_W, p * 2 * GRID_W:(p + 1) * 2 * GRID_W] = (
                    jnp.where(ok, tab_ref[0, entry], NEG))

    start = pl.multiple_of(ws * GRID_W, 256)
    k = k_ref[0, pl.ds(start, NAT_KROWS * GRID_W), :]
    v = v_ref[0, pl.ds(start, NAT_KROWS * GRID_W), :]
    s = lax.dot_general(q_ref[0], k, (((1,), (1,)), ((), ())), preferred_element_type=F32)
    s = s * (HEAD_DIM ** -0.5) + bias_ref[...]
    m = jnp.max(s, axis=-1, keepdims=True)
    p = jnp.exp(s - m)
    den = jnp.sum(p, axis=-1, keepdims=True)
    o = jnp.dot(p.astype(BF16), v, preferred_element_type=F32)
    o_ref[0] = (o / den).astype(o_ref.dtype)


def neighbourhood_attention(proj3, rpb, col0):
    b, t, _ = proj3.shape
    nh = rpb.shape[0]
    rows = t // GRID_W
    assert rows >= NAT_KROWS and rows % NAT_QROWS == 0
    tq = NAT_QROWS * GRID_W
    tab = _nat_bias_table(rpb)
    return pl.pallas_call(
        functools.partial(_nat_kernel, rows=rows),
        grid=(nh, b, rows // NAT_QROWS),
        in_specs=[pl.BlockSpec((1, tq, HEAD_DIM), lambda h, bi, i: (bi, i, col0 + h)),
                  pl.BlockSpec((1, t, HEAD_DIM), lambda h, bi, i: (bi, 0, col0 + nh + h)),
                  pl.BlockSpec((1, t, HEAD_DIM), lambda h, bi, i: (bi, 0, col0 + 2 * nh + h)),
                  pl.BlockSpec((1, 2 * NAT_WIN_ROWS, GRID_W, 2 * GRID_W), lambda h, bi, i: (h, 0, 0, 0))],
        out_specs=pl.BlockSpec((1, tq, HEAD_DIM), lambda h, bi, i: (bi, i, h)),
        out_shape=jax.ShapeDtypeStruct((b, t, nh * HEAD_DIM), BF16),
        scratch_shapes=[pltpu.VMEM((tq, NAT_KROWS * GRID_W), F32)],
        compiler_params=_cparams(("arbitrary", "arbitrary", "arbitrary")),
        name="neighbourhood_attention",
    )(proj3, proj3, proj3, tab)


def _t5_bucket(rel):
    nb = T5_BUCKETS // 2
    max_exact = nb // 2
    n = jnp.abs(rel)
    sign = jnp.where(rel > 0, nb, 0)
    nf = jnp.maximum(n, 1).astype(F32)
    large = max_exact + (jnp.log(nf / max_exact) / math.log(T5_MAX_DIST / max_exact)
                         * (nb - max_exact)).astype(jnp.int32)
    large = jnp.minimum(large, nb - 1)
    return sign + jnp.where(n < max_exact, n, large)


def _dil_dims(g):
    win, dil = DIL_CONFIGS[g]
    assert win == 2 * dil * DIL_HALF
    mq = min(DIL_MQ, DIL_TQ // dil)
    wk = -(-(mq + 2 * DIL_HALF) // 128) * 128
    front = dil * DIL_HALF
    back = dil * (wk - mq - DIL_HALF)
    return dil, mq, wk, front, back


def _dil_bias_rows(t5_bias, g, nh):
    dil, mq, wk, _, _ = _dil_dims(g)
    delta = jnp.arange(wk) - DIL_HALF
    on = delta <= DIL_HALF
    vals = t5_bias[:, g * nh:(g + 1) * nh][_t5_bucket(delta * dil)].astype(F32)
    return jnp.where(on[None], vals.T, NEG)[:, None, :]


def _dil_kernel(q0_ref, q1_ref, q2_ref, u0_ref, u1_ref, u2_ref, proj_ref, o_ref,
                k0, v0, k1, v1, k2, v2, b0_ref, b1_ref, b2_ref, m_ref, l_ref, n_ref, sem, *, t, nh):
    h, bi, i = pl.program_id(0), pl.program_id(1), pl.program_id(2)
    kv = ((k0, v0), (k1, v1), (k2, v2))
    scale = HEAD_DIM ** -0.5

    @pl.when((bi == 0) & (i == 0))
    def _():
        for u_ref, b_ref in ((u0_ref, b0_ref), (u1_ref, b1_ref), (u2_ref, b2_ref)):
            rows = jnp.broadcast_to(u_ref[0], b_ref.shape)
            b_ref[...] = pltpu.roll(rows, 0, 1, stride=1, stride_axis=0)

    def src(g, which):
        col = pl.multiple_of(((g * 3 + 1 + which) * nh + h) * HEAD_DIM, HEAD_DIM)
        return proj_ref.at[bi, :, pl.ds(col, HEAD_DIM)]

    @pl.when(i == 0)
    def _():
        cps = []
        for g in range(3):
            _, _, _, front, back = _dil_dims(g)
            for w in range(2):
                cps.append(pltpu.make_async_copy(src(g, w), kv[g][w].at[pl.ds(front, t), :], sem.at[g * 2 + w]))
                cps[-1].start()
                kv[g][w][0:front, :] = jnp.zeros((front, HEAD_DIM), BF16)
                kv[g][w][front + t:front + t + back, :] = jnp.zeros((back, HEAD_DIM), BF16)
        for cp in cps:
            cp.wait()

    t0 = pl.multiple_of(i * DIL_TQ, DIL_TQ)

    def class_rows(ref, first_block_row, n_blocks, dil, r):
        n = PERM_BLOCK // dil
        parts = []
        for blk in range(n_blocks):
            start = first_block_row + blk * PERM_BLOCK + r * n
            if not isinstance(start, int):
                start = pl.multiple_of(start, n)
            parts.append(ref[pl.ds(start, n), :])
        return jnp.concatenate(parts, axis=0)

    def group_tiles(g, q_ref, b_ref):
        dil, mq, wk, front, _ = _dil_dims(g)
        if dil == 1:
            tiles = [(q_ref[0, j * mq:(j + 1) * mq, :], t0 + j * mq, 0, slice(j * mq, (j + 1) * mq))
                     for j in range(DIL_TQ // mq)]
        else:
            assert dil * mq == DIL_TQ
            tiles = [(class_rows(q_ref.at[0], 0, DIL_TQ // PERM_BLOCK, dil, r), t0, r,
                      pl.ds(r, mq, stride=dil)) for r in range(dil)]

        def keys_values(which, row0, r):
            if dil == 1:
                return kv[g][which][pl.ds(pl.multiple_of(row0, 128), wk), :]
            return class_rows(kv[g][which], row0, wk * dil // PERM_BLOCK, dil, r)

        scores = []
        for q, row0, r, _ in tiles:
            s = lax.dot_general(q, keys_values(0, row0, r), (((1,), (1,)), ((), ())),
                                preferred_element_type=F32)
            kpos = (row0 - front + r) + dil * lax.broadcasted_iota(jnp.int32, (1, wk), 1)
            kmask = jnp.where((kpos >= 0) & (kpos < t), 0.0, NEG)
            scores.append(s * scale + b_ref[...] + kmask)
        stats = []
        for s in scores:
            m = jnp.max(s, axis=-1, keepdims=True)
            p = jnp.exp(s - m)
            stats.append((m, jnp.sum(p, axis=-1, keepdims=True), p.astype(BF16)))
        for (_, row0, r, rows), (m, den, p) in zip(tiles, stats):
            n_ref[g, rows, :] = jnp.dot(p, keys_values(1, row0, r), preferred_element_type=F32)
            m_ref[g, rows, :] = jnp.broadcast_to(m, (mq, HEAD_DIM))
            l_ref[g, rows, :] = jnp.broadcast_to(den, (mq, HEAD_DIM))

    for g, q_ref, b_ref in ((0, q0_ref, b0_ref), (1, q1_ref, b1_ref), (2, q2_ref, b2_ref)):
        group_tiles(g, q_ref, b_ref)

    m_all = jnp.maximum(jnp.maximum(m_ref[0], m_ref[1]), m_ref[2])
    num = jnp.zeros((DIL_TQ, HEAD_DIM), F32)
    den = jnp.zeros((DIL_TQ, HEAD_DIM), F32)
    for g in range(3):
        w = jnp.exp(m_ref[g] - m_all)
        num = num + w * n_ref[g]
        den = den + w * l_ref[g]
    o_ref[0] = (num / den).astype(o_ref.dtype)


def dilated_attention(proj3, t5_bias, nh):
    b, t, _ = proj3.shape
    assert t % DIL_TQ == 0
    tabs = [_dil_bias_rows(t5_bias, g, nh) for g in range(3)]
    qspec = lambda g: pl.BlockSpec((1, DIL_TQ, HEAD_DIM), lambda h, bi, i: (bi, i, g * 3 * nh + h))
    bspec = lambda g: pl.BlockSpec((1, 1, _dil_dims(g)[2]), lambda h, bi, i: (h, 0, 0))
    scratch = []
    for g in range(3):
        _, _, _, front, back = _dil_dims(g)
        assert g == 0 or (front % PERM_BLOCK == 0 and back % PERM_BLOCK == 0)
        scratch += [pltpu.VMEM((front + t + back, HEAD_DIM), BF16)] * 2
    scratch += [pltpu.VMEM(_dil_dims(g)[1:3], F32) for g in range(3)]
    scratch += [pltpu.VMEM((3, DIL_TQ, HEAD_DIM), F32)] * 3
    scratch.append(pltpu.SemaphoreType.DMA((6,)))
    return pl.pallas_call(
        functools.partial(_dil_kernel, t=t, nh=nh),
        grid=(nh, b, t // DIL_TQ),
        in_specs=[qspec(0), qspec(1), qspec(2), bspec(0), bspec(1), bspec(2),
                  pl.BlockSpec(memory_space=pl.ANY)],
        out_specs=pl.BlockSpec((1, DIL_TQ, HEAD_DIM), lambda h, bi, i: (bi, i, h)),
        out_shape=jax.ShapeDtypeStruct((b, t, nh * HEAD_DIM), BF16),
        scratch_shapes=scratch,
        compiler_params=_cparams(("arbitrary", "arbitrary", "arbitrary")),
        name="dilated_attention",
    )(proj3, proj3, proj3, tabs[0], tabs[1], tabs[2], proj3)


ROUTER_LANES = 128


def _pack_pair(lo, hi):
    lo_bits = lax.bitcast_convert_type(lo.astype(BF16).astype(F32), jnp.uint32)
    hi_bits = lax.bitcast_convert_type(hi.astype(BF16).astype(F32), jnp.uint32)
    return lax.shift_right_logical(lo_bits, jnp.uint32(16)) | (hi_bits & jnp.uint32(0xFFFF0000))


def _unpack_pair(word):
    lo = lax.bitcast_convert_type(lax.shift_left(word, jnp.uint32(16)), F32)
    hi = lax.bitcast_convert_type(word & jnp.uint32(0xFFFF0000), F32)
    return lo, hi


def _router_kernel(x_ref, g_ref, whi_ref, wlo_ref, b_ref, o_ref, hp_ref, cnt_ref, carry_ref):
    step = pl.program_id(0)
    half = x_ref.shape[1] // 2
    x = x_ref[...]
    ms = jnp.mean(x * x, axis=-1, keepdims=True)
    h = x * lax.rsqrt(ms + RMS_EPS) * g_ref[...]
    hp_ref[...] = _pack_pair(h[:, :half], h[:, half:])
    h_hi = h.astype(BF16)
    h_lo = (h - h_hi.astype(F32)).astype(BF16)
    logits = (jnp.dot(h_hi, whi_ref[...], preferred_element_type=F32)
              + jnp.dot(h_lo, whi_ref[...], preferred_element_type=F32)
              + jnp.dot(h_hi, wlo_ref[...], preferred_element_type=F32)) + b_ref[...]

    lane = lax.broadcasted_iota(jnp.int32, logits.shape, 1)
    big = jnp.int32(ROUTER_LANES)

    def first_argmax(vals):
        top = jnp.max(vals, axis=-1, keepdims=True)
        return top, jnp.min(jnp.where(vals == top, lane, big), axis=-1, keepdims=True)

    gl = jnp.where(lane < N_GROUPS, logits, NEG)
    g_top, g_idx = first_argmax(gl)
    g_val = 1.0 / jnp.sum(jnp.exp(gl - g_top), axis=-1, keepdims=True)
    in_group = (lane >= N_GROUPS) & (lane < N_GROUPS + N_EXPERTS) & (
        lax.shift_right_arithmetic(lane - N_GROUPS, jnp.int32(3)) == g_idx)
    el = jnp.where(in_group, logits, NEG)
    v1, i1 = first_argmax(el)
    el2 = jnp.where(lane == i1, NEG, el)
    v2, i2 = first_argmax(el2)
    e21 = jnp.exp(v2 - v1)
    gate1 = g_val / (1.0 + e21)
    gate2 = g_val * e21 / (1.0 + e21)

    @pl.when(step == 0)
    def _():
        carry_ref[...] = jnp.zeros(carry_ref.shape, F32)

    e1, e2 = i1 - N_GROUPS, i2 - N_GROUPS
    oh1 = (lane == e1).astype(F32)
    oh2 = (lane == e2).astype(F32)
    both = oh1 + oh2
    tm = x.shape[0]
    tri = (lax.broadcasted_iota(jnp.int32, (tm, tm), 1) < lax.broadcasted_iota(jnp.int32, (tm, tm), 0))
    before = jnp.dot(tri.astype(BF16), both.astype(BF16), preferred_element_type=F32) + carry_ref[...]
    rank1 = jnp.sum(oh1 * before, axis=-1, keepdims=True)
    rank2 = jnp.sum(oh2 * before, axis=-1, keepdims=True)
    carry_ref[...] = carry_ref[...] + jnp.sum(both, axis=0, keepdims=True)
    cnt_ref[...] = jnp.broadcast_to(carry_ref[...], cnt_ref.shape)

    out = jnp.where(lane == 0, gate1, 0.0)
    out = jnp.where(lane == 1, gate2, out)
    out = jnp.where(lane == 2, e1.astype(F32), out)
    out = jnp.where(lane == 3, e2.astype(F32), out)
    out = jnp.where(lane == 4, rank1, out)
    out = jnp.where(lane == 5, rank2, out)
    o_ref[...] = out


def router(x2, gain, w_group, b_group, w_router, b_router):
    n, d = x2.shape
    wcat = jnp.concatenate([w_group, w_router], axis=1).astype(F32)
    wcat = jnp.pad(wcat, ((0, 0), (0, ROUTER_LANES - wcat.shape[1])))
    w_hi = wcat.astype(BF16)
    w_lo = (wcat - w_hi.astype(F32)).astype(BF16)
    bias = jnp.concatenate([b_group.reshape(-1), b_router.reshape(-1)]).astype(F32)
    bias = jnp.pad(bias, (0, ROUTER_LANES - bias.shape[0])).reshape(1, ROUTER_LANES)
    tm = ROUTER_TM
    return pl.pallas_call(
        _router_kernel,
        grid=(n // tm,),
        in_specs=[pl.BlockSpec((tm, d), lambda i: (i, 0)),
                  pl.BlockSpec((1, d), lambda i: (0, 0)),
                  pl.BlockSpec((d, ROUTER_LANES), lambda i: (0, 0)),
                  pl.BlockSpec((d, ROUTER_LANES), lambda i: (0, 0)),
                  pl.BlockSpec((1, ROUTER_LANES), lambda i: (0, 0))],
        out_specs=[pl.BlockSpec((tm, ROUTER_LANES), lambda i: (i, 0)),
                   pl.BlockSpec((tm, d // 2), lambda i: (i, 0)),
                   pl.BlockSpec((8, ROUTER_LANES), lambda i: (0, 0))],
        out_shape=[jax.ShapeDtypeStruct((n, ROUTER_LANES), F32),
                   jax.ShapeDtypeStruct((n, d // 2), jnp.uint32),
                   jax.ShapeDtypeStruct((8, ROUTER_LANES), F32)],
        scratch_shapes=[pltpu.VMEM((1, ROUTER_LANES), F32)],
        compiler_params=_cparams(("arbitrary",)),
        name="moe_router",
    )(x2, gain.reshape(1, d), w_hi, w_lo, bias)


def dispatch(route, counts, n):
    n_blocks = n * TOP_K // MOE_TM + N_EXPERTS
    counts = counts[0, :N_EXPERTS].astype(jnp.int32)
    pcounts = (counts + MOE_TM - 1) // MOE_TM * MOE_TM
    pends = jnp.cumsum(pcounts)
    pstarts = pends - pcounts
    expert = route[:, 2:2 + TOP_K].astype(jnp.int32)
    rank = route[:, 2 + TOP_K:2 + 2 * TOP_K].astype(jnp.int32)
    ids = jnp.arange(N_EXPERTS, dtype=jnp.int32)
    base = jnp.sum(jnp.where(expert[..., None] == ids, pstarts, 0), axis=-1)
    slot_of = (base + rank).reshape(-1)
    blk_start = jnp.arange(n_blocks, dtype=jnp.int32) * MOE_TM
    block_e = jnp.minimum(jnp.sum(blk_start[:, None] >= pends[None, :], axis=-1), N_EXPERTS - 1)
    n_on = (pends[-1] // MOE_TM).reshape(1)
    pad_from = pstarts + counts
    return slot_of, block_e.astype(jnp.int32), n_on.astype(jnp.int32), pad_from, pends


def _scatter_kernel(slot_ref, padfrom_ref, pend_ref, hp_ref, xs_hbm, zero_ref, sem, zsem):
    i = pl.program_id(0)
    tm = SCAT_TM
    n_blocks = xs_hbm.shape[0] // MOE_TM

    def pad_rows(fn):
        for e in range(N_EXPERTS):
            def body(r, carry):
                fn(pltpu.make_async_copy(zero_ref.at[pl.ds(0, 1), :], xs_hbm.at[pl.ds(r, 1), :], zsem))
                return carry
            lax.fori_loop(padfrom_ref[e], pend_ref[e], body, 0)

        def blk(b, carry):
            start = pl.multiple_of(b * MOE_TM, MOE_TM)
            fn(pltpu.make_async_copy(zero_ref, xs_hbm.at[pl.ds(start, MOE_TM), :], zsem))
            return carry
        lax.fori_loop(pend_ref[N_EXPERTS - 1] // MOE_TM, n_blocks, blk, 0)

    @pl.when(i == 0)
    def _():
        zero_ref[...] = jnp.zeros(zero_ref.shape, zero_ref.dtype)
        pad_rows(lambda cp: cp.start())

    def body(s, carry):
        for k in range(TOP_K):
            row = slot_ref[(i * tm + s) * TOP_K + k]
            pltpu.make_async_copy(hp_ref.at[pl.ds(s, 1), :], xs_hbm.at[pl.ds(row, 1), :], sem).start()
        return carry
    lax.fori_loop(0, tm, body, 0, unroll=8)
    for k in range(TOP_K):
        pltpu.make_async_copy(hp_ref, xs_hbm.at[pl.ds(0, tm), :], sem).wait()

    @pl.when(i == pl.num_programs(0) - 1)
    def _():
        pad_rows(lambda cp: cp.wait())


def scatter_rows(hp, slot_of, pad_from, pends, n_rows):
    n, w = hp.shape
    grid_spec = pltpu.PrefetchScalarGridSpec(
        num_scalar_prefetch=3,
        grid=(n // SCAT_TM,),
        in_specs=[pl.BlockSpec((SCAT_TM, w), lambda i, slots, pf, pe: (i, 0))],
        out_specs=pl.BlockSpec(memory_space=pl.ANY),
        scratch_shapes=[pltpu.VMEM((MOE_TM, w), hp.dtype),
                        pltpu.SemaphoreType.DMA(()), pltpu.SemaphoreType.DMA(())],
    )
    return pl.pallas_call(
        _scatter_kernel,
        grid_spec=grid_spec,
        out_shape=jax.ShapeDtypeStruct((n_rows, w), hp.dtype),
        compiler_params=_cparams(("arbitrary",)),
        name="moe_scatter",
    )(slot_of, pad_from, pends, hp)


def _expert_kernel(be_ref, non_ref, xs_ref, wg_ref, wu_ref, wd_ref, o_ref):
    on = pl.program_id(0) < non_ref[0]

    @pl.when(jnp.logical_not(on))
    def _():
        o_ref[...] = jnp.zeros(o_ref.shape, o_ref.dtype)

    @pl.when(on)
    def _():
        lo, hi = _unpack_pair(xs_ref[...])
        lo, hi = lo.astype(BF16), hi.astype(BF16)
        half = lo.shape[1]

        def proj(w_ref):
            return (jnp.dot(lo, w_ref[0, 0, :half], preferred_element_type=F32)
                    + jnp.dot(hi, w_ref[0, 0, half:], preferred_element_type=F32))

        g = proj(wg_ref)
        u = proj(wu_ref)
        hdn = (g * jax.nn.sigmoid(g) * u).astype(BF16)
        y = jnp.dot(hdn, wd_ref[0, 0], preferred_element_type=F32)
        o_ref[...] = _pack_pair(y[:, :half], y[:, half:])


def expert_blocks(xs, block_e, n_on, layer, w_gate, w_up, w_down):
    n_rows, half = xs.shape
    _, _, d, hid = w_gate.shape
    tm = MOE_TM
    wspec = lambda shape: pl.BlockSpec((1, 1) + shape, lambda b, be, non: (layer, be[b], 0, 0))
    grid_spec = pltpu.PrefetchScalarGridSpec(
        num_scalar_prefetch=2,
        grid=(n_rows // tm,),
        in_specs=[pl.BlockSpec((tm, half), lambda b, be, non: (b, 0)),
                  wspec((d, hid)), wspec((d, hid)), wspec((hid, d))],
        out_specs=pl.BlockSpec((tm, half), lambda b, be, non: (b, 0)),
    )
    return pl.pallas_call(
        _expert_kernel,
        grid_spec=grid_spec,
        out_shape=jax.ShapeDtypeStruct((n_rows, half), xs.dtype),
        compiler_params=_cparams(("arbitrary",)),
        name="moe_experts",
    )(block_e, n_on, xs, w_gate, w_up, w_down)


def _combine_kernel(slot_ref, x_ref, r_ref, gain_ref, y_hbm, *rest, emit_x):
    if emit_x:
        xo_ref, ho_ref, ybuf, sem = rest
    else:
        ho_ref, ybuf, sem = rest
        xo_ref = None
    i = pl.program_id(0)
    ni = pl.num_programs(0)
    slot = i % 2
    tm = COMB_TM

    def gather(blk, slot_):
        def body(s, carry):
            for k in range(TOP_K):
                row = slot_ref[(blk * tm + s) * TOP_K + k]
                pltpu.make_async_copy(y_hbm.at[pl.ds(row, 1), :], ybuf.at[slot_, k, pl.ds(s, 1), :],
                                      sem.at[slot_]).start()
            return carry
        lax.fori_loop(0, tm, body, 0, unroll=8)

    @pl.when(i == 0)
    def _():
        gather(0, 0)

    @pl.when(i + 1 < ni)
    def _():
        gather(i + 1, 1 - slot)

    for k in range(TOP_K):
        pltpu.make_async_copy(y_hbm.at[pl.ds(0, tm), :], ybuf.at[slot, k], sem.at[slot]).wait()

    half = x_ref.shape[1] // 2
    g0, g1 = r_ref[:, 0:1], r_ref[:, 1:2]
    y0_lo, y0_hi = _unpack_pair(ybuf[slot, 0])
    y1_lo, y1_hi = _unpack_pair(ybuf[slot, 1])
    x_lo = x_ref[:, :half] + (g0 * y0_lo + g1 * y1_lo)
    x_hi = x_ref[:, half:] + (g0 * y0_hi + g1 * y1_hi)
    if emit_x:
        xo_ref[:, :half] = x_lo
        xo_ref[:, half:] = x_hi
    ssq = jnp.sum(x_lo * x_lo, axis=-1, keepdims=True) + jnp.sum(x_hi * x_hi, axis=-1, keepdims=True)
    inv = lax.rsqrt(ssq / x_ref.shape[1] + RMS_EPS)
    ho_ref[:, :half] = (x_lo * inv * gain_ref[:, :half]).astype(ho_ref.dtype)
    ho_ref[:, half:] = (x_hi * inv * gain_ref[:, half:]).astype(ho_ref.dtype)


def combine(x2, route, y_rows, slot_of, gain, emit_x, h_dtype):
    n, d = x2.shape
    tm = COMB_TM
    row = lambda i, slots: (i, 0)
    out_shape = [jax.ShapeDtypeStruct((n, d), h_dtype)]
    out_specs = [pl.BlockSpec((tm, d), row)]
    if emit_x:
        out_shape.insert(0, jax.ShapeDtypeStruct((n, d), F32))
        out_specs.insert(0, pl.BlockSpec((tm, d), row))
    grid_spec = pltpu.PrefetchScalarGridSpec(
        num_scalar_prefetch=1,
        grid=(n // tm,),
        in_specs=[pl.BlockSpec((tm, d), row),
                  pl.BlockSpec((tm, ROUTER_LANES), row),
                  pl.BlockSpec((1, d), lambda i, slots: (0, 0)),
                  pl.BlockSpec(memory_space=pl.ANY)],
        out_specs=out_specs,
        scratch_shapes=[pltpu.VMEM((2, TOP_K, tm, d // 2), y_rows.dtype), pltpu.SemaphoreType.DMA((2,))],
    )
    return pl.pallas_call(
        functools.partial(_combine_kernel, emit_x=emit_x),
        grid_spec=grid_spec,
        out_shape=out_shape,
        compiler_params=_cparams(("arbitrary",)),
        name="moe_combine",
    )(slot_of, x2, route, gain.reshape(1, d), y_rows)


def moe_layer(x2, layer, ffn_gain, next_gain, emit_x, h_dtype, w_group, b_group, w_router, b_router,
              w_gate, w_up, w_down):
    n = x2.shape[0]
    route, hp, counts = router(x2, ffn_gain, w_group, b_group, w_router, b_router)
    slot_of, block_e, n_on, pad_from, pends = dispatch(route, counts, n)
    xs = scatter_rows(hp, slot_of, pad_from, pends, block_e.shape[0] * MOE_TM)
    y_rows = expert_blocks(xs, block_e, n_on, layer, w_gate, w_up, w_down)
    return combine(x2, route, y_rows, slot_of, next_gain, emit_x, h_dtype)


def kernel(x, rms_mix, rms_ffn, rms_final, w_in_ab, conv_dw_w, conv_dw_b, conv_ln_g, conv_ln_b, nat_rpb,
           w_out_ab, w_in_c, w_out_c, t5_bias, moe_w_group, moe_b_group, moe_w_router, moe_b_router,
           moe_w_gate, moe_w_up, moe_w_down):
    b, t, d = x.shape
    n = b * t
    x2 = x.reshape(n, d)
    conv_ch = conv_dw_w.shape[-1]
    nat_heads = nat_rpb.shape[1]
    dil_heads = w_out_c.shape[1] // HEAD_DIM

    h = rmsnorm(x2, rms_mix[0], BF16)
    proj = matmul([h], w_in_ab[0], BF16, name="in_proj_ab").reshape(b, t, -1)
    a_out = conformer_conv(proj, conv_dw_w[0], conv_dw_b[0], conv_ln_g[0], conv_ln_b[0])
    b_out = neighbourhood_attention(proj, nat_rpb[0], 2 * conv_ch // HEAD_DIM)
    x2 = matmul([a_out.reshape(n, -1), b_out.reshape(n, -1)], w_out_ab[0], F32, residual=x2,
                name="out_proj_ab")
    wg, wu, wd = moe_w_gate.astype(BF16), moe_w_up.astype(BF16), moe_w_down.astype(BF16)
    x2, h = moe_layer(x2, 0, rms_ffn[0], rms_mix[1], True, BF16, moe_w_group[0], moe_b_group[0],
                      moe_w_router[0], moe_b_router[0], wg, wu, wd)

    proj = matmul([h], w_in_c[0], BF16, perm_dils=tuple(dil for _, dil in DIL_CONFIGS),
                  name="in_proj_c").reshape(b, t, -1)
    o = dilated_attention(proj, t5_bias, dil_heads)
    x2 = matmul([o.reshape(n, -1)], w_out_c[0], F32, residual=x2, name="out_proj_c")
    (y,) = moe_layer(x2, 1, rms_ffn[1], rms_final, False, x.dtype, moe_w_group[1], moe_b_group[1],
                     moe_w_router[1], moe_b_router[1], wg, wu, wd)
    return y.reshape(b, t, d)
```

```python
import functools
import math

import jax
import jax.numpy as jnp
from jax import lax
from jax.experimental import pallas as pl
from jax.experimental.pallas import tpu as pltpu

F32 = jnp.float32
BF16 = jnp.bfloat16

HEAD_DIM = 128
CONV_WIDTH = 31
CONV_PAD = CONV_WIDTH // 2
NAT_WIN_ROWS = 8
NAT_WIN_COLS = 16
GRID_W = 64
DIL_CONFIGS = ((128, 1), (512, 4), (2048, 16))
T5_BUCKETS = 32
T5_MAX_DIST = 1024
N_GROUPS = 4
EXPERTS_PER_GROUP = 8
N_EXPERTS = N_GROUPS * EXPERTS_PER_GROUP
TOP_K = 2
RMS_EPS = 1e-6
LN_EPS = 1e-5
NEG = -1e30

VMEM_LIMIT = 56 * 1024 * 1024
SUBLANES = 8
BF16_SUBLANES = 16

MM_TM = 1024
MM_TN = 512
NORM_TM = 256
CONV_TT = 128
CONV_RB = 32
CONV_CB = 512
NAT_QROWS = 8
NAT_SUBROWS = 4
NAT_KROWS = 12
DIL_TQ = 1024
DIL_MQ = 256
DIL_HALF = 64
PERM_BLOCK = 256
MOE_TM = 256
SCAT_TM = 512
COMB_TM = 128
ROUTER_TM = 256


def _cparams(sem):
    return pltpu.CompilerParams(dimension_semantics=sem, vmem_limit_bytes=VMEM_LIMIT)


def _rmsnorm_kernel(x_ref, g_ref, o_ref):
    x = x_ref[...]
    ms = jnp.mean(x * x, axis=-1, keepdims=True)
    o_ref[...] = (x * lax.rsqrt(ms + RMS_EPS) * g_ref[...]).astype(o_ref.dtype)


def rmsnorm(x2, g, out_dtype):
    n, d = x2.shape
    return pl.pallas_call(
        _rmsnorm_kernel,
        grid=(n // NORM_TM,),
        in_specs=[pl.BlockSpec((NORM_TM, d), lambda i: (i, 0)),
                  pl.BlockSpec((1, d), lambda i: (0, 0))],
        out_specs=pl.BlockSpec((NORM_TM, d), lambda i: (i, 0)),
        out_shape=jax.ShapeDtypeStruct((n, d), out_dtype),
        compiler_params=_cparams(("arbitrary",)),
        name="rmsnorm",
    )(x2, g.reshape(1, d))


def _row_perm_matrix(dil):
    n = PERM_BLOCK // dil
    out = jnp.arange(PERM_BLOCK)
    src = dil * (out % n) + out // n
    return (src[:, None] == jnp.arange(PERM_BLOCK)[None, :]).astype(BF16)


def _mm_kernel(*refs, n_x, has_res, perm_dils, tiles_per_group):
    x_refs = refs[:n_x]
    w_refs = refs[n_x:2 * n_x]
    pos = 2 * n_x
    r_ref = refs[pos] if has_res else None
    pos += int(has_res)
    n_perm = sum(d > 1 for d in perm_dils)
    p_refs = refs[pos:pos + n_perm]
    pos += n_perm
    o_ref, wbf_ref = refs[pos], refs[pos + 1]

    @pl.when(pl.program_id(1) == 0)
    def _():
        off = 0
        for w_ref in w_refs:
            kk = w_ref.shape[0]
            wbf_ref[off:off + kk, :] = w_ref[...].astype(BF16)
            off += kk

    acc = None
    off = 0
    for x_ref in x_refs:
        kk = x_ref.shape[1]
        part = jnp.dot(x_ref[...], wbf_ref[off:off + kk, :], preferred_element_type=F32)
        acc = part if acc is None else acc + part
        off += kk
    if has_res:
        acc = acc + r_ref[...]
    out = acc.astype(o_ref.dtype)
    if not p_refs:
        o_ref[...] = out
        return

    group = pl.program_id(0) // tiles_per_group
    p_iter = iter(p_refs)
    for g, dil in enumerate(perm_dils):
        p_ref = next(p_iter) if dil > 1 else None

        @pl.when(group == g)
        def _(p_ref=p_ref):
            if p_ref is None:
                o_ref[...] = out
            else:
                for blk in range(out.shape[0] // PERM_BLOCK):
                    rows = slice(blk * PERM_BLOCK, (blk + 1) * PERM_BLOCK)
                    o_ref[rows, :] = jnp.dot(p_ref[...], out[rows, :],
                                             preferred_element_type=F32).astype(o_ref.dtype)


def matmul(xs, w, out_dtype, residual=None, perm_dils=(), name="matmul"):
    m = xs[0].shape[0]
    ks = [x.shape[1] for x in xs]
    ktot, nc = w.shape
    assert sum(ks) == ktot and all(k == ks[0] for k in ks)
    tm, tn = min(MM_TM, m), min(MM_TN, nc)
    assert m % tm == 0 and nc % tn == 0
    in_specs = [pl.BlockSpec((tm, k), lambda j, i: (i, 0)) for k in ks]
    in_specs += [pl.BlockSpec((k, tn), functools.partial(lambda j, i, p: (p, j), p=p))
                 for p, k in enumerate(ks)]
    args = list(xs) + [w] * len(xs)
    if residual is not None:
        in_specs.append(pl.BlockSpec((tm, tn), lambda j, i: (i, j)))
        args.append(residual)
    tiles_per_group = 0
    if perm_dils:
        assert out_dtype == BF16 and tm % PERM_BLOCK == 0 and (nc // tn) % len(perm_dils) == 0
        tiles_per_group = (nc // tn) // len(perm_dils)
        for dil in perm_dils:
            if dil > 1:
                in_specs.append(pl.BlockSpec((PERM_BLOCK, PERM_BLOCK), lambda j, i: (0, 0)))
                args.append(_row_perm_matrix(dil))
    return pl.pallas_call(
        functools.partial(_mm_kernel, n_x=len(xs), has_res=residual is not None,
                          perm_dils=tuple(perm_dils), tiles_per_group=tiles_per_group),
        grid=(nc // tn, m // tm),
        in_specs=in_specs,
        out_specs=pl.BlockSpec((tm, tn), lambda j, i: (i, j)),
        out_shape=jax.ShapeDtypeStruct((m, nc), out_dtype),
        scratch_shapes=[pltpu.VMEM((ktot, tn), BF16)],
        compiler_params=_cparams(("arbitrary", "arbitrary")),
        name=name,
    )(*args)


def _conv_kernel(ac_ref, gc_ref, ap_ref, gp_ref, an_ref, gn_ref, w_ref, b_ref, lg_ref, lb_ref,
                 o_ref, buf_ref, cv_ref, *, tt, ch):
    i = pl.program_id(1)
    last = pl.num_programs(1) - 1
    halo = BF16_SUBLANES

    def glu(a_ref, g_ref):
        return a_ref[0].astype(F32) * jax.nn.sigmoid(g_ref[0].astype(F32))

    buf_ref[0, 0:halo, :] = jnp.where(i > 0, glu(ap_ref, gp_ref), 0.0)
    buf_ref[0, halo:halo + tt, :] = glu(ac_ref, gc_ref)
    buf_ref[0, halo + tt:2 * halo + tt, :] = jnp.where(i < last, glu(an_ref, gn_ref), 0.0)
    span = tt + 2 * halo - SUBLANES
    for s in range(1, SUBLANES):
        buf_ref[s, 0:span, :] = buf_ref[0, s:s + span, :]

    base = halo - CONV_PAD
    for cb in range(ch // CONV_CB):
        cs = slice(cb * CONV_CB, (cb + 1) * CONV_CB)
        for rb in range(tt // CONV_RB):
            accs = [jnp.zeros((SUBLANES, CONV_CB), F32) for _ in range(CONV_RB // SUBLANES)]
            for j in range(CONV_WIDTH):
                shift = (base + j) % SUBLANES
                r0 = rb * CONV_RB + base + j - shift
                wj = w_ref[j, :, cs]
                for k in range(len(accs)):
                    rk = r0 + k * SUBLANES
                    accs[k] = accs[k] + wj * buf_ref[shift, rk:rk + SUBLANES, cs]
            for k, acc in enumerate(accs):
                rk = rb * CONV_RB + k * SUBLANES
                cv_ref[rk:rk + SUBLANES, cs] = acc + b_ref[:, cs]

    a = cv_ref[...]
    mu = jnp.mean(a, axis=-1, keepdims=True)
    ctr = a - mu
    var = jnp.mean(ctr * ctr, axis=-1, keepdims=True)
    y = ctr * lax.rsqrt(var + LN_EPS) * lg_ref[...] + lb_ref[...]
    o_ref[0] = (y * jax.nn.sigmoid(y)).astype(o_ref.dtype)


def conformer_conv(proj3, w_dw, b_dw, ln_g, ln_b):
    b, t, _ = proj3.shape
    ch = w_dw.shape[1]
    tt, halo = CONV_TT, BF16_SUBLANES
    hb = tt // halo
    n_hb = t // halo
    cur = lambda col: pl.BlockSpec((1, tt, ch), lambda bi, i: (bi, i, col))
    prv = lambda col: pl.BlockSpec((1, halo, ch), lambda bi, i: (bi, jnp.maximum(i * hb - 1, 0), col))
    nxt = lambda col: pl.BlockSpec((1, halo, ch), lambda bi, i: (bi, jnp.minimum((i + 1) * hb, n_hb - 1), col))
    vec = lambda rows: pl.BlockSpec((rows, ch), lambda bi, i: (0, 0))
    return pl.pallas_call(
        functools.partial(_conv_kernel, tt=tt, ch=ch),
        grid=(b, t // tt),
        in_specs=[cur(0), cur(1), prv(0), prv(1), nxt(0), nxt(1),
                  pl.BlockSpec((CONV_WIDTH, SUBLANES, ch), lambda bi, i: (0, 0, 0)), vec(1), vec(1), vec(1)],
        out_specs=pl.BlockSpec((1, tt, ch), lambda bi, i: (bi, i, 0)),
        out_shape=jax.ShapeDtypeStruct((b, t, ch), BF16),
        scratch_shapes=[pltpu.VMEM((SUBLANES, tt + 2 * halo, ch), F32), pltpu.VMEM((tt, ch), F32)],
        compiler_params=_cparams(("arbitrary", "arbitrary")),
        name="conformer_conv",
    )(proj3, proj3, proj3, proj3, proj3, proj3,
      jnp.broadcast_to(w_dw[:, None, :], (CONV_WIDTH, SUBLANES, ch)), b_dw.reshape(1, ch), ln_g.reshape(1, ch),
      ln_b.reshape(1, ch))


def _nat_bias_table(rpb):
    h = rpb.shape[0]
    kc = NAT_WIN_COLS
    cols = jnp.arange(GRID_W)
    col_start = jnp.clip(cols - kc // 2, 0, GRID_W - kc)
    col_mask = (cols[None, :] >= col_start[:, None]) & (cols[None, :] < col_start[:, None] + kc)
    dc_idx = jnp.clip(cols[None, :] - cols[:, None], -(kc - 1), kc - 1) + (kc - 1)
    bc = jnp.where(col_mask[None, None], rpb[:, :, dc_idx].astype(F32), NEG)
    blank = jnp.full((h, 1, GRID_W, GRID_W), NEG, F32)
    bc = jnp.concatenate([blank, bc, blank], axis=1)
    return jnp.concatenate([bc[:, :-1], bc[:, 1:]], axis=-1)


def _nat_kernel(q_ref, k_ref, v_ref, tab_ref, o_ref, bias_ref, *, rows):
    blk = pl.program_id(2)
    r_base = blk * NAT_QROWS
    n_sub = NAT_QROWS // NAT_SUBROWS
    tq = NAT_SUBROWS * GRID_W
    ws = [jnp.clip(r_base + j * NAT_SUBROWS - NAT_WIN_ROWS // 2, 0, rows - NAT_KROWS) for j in range(n_sub)]
    lane = lax.broadcasted_iota(jnp.int32, (GRID_W, 2 * GRID_W), 1)

    @pl.when((blk <= 1) | (blk == pl.num_programs(2) - 1))
    def _():
        for j in range(n_sub):
            for qr in range(NAT_SUBROWS):
                r = r_base + j * NAT_SUBROWS + qr
                r0 = jnp.clip(r - NAT_WIN_ROWS // 2, 0, rows - NAT_WIN_ROWS)
                for p in range(NAT_KROWS // 2):
                    kr = ws[j] + 2 * p
                    entry = jnp.clip(kr - r + NAT_WIN_ROWS, 0, 2 * NAT_WIN_ROWS - 1)
                    ok0 = ((kr >= r0) & (kr < r0 + NAT_WIN_ROWS)).astype(jnp.int32)
                    ok1 = ((kr + 1 >= r0) & (kr + 1 < r0 + NAT_WIN_ROWS)).astype(jnp.int32)
                    ok = jnp.where(lane < GRID_W, ok0, ok1) > 0
                    bias_ref[j, qr * GRID_W:(qr + 1) * GRID_W, p * 2 * GRID_W:(p + 1) * 2 * GRID_W] = (
                        jnp.where(ok, tab_ref[0, entry], NEG))

    def window(ref, j):
        return ref[0, pl.ds(pl.multiple_of(ws[j] * GRID_W, 128), NAT_KROWS * GRID_W), :]

    scores = [lax.dot_general(q_ref[0, j * tq:(j + 1) * tq, :], window(k_ref, j), (((1,), (1,)), ((), ())),
                              preferred_element_type=F32) * (HEAD_DIM ** -0.5) + bias_ref[j]
              for j in range(n_sub)]
    stats = []
    for s in scores:
        m = jnp.max(s, axis=-1, keepdims=True)
        p = jnp.exp(s - m)
        stats.append((jnp.sum(p, axis=-1, keepdims=True), p.astype(BF16)))
    for j, (den, p) in enumerate(stats):
        o = jnp.dot(p, window(v_ref, j), preferred_element_type=F32)
        o_ref[0, j * tq:(j + 1) * tq, :] = (o / den).astype(o_ref.dtype)


def neighbourhood_attention(proj3, rpb, col0):
    b, t, _ = proj3.shape
    nh = rpb.shape[0]
    rows = t // GRID_W
    assert rows >= NAT_KROWS and rows % NAT_QROWS == 0
    tq = NAT_QROWS * GRID_W
    tab = _nat_bias_table(rpb)
    return pl.pallas_call(
        functools.partial(_nat_kernel, rows=rows),
        grid=(nh, b, rows // NAT_QROWS),
        in_specs=[pl.BlockSpec((1, tq, HEAD_DIM), lambda h, bi, i: (bi, i, col0 + h)),
                  pl.BlockSpec((1, t, HEAD_DIM), lambda h, bi, i: (bi, 0, col0 + nh + h)),
                  pl.BlockSpec((1, t, HEAD_DIM), lambda h, bi, i: (bi, 0, col0 + 2 * nh + h)),
                  pl.BlockSpec((1, 2 * NAT_WIN_ROWS, GRID_W, 2 * GRID_W), lambda h, bi, i: (h, 0, 0, 0))],
        out_specs=pl.BlockSpec((1, tq, HEAD_DIM), lambda h, bi, i: (bi, i, h)),
        out_shape=jax.ShapeDtypeStruct((b, t, nh * HEAD_DIM), BF16),
        scratch_shapes=[pltpu.VMEM((NAT_QROWS // NAT_SUBROWS, NAT_SUBROWS * GRID_W, NAT_KROWS * GRID_W), F32)],
        compiler_params=_cparams(("arbitrary", "arbitrary", "arbitrary")),
        name="neighbourhood_attention",
    )(proj3, proj3, proj3, tab)


def _t5_bucket(rel):
    nb = T5_BUCKETS // 2
    max_exact = nb // 2
    n = jnp.abs(rel)
    sign = jnp.where(rel > 0, nb, 0)
    nf = jnp.maximum(n, 1).astype(F32)
    large = max_exact + (jnp.log(nf / max_exact) / math.log(T5_MAX_DIST / max_exact)
                         * (nb - max_exact)).astype(jnp.int32)
    large = jnp.minimum(large, nb - 1)
    return sign + jnp.where(n < max_exact, n, large)


def _dil_dims(g):
    win, dil = DIL_CONFIGS[g]
    assert win == 2 * dil * DIL_HALF
    mq = min(DIL_MQ, DIL_TQ // dil)
    wk = -(-(mq + 2 * DIL_HALF) // 128) * 128
    front = dil * DIL_HALF
    back = dil * (wk - mq - DIL_HALF)
    return dil, mq, wk, front, back


def _dil_bias_rows(t5_bias, g, nh):
    dil, mq, wk, _, _ = _dil_dims(g)
    delta = jnp.arange(wk) - DIL_HALF
    on = delta <= DIL_HALF
    vals = t5_bias[:, g * nh:(g + 1) * nh][_t5_bucket(delta * dil)].astype(F32)
    return jnp.where(on[None], vals.T, NEG)[:, None, :]


def _dil_kernel(q0_ref, q1_ref, q2_ref, u0_ref, u1_ref, u2_ref, proj_ref, o_ref,
                k0, v0, k1, v1, k2, v2, b0_ref, b1_ref, b2_ref, m_ref, l_ref, n_ref, sem, *, t, nh):
    h, bi, i = pl.program_id(0), pl.program_id(1), pl.program_id(2)
    kv = ((k0, v0), (k1, v1), (k2, v2))
    scale = HEAD_DIM ** -0.5

    @pl.when((bi == 0) & (i == 0))
    def _():
        for u_ref, b_ref in ((u0_ref, b0_ref), (u1_ref, b1_ref), (u2_ref, b2_ref)):
            rows = jnp.broadcast_to(u_ref[0], b_ref.shape)
            b_ref[...] = pltpu.roll(rows, 0, 1, stride=1, stride_axis=0)

    def src(g, which):
        col = pl.multiple_of(((g * 3 + 1 + which) * nh + h) * HEAD_DIM, HEAD_DIM)
        return proj_ref.at[bi, :, pl.ds(col, HEAD_DIM)]

    @pl.when(i == 0)
    def _():
        cps = []
        for g in range(3):
            _, _, _, front, back = _dil_dims(g)
            for w in range(2):
                cps.append(pltpu.make_async_copy(src(g, w), kv[g][w].at[pl.ds(front, t), :], sem.at[g * 2 + w]))
                cps[-1].start()
                kv[g][w][0:front, :] = jnp.zeros((front, HEAD_DIM), BF16)
                kv[g][w][front + t:front + t + back, :] = jnp.zeros((back, HEAD_DIM), BF16)
        for cp in cps:
            cp.wait()

    t0 = pl.multiple_of(i * DIL_TQ, DIL_TQ)

    def class_rows(ref, first_block_row, n_blocks, dil, r):
        n = PERM_BLOCK // dil
        parts = []
        for blk in range(n_blocks):
            start = first_block_row + blk * PERM_BLOCK + r * n
            if not isinstance(start, int):
                start = pl.multiple_of(start, n)
            parts.append(ref[pl.ds(start, n), :])
        return jnp.concatenate(parts, axis=0)

    def group_tiles(g, q_ref, b_ref):
        dil, mq, wk, front, _ = _dil_dims(g)
        if dil == 1:
            tiles = [(q_ref[0, j * mq:(j + 1) * mq, :], t0 + j * mq, 0, slice(j * mq, (j + 1) * mq))
                     for j in range(DIL_TQ // mq)]
        else:
            assert dil * mq == DIL_TQ
            tiles = [(class_rows(q_ref.at[0], 0, DIL_TQ // PERM_BLOCK, dil, r), t0, r,
                      pl.ds(r, mq, stride=dil)) for r in range(dil)]

        def keys_values(which, row0, r):
            if dil == 1:
                return kv[g][which][pl.ds(pl.multiple_of(row0, 128), wk), :]
            return class_rows(kv[g][which], row0, wk * dil // PERM_BLOCK, dil, r)

        scores = []
        for q, row0, r, _ in tiles:
            s = lax.dot_general(q, keys_values(0, row0, r), (((1,), (1,)), ((), ())),
                                preferred_element_type=F32)
            kpos = (row0 - front + r) + dil * lax.broadcasted_iota(jnp.int32, (1, wk), 1)
            kmask = jnp.where((kpos >= 0) & (kpos < t), 0.0, NEG)
            scores.append(s * scale + b_ref[...] + kmask)
        stats = []
        for s in scores:
            m = jnp.max(s, axis=-1, keepdims=True)
            p = jnp.exp(s - m)
            stats.append((m, jnp.sum(p, axis=-1, keepdims=True), p.astype(BF16)))
        for (_, row0, r, rows), (m, den, p) in zip(tiles, stats):
            n_ref[g, rows, :] = jnp.dot(p, keys_values(1, row0, r), preferred_element_type=F32)
            m_ref[g, rows, :] = jnp.broadcast_to(m, (mq, HEAD_DIM))
            l_ref[g, rows, :] = jnp.broadcast_to(den, (mq, HEAD_DIM))

    for g, q_ref, b_ref in ((0, q0_ref, b0_ref), (1, q1_ref, b1_ref), (2, q2_ref, b2_ref)):
        group_tiles(g, q_ref, b_ref)

    m_all = jnp.maximum(jnp.maximum(m_ref[0], m_ref[1]), m_ref[2])
    num = jnp.zeros((DIL_TQ, HEAD_DIM), F32)
    den = jnp.zeros((DIL_TQ, HEAD_DIM), F32)
    for g in range(3):
        w = jnp.exp(m_ref[g] - m_all)
        num = num + w * n_ref[g]
        den = den + w * l_ref[g]
    o_ref[0] = (num / den).astype(o_ref.dtype)


def dilated_attention(proj3, t5_bias, nh):
    b, t, _ = proj3.shape
    assert t % DIL_TQ == 0
    tabs = [_dil_bias_rows(t5_bias, g, nh) for g in range(3)]
    qspec = lambda g: pl.BlockSpec((1, DIL_TQ, HEAD_DIM), lambda h, bi, i: (bi, i, g * 3 * nh + h))
    bspec = lambda g: pl.BlockSpec((1, 1, _dil_dims(g)[2]), lambda h, bi, i: (h, 0, 0))
    scratch = []
    for g in range(3):
        _, _, _, front, back = _dil_dims(g)
        assert g == 0 or (front % PERM_BLOCK == 0 and back % PERM_BLOCK == 0)
        scratch += [pltpu.VMEM((front + t + back, HEAD_DIM), BF16)] * 2
    scratch += [pltpu.VMEM(_dil_dims(g)[1:3], F32) for g in range(3)]
    scratch += [pltpu.VMEM((3, DIL_TQ, HEAD_DIM), F32)] * 3
    scratch.append(pltpu.SemaphoreType.DMA((6,)))
    return pl.pallas_call(
        functools.partial(_dil_kernel, t=t, nh=nh),
        grid=(nh, b, t // DIL_TQ),
        in_specs=[qspec(0), qspec(1), qspec(2), bspec(0), bspec(1), bspec(2),
                  pl.BlockSpec(memory_space=pl.ANY)],
        out_specs=pl.BlockSpec((1, DIL_TQ, HEAD_DIM), lambda h, bi, i: (bi, i, h)),
        out_shape=jax.ShapeDtypeStruct((b, t, nh * HEAD_DIM), BF16),
        scratch_shapes=scratch,
        compiler_params=_cparams(("arbitrary", "arbitrary", "arbitrary")),
        name="dilated_attention",
    )(proj3, proj3, proj3, tabs[0], tabs[1], tabs[2], proj3)


ROUTER_LANES = 128


def _pack_pair(lo, hi):
    lo_bits = lax.bitcast_convert_type(lo.astype(BF16).astype(F32), jnp.uint32)
    hi_bits = lax.bitcast_convert_type(hi.astype(BF16).astype(F32), jnp.uint32)
    return lax.shift_right_logical(lo_bits, jnp.uint32(16)) | (hi_bits & jnp.uint32(0xFFFF0000))


def _unpack_pair(word):
    lo = lax.bitcast_convert_type(lax.shift_left(word, jnp.uint32(16)), F32)
    hi = lax.bitcast_convert_type(word & jnp.uint32(0xFFFF0000), F32)
    return lo, hi


def _router_kernel(x_ref, g_ref, whi_ref, wlo_ref, b_ref, o_ref, hp_ref, cnt_ref, carry_ref):
    step = pl.program_id(0)
    half = x_ref.shape[1] // 2
    x = x_ref[...]
    ms = jnp.mean(x * x, axis=-1, keepdims=True)
    h = x * lax.rsqrt(ms + RMS_EPS) * g_ref[...]
    hp_ref[...] = _pack_pair(h[:, :half], h[:, half:])
    h_hi = h.astype(BF16)
    h_lo = (h - h_hi.astype(F32)).astype(BF16)
    logits = (jnp.dot(h_hi, whi_ref[...], preferred_element_type=F32)
              + jnp.dot(h_lo, whi_ref[...], preferred_element_type=F32)
              + jnp.dot(h_hi, wlo_ref[...], preferred_element_type=F32)) + b_ref[...]

    lane = lax.broadcasted_iota(jnp.int32, logits.shape, 1)
    big = jnp.int32(ROUTER_LANES)

    def first_argmax(vals):
        top = jnp.max(vals, axis=-1, keepdims=True)
        return top, jnp.min(jnp.where(vals == top, lane, big), axis=-1, keepdims=True)

    gl = jnp.where(lane < N_GROUPS, logits, NEG)
    g_top, g_idx = first_argmax(gl)
    g_val = 1.0 / jnp.sum(jnp.exp(gl - g_top), axis=-1, keepdims=True)
    in_group = (lane >= N_GROUPS) & (lane < N_GROUPS + N_EXPERTS) & (
        lax.shift_right_arithmetic(lane - N_GROUPS, jnp.int32(3)) == g_idx)
    el = jnp.where(in_group, logits, NEG)
    v1, i1 = first_argmax(el)
    el2 = jnp.where(lane == i1, NEG, el)
    v2, i2 = first_argmax(el2)
    e21 = jnp.exp(v2 - v1)
    gate1 = g_val / (1.0 + e21)
    gate2 = g_val * e21 / (1.0 + e21)

    @pl.when(step == 0)
    def _():
        carry_ref[...] = jnp.zeros(carry_ref.shape, F32)

    e1, e2 = i1 - N_GROUPS, i2 - N_GROUPS
    oh1 = (lane == e1).astype(F32)
    oh2 = (lane == e2).astype(F32)
    both = oh1 + oh2
    tm = x.shape[0]
    tri = (lax.broadcasted_iota(jnp.int32, (tm, tm), 1) < lax.broadcasted_iota(jnp.int32, (tm, tm), 0))
    before = jnp.dot(tri.astype(BF16), both.astype(BF16), preferred_element_type=F32) + carry_ref[...]
    rank1 = jnp.sum(oh1 * before, axis=-1, keepdims=True)
    rank2 = jnp.sum(oh2 * before, axis=-1, keepdims=True)
    carry_ref[...] = carry_ref[...] + jnp.sum(both, axis=0, keepdims=True)
    cnt_ref[...] = jnp.broadcast_to(carry_ref[...], cnt_ref.shape)

    out = jnp.where(lane == 0, gate1, 0.0)
    out = jnp.where(lane == 1, gate2, out)
    out = jnp.where(lane == 2, e1.astype(F32), out)
    out = jnp.where(lane == 3, e2.astype(F32), out)
    out = jnp.where(lane == 4, rank1, out)
    out = jnp.where(lane == 5, rank2, out)
    o_ref[...] = out


def router(x2, gain, w_group, b_group, w_router, b_router):
    n, d = x2.shape
    wcat = jnp.concatenate([w_group, w_router], axis=1).astype(F32)
    wcat = jnp.pad(wcat, ((0, 0), (0, ROUTER_LANES - wcat.shape[1])))
    w_hi = wcat.astype(BF16)
    w_lo = (wcat - w_hi.astype(F32)).astype(BF16)
    bias = jnp.concatenate([b_group.reshape(-1), b_router.reshape(-1)]).astype(F32)
    bias = jnp.pad(bias, (0, ROUTER_LANES - bias.shape[0])).reshape(1, ROUTER_LANES)
    tm = ROUTER_TM
    return pl.pallas_call(
        _router_kernel,
        grid=(n // tm,),
        in_specs=[pl.BlockSpec((tm, d), lambda i: (i, 0)),
                  pl.BlockSpec((1, d), lambda i: (0, 0)),
                  pl.BlockSpec((d, ROUTER_LANES), lambda i: (0, 0)),
                  pl.BlockSpec((d, ROUTER_LANES), lambda i: (0, 0)),
                  pl.BlockSpec((1, ROUTER_LANES), lambda i: (0, 0))],
        out_specs=[pl.BlockSpec((tm, ROUTER_LANES), lambda i: (i, 0)),
                   pl.BlockSpec((tm, d // 2), lambda i: (i, 0)),
                   pl.BlockSpec((8, ROUTER_LANES), lambda i: (0, 0))],
        out_shape=[jax.ShapeDtypeStruct((n, ROUTER_LANES), F32),
                   jax.ShapeDtypeStruct((n, d // 2), jnp.uint32),
                   jax.ShapeDtypeStruct((8, ROUTER_LANES), F32)],
        scratch_shapes=[pltpu.VMEM((1, ROUTER_LANES), F32)],
        compiler_params=_cparams(("arbitrary",)),
        name="moe_router",
    )(x2, gain.reshape(1, d), w_hi, w_lo, bias)


def dispatch(route, counts, n):
    n_blocks = n * TOP_K // MOE_TM + N_EXPERTS
    counts = counts[0, :N_EXPERTS].astype(jnp.int32)
    pcounts = (counts + MOE_TM - 1) // MOE_TM * MOE_TM
    pends = jnp.cumsum(pcounts)
    pstarts = pends - pcounts
    expert = route[:, 2:2 + TOP_K].astype(jnp.int32)
    rank = route[:, 2 + TOP_K:2 + 2 * TOP_K].astype(jnp.int32)
    ids = jnp.arange(N_EXPERTS, dtype=jnp.int32)
    base = jnp.sum(jnp.where(expert[..., None] == ids, pstarts, 0), axis=-1)
    slot_of = (base + rank).reshape(-1)
    blk_start = jnp.arange(n_blocks, dtype=jnp.int32) * MOE_TM
    block_e = jnp.minimum(jnp.sum(blk_start[:, None] >= pends[None, :], axis=-1), N_EXPERTS - 1)
    block_e = block_e.astype(jnp.int32)
    n_on = (pends[-1] // MOE_TM).astype(jnp.int32)
    pad_from = pstarts + counts
    blk_ids = jnp.arange(n_blocks, dtype=jnp.int32)
    first = ((blk_ids == 0) | (block_e != jnp.roll(block_e, 1))) & (blk_ids < n_on)
    later = jnp.where((ids[None, :] > ids[:, None]) & (counts[None, :] > 0), ids[None, :], N_EXPERTS)
    next_nonempty = jnp.min(later, axis=-1)
    next_e = jnp.sum(jnp.where(block_e[:, None] == ids, next_nonempty, 0), axis=-1)
    next_e = jnp.where(next_e < N_EXPERTS, next_e, -1).astype(jnp.int32)
    return slot_of, block_e, n_on.reshape(1), first.astype(jnp.int32), next_e, pad_from, pends


def _scatter_kernel(slot_ref, padfrom_ref, pend_ref, hp_ref, xs_hbm, zero_ref, sem, zsem):
    i = pl.program_id(0)
    tm = SCAT_TM
    n_blocks = xs_hbm.shape[0] // MOE_TM

    def pad_rows(fn):
        for e in range(N_EXPERTS):
            def body(r, carry):
                fn(pltpu.make_async_copy(zero_ref.at[pl.ds(0, 1), :], xs_hbm.at[pl.ds(r, 1), :], zsem))
                return carry
            lax.fori_loop(padfrom_ref[e], pend_ref[e], body, 0)

        def blk(b, carry):
            start = pl.multiple_of(b * MOE_TM, MOE_TM)
            fn(pltpu.make_async_copy(zero_ref, xs_hbm.at[pl.ds(start, MOE_TM), :], zsem))
            return carry
        lax.fori_loop(pend_ref[N_EXPERTS - 1] // MOE_TM, n_blocks, blk, 0)

    @pl.when(i == 0)
    def _():
        zero_ref[...] = jnp.zeros(zero_ref.shape, zero_ref.dtype)
        pad_rows(lambda cp: cp.start())

    def body(s, carry):
        for k in range(TOP_K):
            row = slot_ref[(i * tm + s) * TOP_K + k]
            pltpu.make_async_copy(hp_ref.at[pl.ds(s, 1), :], xs_hbm.at[pl.ds(row, 1), :], sem).start()
        return carry
    lax.fori_loop(0, tm, body, 0, unroll=8)
    for k in range(TOP_K):
        pltpu.make_async_copy(hp_ref, xs_hbm.at[pl.ds(0, tm), :], sem).wait()

    @pl.when(i == pl.num_programs(0) - 1)
    def _():
        pad_rows(lambda cp: cp.wait())


def scatter_rows(hp, slot_of, pad_from, pends, n_rows):
    n, w = hp.shape
    grid_spec = pltpu.PrefetchScalarGridSpec(
        num_scalar_prefetch=3,
        grid=(n // SCAT_TM,),
        in_specs=[pl.BlockSpec((SCAT_TM, w), lambda i, slots, pf, pe: (i, 0))],
        out_specs=pl.BlockSpec(memory_space=pl.ANY),
        scratch_shapes=[pltpu.VMEM((MOE_TM, w), hp.dtype),
                        pltpu.SemaphoreType.DMA(()), pltpu.SemaphoreType.DMA(())],
    )
    return pl.pallas_call(
        _scatter_kernel,
        grid_spec=grid_spec,
        out_shape=jax.ShapeDtypeStruct((n_rows, w), hp.dtype),
        compiler_params=_cparams(("arbitrary",)),
        name="moe_scatter",
    )(slot_of, pad_from, pends, hp)


def _expert_kernel(be_ref, non_ref, first_ref, next_ref, xs_ref, wg_hbm, wu_hbm, wd_hbm, o_ref,
                   stage_g, stage_u, stage_d, wg_ref, wu_ref, wd_ref, sem, *, layer):
    b = pl.program_id(0)
    on = b < non_ref[0]
    staged = ((wg_hbm, stage_g), (wu_hbm, stage_u), (wd_hbm, stage_d))

    def weight_copies(e):
        return [pltpu.make_async_copy(w_hbm.at[layer, e], stage, sem.at[k])
                for k, (w_hbm, stage) in enumerate(staged)]

    @pl.when(b == 0)
    def _():
        for cp in weight_copies(be_ref[0]):
            cp.start()

    @pl.when(first_ref[b] > 0)
    def _():
        for cp in weight_copies(be_ref[b]):
            cp.wait()
        for (_, stage), w_ref in zip(staged, (wg_ref, wu_ref, wd_ref)):
            w_ref[...] = stage[...].astype(BF16)

        @pl.when(next_ref[b] >= 0)
        def _():
            for cp in weight_copies(next_ref[b]):
                cp.start()

    @pl.when(jnp.logical_not(on))
    def _():
        o_ref[...] = jnp.zeros(o_ref.shape, o_ref.dtype)

    @pl.when(on)
    def _():
        lo, hi = _unpack_pair(xs_ref[...])
        lo, hi = lo.astype(BF16), hi.astype(BF16)
        half = lo.shape[1]

        def proj(w_ref):
            return (jnp.dot(lo, w_ref[:half, :], preferred_element_type=F32)
                    + jnp.dot(hi, w_ref[half:, :], preferred_element_type=F32))

        g = proj(wg_ref)
        u = proj(wu_ref)
        hdn = (g * jax.nn.sigmoid(g) * u).astype(BF16)
        y = jnp.dot(hdn, wd_ref[...], preferred_element_type=F32)
        o_ref[...] = _pack_pair(y[:, :half], y[:, half:])


def expert_blocks(xs, block_e, n_on, first, next_e, layer, w_gate, w_up, w_down):
    n_rows, half = xs.shape
    _, _, d, hid = w_gate.shape
    tm = MOE_TM
    row = lambda b, *_: (b, 0)
    grid_spec = pltpu.PrefetchScalarGridSpec(
        num_scalar_prefetch=4,
        grid=(n_rows // tm,),
        in_specs=[pl.BlockSpec((tm, half), row)] + [pl.BlockSpec(memory_space=pl.ANY)] * 3,
        out_specs=pl.BlockSpec((tm, half), row),
        scratch_shapes=[pltpu.VMEM((d, hid), F32), pltpu.VMEM((d, hid), F32), pltpu.VMEM((hid, d), F32),
                        pltpu.VMEM((d, hid), BF16), pltpu.VMEM((d, hid), BF16), pltpu.VMEM((hid, d), BF16),
                        pltpu.SemaphoreType.DMA((3,))],
    )
    return pl.pallas_call(
        functools.partial(_expert_kernel, layer=layer),
        grid_spec=grid_spec,
        out_shape=jax.ShapeDtypeStruct((n_rows, half), xs.dtype),
        compiler_params=_cparams(("arbitrary",)),
        name="moe_experts",
    )(block_e, n_on, first, next_e, xs, w_gate, w_up, w_down)


def _combine_kernel(slot_ref, x_ref, r_ref, gain_ref, y_hbm, *rest, emit_x):
    if emit_x:
        xo_ref, ho_ref, ybuf, sem = rest
    else:
        ho_ref, ybuf, sem = rest
        xo_ref = None
    i = pl.program_id(0)
    ni = pl.num_programs(0)
    slot = i % 2
    tm = COMB_TM

    def gather(blk, slot_):
        def body(s, carry):
            for k in range(TOP_K):
                row = slot_ref[(blk * tm + s) * TOP_K + k]
                pltpu.make_async_copy(y_hbm.at[pl.ds(row, 1), :], ybuf.at[slot_, k, pl.ds(s, 1), :],
                                      sem.at[slot_]).start()
            return carry
        lax.fori_loop(0, tm, body, 0, unroll=8)

    @pl.when(i == 0)
    def _():
        gather(0, 0)

    @pl.when(i + 1 < ni)
    def _():
        gather(i + 1, 1 - slot)

    for k in range(TOP_K):
        pltpu.make_async_copy(y_hbm.at[pl.ds(0, tm), :], ybuf.at[slot, k], sem.at[slot]).wait()

    half = x_ref.shape[1] // 2
    g0, g1 = r_ref[:, 0:1], r_ref[:, 1:2]
    y0_lo, y0_hi = _unpack_pair(ybuf[slot, 0])
    y1_lo, y1_hi = _unpack_pair(ybuf[slot, 1])
    x_lo = x_ref[:, :half] + (g0 * y0_lo + g1 * y1_lo)
    x_hi = x_ref[:, half:] + (g0 * y0_hi + g1 * y1_hi)
    if emit_x:
        xo_ref[:, :half] = x_lo
        xo_ref[:, half:] = x_hi
    ssq = jnp.sum(x_lo * x_lo, axis=-1, keepdims=True) + jnp.sum(x_hi * x_hi, axis=-1, keepdims=True)
    inv = lax.rsqrt(ssq / x_ref.shape[1] + RMS_EPS)
    ho_ref[:, :half] = (x_lo * inv * gain_ref[:, :half]).astype(ho_ref.dtype)
    ho_ref[:, half:] = (x_hi * inv * gain_ref[:, half:]).astype(ho_ref.dtype)


def combine(x2, route, y_rows, slot_of, gain, emit_x, h_dtype):
    n, d = x2.shape
    tm = COMB_TM
    row = lambda i, slots: (i, 0)
    out_shape = [jax.ShapeDtypeStruct((n, d), h_dtype)]
    out_specs = [pl.BlockSpec((tm, d), row)]
    if emit_x:
        out_shape.insert(0, jax.ShapeDtypeStruct((n, d), F32))
        out_specs.insert(0, pl.BlockSpec((tm, d), row))
    grid_spec = pltpu.PrefetchScalarGridSpec(
        num_scalar_prefetch=1,
        grid=(n // tm,),
        in_specs=[pl.BlockSpec((tm, d), row),
                  pl.BlockSpec((tm, ROUTER_LANES), row),
                  pl.BlockSpec((1, d), lambda i, slots: (0, 0)),
                  pl.BlockSpec(memory_space=pl.ANY)],
        out_specs=out_specs,
        scratch_shapes=[pltpu.VMEM((2, TOP_K, tm, d // 2), y_rows.dtype), pltpu.SemaphoreType.DMA((2,))],
    )
    return pl.pallas_call(
        functools.partial(_combine_kernel, emit_x=emit_x),
        grid_spec=grid_spec,
        out_shape=out_shape,
        compiler_params=_cparams(("arbitrary",)),
        name="moe_combine",
    )(slot_of, x2, route, gain.reshape(1, d), y_rows)


def moe_layer(x2, layer, ffn_gain, next_gain, emit_x, h_dtype, w_group, b_group, w_router, b_router,
              w_gate, w_up, w_down):
    n = x2.shape[0]
    route, hp, counts = router(x2, ffn_gain, w_group, b_group, w_router, b_router)
    slot_of, block_e, n_on, first, next_e, pad_from, pends = dispatch(route, counts, n)
    xs = scatter_rows(hp, slot_of, pad_from, pends, block_e.shape[0] * MOE_TM)
    y_rows = expert_blocks(xs, block_e, n_on, first, next_e, layer, w_gate, w_up, w_down)
    return combine(x2, route, y_rows, slot_of, next_gain, emit_x, h_dtype)


def kernel(x, rms_mix, rms_ffn, rms_final, w_in_ab, conv_dw_w, conv_dw_b, conv_ln_g, conv_ln_b, nat_rpb,
           w_out_ab, w_in_c, w_out_c, t5_bias, moe_w_group, moe_b_group, moe_w_router, moe_b_router,
           moe_w_gate, moe_w_up, moe_w_down):
    b, t, d = x.shape
    n = b * t
    x2 = x.reshape(n, d)
    conv_ch = conv_dw_w.shape[-1]
    nat_heads = nat_rpb.shape[1]
    dil_heads = w_out_c.shape[1] // HEAD_DIM

    h = rmsnorm(x2, rms_mix[0], BF16)
    proj = matmul([h], w_in_ab[0], BF16, name="in_proj_ab").reshape(b, t, -1)
    a_out = conformer_conv(proj, conv_dw_w[0], conv_dw_b[0], conv_ln_g[0], conv_ln_b[0])
    b_out = neighbourhood_attention(proj, nat_rpb[0], 2 * conv_ch // HEAD_DIM)
    x2 = matmul([a_out.reshape(n, -1), b_out.reshape(n, -1)], w_out_ab[0], F32, residual=x2,
                name="out_proj_ab")
    wg, wu, wd = moe_w_gate, moe_w_up, moe_w_down
    x2, h = moe_layer(x2, 0, rms_ffn[0], rms_mix[1], True, BF16, moe_w_group[0], moe_b_group[0],
                      moe_w_router[0], moe_b_router[0], wg, wu, wd)

    proj = matmul([h], w_in_c[0], BF16, perm_dils=tuple(dil for _, dil in DIL_CONFIGS),
                  name="in_proj_c").reshape(b, t, -1)
    o = dilated_attention(proj, t5_bias, dil_heads)
    x2 = matmul([o.reshape(n, -1)], w_out_c[0], F32, residual=x2, name="out_proj_c")
    (y,) = moe_layer(x2, 1, rms_ffn[1], rms_final, False, x.dtype, moe_w_group[1], moe_b_group[1],
                     moe_w_router[1], moe_b_router[1], wg, wu, wd)
    return y.reshape(b, t, d)
```

```python
import functools
import math

import jax
import jax.numpy as jnp
from jax import lax
from jax.experimental import pallas as pl
from jax.experimental.pallas import tpu as pltpu

F32 = jnp.float32
BF16 = jnp.bfloat16

HEAD_DIM = 128
CONV_WIDTH = 31
CONV_PAD = CONV_WIDTH // 2
NAT_WIN_ROWS = 8
NAT_WIN_COLS = 16
GRID_W = 64
DIL_CONFIGS = ((128, 1), (512, 4), (2048, 16))
T5_BUCKETS = 32
T5_MAX_DIST = 1024
N_GROUPS = 4
EXPERTS_PER_GROUP = 8
N_EXPERTS = N_GROUPS * EXPERTS_PER_GROUP
TOP_K = 2
RMS_EPS = 1e-6
LN_EPS = 1e-5
NEG = -1e30

VMEM_LIMIT = 56 * 1024 * 1024
SUBLANES = 8
BF16_SUBLANES = 16

MM_TM = 1024
MM_TN = 512
NORM_TM = 256
CONV_TT = 128
CONV_RB = 32
CONV_CB = 512
NAT_QROWS = 8
NAT_SUBROWS = 4
NAT_KROWS = 12
DIL_TQ = 1024
DIL_MQ = 256
DIL_HALF = 64
PERM_BLOCK = 256
MOE_TM = 256
SCAT_TM = 512
COMB_TM = 128
ROUTER_TM = 256


def _cparams(sem):
    return pltpu.CompilerParams(dimension_semantics=sem, vmem_limit_bytes=VMEM_LIMIT)


def _rmsnorm_kernel(x_ref, g_ref, o_ref):
    x = x_ref[...]
    ms = jnp.mean(x * x, axis=-1, keepdims=True)
    o_ref[...] = (x * lax.rsqrt(ms + RMS_EPS) * g_ref[...]).astype(o_ref.dtype)


def rmsnorm(x2, g, out_dtype):
    n, d = x2.shape
    return pl.pallas_call(
        _rmsnorm_kernel,
        grid=(n // NORM_TM,),
        in_specs=[pl.BlockSpec((NORM_TM, d), lambda i: (i, 0)),
                  pl.BlockSpec((1, d), lambda i: (0, 0))],
        out_specs=pl.BlockSpec((NORM_TM, d), lambda i: (i, 0)),
        out_shape=jax.ShapeDtypeStruct((n, d), out_dtype),
        compiler_params=_cparams(("arbitrary",)),
        name="rmsnorm",
    )(x2, g.reshape(1, d))


def _row_perm_matrix(dil):
    n = PERM_BLOCK // dil
    out = jnp.arange(PERM_BLOCK)
    src = dil * (out % n) + out // n
    return (src[:, None] == jnp.arange(PERM_BLOCK)[None, :]).astype(BF16)


def _mm_kernel(*refs, n_x, has_res, perm_dils, tiles_per_group, q_tiles):
    x_refs = refs[:n_x]
    w_refs = refs[n_x:2 * n_x]
    pos = 2 * n_x
    r_ref = refs[pos] if has_res else None
    pos += int(has_res)
    n_perm = sum(d > 1 for d in perm_dils)
    p_refs = refs[pos:pos + n_perm]
    pos += n_perm
    o_ref, wbf_ref = refs[pos], refs[pos + 1]

    @pl.when(pl.program_id(1) == 0)
    def _():
        factor = None
        if q_tiles is not None:
            period, lo, hi = q_tiles
            phase = lax.rem(pl.program_id(0), period)
            factor = jnp.where((phase >= lo) & (phase < hi), jnp.float32(HEAD_DIM ** -0.5), jnp.float32(1.0))
        off = 0
        for w_ref in w_refs:
            kk = w_ref.shape[0]
            w = w_ref[...] if factor is None else w_ref[...] * factor
            wbf_ref[off:off + kk, :] = w.astype(BF16)
            off += kk

    acc = None
    off = 0
    for x_ref in x_refs:
        kk = x_ref.shape[1]
        part = jnp.dot(x_ref[...], wbf_ref[off:off + kk, :], preferred_element_type=F32)
        acc = part if acc is None else acc + part
        off += kk
    if has_res:
        acc = acc + r_ref[...]
    out = acc.astype(o_ref.dtype)
    if not p_refs:
        o_ref[...] = out
        return

    group = pl.program_id(0) // tiles_per_group
    p_iter = iter(p_refs)
    for g, dil in enumerate(perm_dils):
        p_ref = next(p_iter) if dil > 1 else None

        @pl.when(group == g)
        def _(p_ref=p_ref):
            if p_ref is None:
                o_ref[...] = out
            else:
                for blk in range(out.shape[0] // PERM_BLOCK):
                    rows = slice(blk * PERM_BLOCK, (blk + 1) * PERM_BLOCK)
                    o_ref[rows, :] = jnp.dot(p_ref[...], out[rows, :],
                                             preferred_element_type=F32).astype(o_ref.dtype)


def matmul(xs, w, out_dtype, residual=None, perm_dils=(), q_cols=None, name="matmul"):
    m = xs[0].shape[0]
    ks = [x.shape[1] for x in xs]
    ktot, nc = w.shape
    assert sum(ks) == ktot and all(k == ks[0] for k in ks)
    tm, tn = min(MM_TM, m), min(MM_TN, nc)
    assert m % tm == 0 and nc % tn == 0
    in_specs = [pl.BlockSpec((tm, k), lambda j, i: (i, 0)) for k in ks]
    in_specs += [pl.BlockSpec((k, tn), functools.partial(lambda j, i, p: (p, j), p=p))
                 for p, k in enumerate(ks)]
    args = list(xs) + [w] * len(xs)
    if residual is not None:
        in_specs.append(pl.BlockSpec((tm, tn), lambda j, i: (i, j)))
        args.append(residual)
    tiles_per_group = 0
    if perm_dils:
        assert out_dtype == BF16 and tm % PERM_BLOCK == 0 and (nc // tn) % len(perm_dils) == 0
        tiles_per_group = (nc // tn) // len(perm_dils)
        for dil in perm_dils:
            if dil > 1:
                in_specs.append(pl.BlockSpec((PERM_BLOCK, PERM_BLOCK), lambda j, i: (0, 0)))
                args.append(_row_perm_matrix(dil))
    q_tiles = None
    if q_cols is not None:
        assert all(c % tn == 0 for c in q_cols)
        q_tiles = tuple(c // tn for c in q_cols)
    return pl.pallas_call(
        functools.partial(_mm_kernel, n_x=len(xs), has_res=residual is not None,
                          perm_dils=tuple(perm_dils), tiles_per_group=tiles_per_group, q_tiles=q_tiles),
        grid=(nc // tn, m // tm),
        in_specs=in_specs,
        out_specs=pl.BlockSpec((tm, tn), lambda j, i: (i, j)),
        out_shape=jax.ShapeDtypeStruct((m, nc), out_dtype),
        scratch_shapes=[pltpu.VMEM((ktot, tn), BF16)],
        compiler_params=_cparams(("arbitrary", "arbitrary")),
        name=name,
    )(*args)


def _conv_kernel(ac_ref, gc_ref, ap_ref, gp_ref, an_ref, gn_ref, w_ref, b_ref, lg_ref, lb_ref,
                 o_ref, buf_ref, cv_ref, *, tt, ch):
    i = pl.program_id(1)
    last = pl.num_programs(1) - 1
    halo = BF16_SUBLANES

    def glu(a_ref, g_ref):
        return a_ref[0].astype(F32) * jax.nn.sigmoid(g_ref[0].astype(F32))

    buf_ref[0, 0:halo, :] = jnp.where(i > 0, glu(ap_ref, gp_ref), 0.0)
    buf_ref[0, halo:halo + tt, :] = glu(ac_ref, gc_ref)
    buf_ref[0, halo + tt:2 * halo + tt, :] = jnp.where(i < last, glu(an_ref, gn_ref), 0.0)
    span = tt + 2 * halo - SUBLANES
    for s in range(1, SUBLANES):
        buf_ref[s, 0:span, :] = buf_ref[0, s:s + span, :]

    base = halo - CONV_PAD
    for cb in range(ch // CONV_CB):
        cs = slice(cb * CONV_CB, (cb + 1) * CONV_CB)
        for rb in range(tt // CONV_RB):
            accs = [jnp.zeros((SUBLANES, CONV_CB), F32) for _ in range(CONV_RB // SUBLANES)]
            for j in range(CONV_WIDTH):
                shift = (base + j) % SUBLANES
                r0 = rb * CONV_RB + base + j - shift
                wj = w_ref[j, :, cs]
                for k in range(len(accs)):
                    rk = r0 + k * SUBLANES
                    accs[k] = accs[k] + wj * buf_ref[shift, rk:rk + SUBLANES, cs]
            for k, acc in enumerate(accs):
                rk = rb * CONV_RB + k * SUBLANES
                cv_ref[rk:rk + SUBLANES, cs] = acc + b_ref[:, cs]

    a = cv_ref[...]
    mu = jnp.mean(a, axis=-1, keepdims=True)
    ctr = a - mu
    var = jnp.mean(ctr * ctr, axis=-1, keepdims=True)
    y = ctr * lax.rsqrt(var + LN_EPS) * lg_ref[...] + lb_ref[...]
    o_ref[0] = (y * jax.nn.sigmoid(y)).astype(o_ref.dtype)


def conformer_conv(proj3, w_dw, b_dw, ln_g, ln_b):
    b, t, _ = proj3.shape
    ch = w_dw.shape[1]
    tt, halo = CONV_TT, BF16_SUBLANES
    hb = tt // halo
    n_hb = t // halo
    cur = lambda col: pl.BlockSpec((1, tt, ch), lambda bi, i: (bi, i, col))
    prv = lambda col: pl.BlockSpec((1, halo, ch), lambda bi, i: (bi, jnp.maximum(i * hb - 1, 0), col))
    nxt = lambda col: pl.BlockSpec((1, halo, ch), lambda bi, i: (bi, jnp.minimum((i + 1) * hb, n_hb - 1), col))
    vec = lambda rows: pl.BlockSpec((rows, ch), lambda bi, i: (0, 0))
    return pl.pallas_call(
        functools.partial(_conv_kernel, tt=tt, ch=ch),
        grid=(b, t // tt),
        in_specs=[cur(0), cur(1), prv(0), prv(1), nxt(0), nxt(1),
                  pl.BlockSpec((CONV_WIDTH, SUBLANES, ch), lambda bi, i: (0, 0, 0)), vec(1), vec(1), vec(1)],
        out_specs=pl.BlockSpec((1, tt, ch), lambda bi, i: (bi, i, 0)),
        out_shape=jax.ShapeDtypeStruct((b, t, ch), BF16),
        scratch_shapes=[pltpu.VMEM((SUBLANES, tt + 2 * halo, ch), F32), pltpu.VMEM((tt, ch), F32)],
        compiler_params=_cparams(("arbitrary", "arbitrary")),
        name="conformer_conv",
    )(proj3, proj3, proj3, proj3, proj3, proj3,
      jnp.broadcast_to(w_dw[:, None, :], (CONV_WIDTH, SUBLANES, ch)), b_dw.reshape(1, ch), ln_g.reshape(1, ch),
      ln_b.reshape(1, ch))


def _nat_bias_table(rpb):
    h = rpb.shape[0]
    kc = NAT_WIN_COLS
    cols = jnp.arange(GRID_W)
    col_start = jnp.clip(cols - kc // 2, 0, GRID_W - kc)
    col_mask = (cols[None, :] >= col_start[:, None]) & (cols[None, :] < col_start[:, None] + kc)
    dc_idx = jnp.clip(cols[None, :] - cols[:, None], -(kc - 1), kc - 1) + (kc - 1)
    bc = jnp.where(col_mask[None, None], rpb[:, :, dc_idx].astype(F32), NEG)
    blank = jnp.full((h, 1, GRID_W, GRID_W), NEG, F32)
    bc = jnp.concatenate([blank, bc, blank], axis=1)
    return jnp.concatenate([bc[:, :-1], bc[:, 1:]], axis=-1)


def _nat_kernel(q_ref, k_ref, v_ref, tab_ref, o_ref, bias_ref, *, rows):
    blk = pl.program_id(2)
    r_base = blk * NAT_QROWS
    n_sub = NAT_QROWS // NAT_SUBROWS
    tq = NAT_SUBROWS * GRID_W
    ws = [jnp.clip(r_base + j * NAT_SUBROWS - NAT_WIN_ROWS // 2, 0, rows - NAT_KROWS) for j in range(n_sub)]
    lane = lax.broadcasted_iota(jnp.int32, (GRID_W, 2 * GRID_W), 1)

    @pl.when((blk <= 1) | (blk == pl.num_programs(2) - 1))
    def _():
        for j in range(n_sub):
            for qr in range(NAT_SUBROWS):
                r = r_base + j * NAT_SUBROWS + qr
                r0 = jnp.clip(r - NAT_WIN_ROWS // 2, 0, rows - NAT_WIN_ROWS)
                for p in range(NAT_KROWS // 2):
                    kr = ws[j] + 2 * p
                    entry = jnp.clip(kr - r + NAT_WIN_ROWS, 0, 2 * NAT_WIN_ROWS - 1)
                    ok0 = ((kr >= r0) & (kr < r0 + NAT_WIN_ROWS)).astype(jnp.int32)
                    ok1 = ((kr + 1 >= r0) & (kr + 1 < r0 + NAT_WIN_ROWS)).astype(jnp.int32)
                    ok = jnp.where(lane < GRID_W, ok0, ok1) > 0
                    bias_ref[j, qr * GRID_W:(qr + 1) * GRID_W, p * 2 * GRID_W:(p + 1) * 2 * GRID_W] = (
                        jnp.where(ok, tab_ref[0, entry], NEG))

    def window(ref, j):
        return ref[0, pl.ds(pl.multiple_of(ws[j] * GRID_W, 128), NAT_KROWS * GRID_W), :]

    scores = [lax.dot_general(q_ref[0, j * tq:(j + 1) * tq, :], window(k_ref, j), (((1,), (1,)), ((), ())),
                              preferred_element_type=F32) + bias_ref[j]
              for j in range(n_sub)]
    stats = []
    for s in scores:
        m = jnp.max(s, axis=-1, keepdims=True)
        p = jnp.exp(s - m)
        stats.append((jnp.sum(p, axis=-1, keepdims=True), p.astype(BF16)))
    for j, (den, p) in enumerate(stats):
        o = jnp.dot(p, window(v_ref, j), preferred_element_type=F32)
        o_ref[0, j * tq:(j + 1) * tq, :] = (o / den).astype(o_ref.dtype)


def neighbourhood_attention(proj3, rpb, col0):
    b, t, _ = proj3.shape
    nh = rpb.shape[0]
    rows = t // GRID_W
    assert rows >= NAT_KROWS and rows % NAT_QROWS == 0
    tq = NAT_QROWS * GRID_W
    tab = _nat_bias_table(rpb)
    return pl.pallas_call(
        functools.partial(_nat_kernel, rows=rows),
        grid=(nh, b, rows // NAT_QROWS),
        in_specs=[pl.BlockSpec((1, tq, HEAD_DIM), lambda h, bi, i: (bi, i, col0 + h)),
                  pl.BlockSpec((1, t, HEAD_DIM), lambda h, bi, i: (bi, 0, col0 + nh + h)),
                  pl.BlockSpec((1, t, HEAD_DIM), lambda h, bi, i: (bi, 0, col0 + 2 * nh + h)),
                  pl.BlockSpec((1, 2 * NAT_WIN_ROWS, GRID_W, 2 * GRID_W), lambda h, bi, i: (h, 0, 0, 0))],
        out_specs=pl.BlockSpec((1, tq, HEAD_DIM), lambda h, bi, i: (bi, i, h)),
        out_shape=jax.ShapeDtypeStruct((b, t, nh * HEAD_DIM), BF16),
        scratch_shapes=[pltpu.VMEM((NAT_QROWS // NAT_SUBROWS, NAT_SUBROWS * GRID_W, NAT_KROWS * GRID_W), F32)],
        compiler_params=_cparams(("arbitrary", "arbitrary", "arbitrary")),
        name="neighbourhood_attention",
    )(proj3, proj3, proj3, tab)


def _t5_bucket(rel):
    nb = T5_BUCKETS // 2
    max_exact = nb // 2
    n = jnp.abs(rel)
    sign = jnp.where(rel > 0, nb, 0)
    nf = jnp.maximum(n, 1).astype(F32)
    large = max_exact + (jnp.log(nf / max_exact) / math.log(T5_MAX_DIST / max_exact)
                         * (nb - max_exact)).astype(jnp.int32)
    large = jnp.minimum(large, nb - 1)
    return sign + jnp.where(n < max_exact, n, large)


def _dil_dims(g):
    win, dil = DIL_CONFIGS[g]
    assert win == 2 * dil * DIL_HALF
    mq = min(DIL_MQ, DIL_TQ // dil)
    wk = -(-(mq + 2 * DIL_HALF) // 128) * 128
    front = dil * DIL_HALF
    back = dil * (wk - mq - DIL_HALF)
    return dil, mq, wk, front, back


def _dil_bias_rows(t5_bias, g, nh):
    dil, mq, wk, _, _ = _dil_dims(g)
    delta = jnp.arange(wk) - DIL_HALF
    on = delta <= DIL_HALF
    vals = t5_bias[:, g * nh:(g + 1) * nh][_t5_bucket(delta * dil)].astype(F32)
    return jnp.where(on[None], vals.T, NEG)[:, None, :]


def _dil_kernel(q0_ref, q1_ref, q2_ref, u0_ref, u1_ref, u2_ref, proj_ref, o_ref,
                k0, v0, k1, v1, k2, v2, b0_ref, b1_ref, b2_ref, m_ref, l_ref, n_ref, sem, *, t, nh):
    h, bi, i = pl.program_id(0), pl.program_id(1), pl.program_id(2)
    kv = ((k0, v0), (k1, v1), (k2, v2))

    @pl.when((bi == 0) & (i == 0))
    def _():
        for u_ref, b_ref in ((u0_ref, b0_ref), (u1_ref, b1_ref), (u2_ref, b2_ref)):
            rows = jnp.broadcast_to(u_ref[0], b_ref.shape)
            b_ref[...] = pltpu.roll(rows, 0, 1, stride=1, stride_axis=0)

    def src(g, which):
        col = pl.multiple_of(((g * 3 + 1 + which) * nh + h) * HEAD_DIM, HEAD_DIM)
        return proj_ref.at[bi, :, pl.ds(col, HEAD_DIM)]

    @pl.when(i == 0)
    def _():
        cps = []
        for g in range(3):
            _, _, _, front, back = _dil_dims(g)
            for w in range(2):
                cps.append(pltpu.make_async_copy(src(g, w), kv[g][w].at[pl.ds(front, t), :], sem.at[g * 2 + w]))
                cps[-1].start()
                kv[g][w][0:front, :] = jnp.zeros((front, HEAD_DIM), BF16)
                kv[g][w][front + t:front + t + back, :] = jnp.zeros((back, HEAD_DIM), BF16)
        for cp in cps:
            cp.wait()

    t0 = pl.multiple_of(i * DIL_TQ, DIL_TQ)

    def class_rows(ref, first_block_row, n_blocks, dil, r):
        n = PERM_BLOCK // dil
        parts = []
        for blk in range(n_blocks):
            start = first_block_row + blk * PERM_BLOCK + r * n
            if not isinstance(start, int):
                start = pl.multiple_of(start, n)
            parts.append(ref[pl.ds(start, n), :])
        return jnp.concatenate(parts, axis=0)

    def group_tiles(g, q_ref, b_ref):
        dil, mq, wk, front, _ = _dil_dims(g)
        if dil == 1:
            tiles = [(q_ref[0, j * mq:(j + 1) * mq, :], t0 + j * mq, 0, slice(j * mq, (j + 1) * mq))
                     for j in range(DIL_TQ // mq)]
        else:
            assert dil * mq == DIL_TQ
            tiles = [(class_rows(q_ref.at[0], 0, DIL_TQ // PERM_BLOCK, dil, r), t0, r,
                      pl.ds(r, mq, stride=dil)) for r in range(dil)]

        def keys_values(which, row0, r):
            if dil == 1:
                return kv[g][which][pl.ds(pl.multiple_of(row0, 128), wk), :]
            return class_rows(kv[g][which], row0, wk * dil // PERM_BLOCK, dil, r)

        scores = []
        for q, row0, r, _ in tiles:
            s = lax.dot_general(q, keys_values(0, row0, r), (((1,), (1,)), ((), ())),
                                preferred_element_type=F32)
            kpos = (row0 - front + r) + dil * lax.broadcasted_iota(jnp.int32, (1, wk), 1)
            kmask = jnp.where((kpos >= 0) & (kpos < t), 0.0, NEG)
            scores.append(s + b_ref[...] + kmask)
        stats = []
        for s in scores:
            m = jnp.max(s, axis=-1, keepdims=True)
            p = jnp.exp(s - m)
            stats.append((m, jnp.sum(p, axis=-1, keepdims=True), p.astype(BF16)))
        for (_, row0, r, rows), (m, den, p) in zip(tiles, stats):
            n_ref[g, rows, :] = jnp.dot(p, keys_values(1, row0, r), preferred_element_type=F32)
            m_ref[g, rows, :] = jnp.broadcast_to(m, (mq, HEAD_DIM))
            l_ref[g, rows, :] = jnp.broadcast_to(den, (mq, HEAD_DIM))

    for g, q_ref, b_ref in ((0, q0_ref, b0_ref), (1, q1_ref, b1_ref), (2, q2_ref, b2_ref)):
        group_tiles(g, q_ref, b_ref)

    m_all = jnp.maximum(jnp.maximum(m_ref[0], m_ref[1]), m_ref[2])
    num = jnp.zeros((DIL_TQ, HEAD_DIM), F32)
    den = jnp.zeros((DIL_TQ, HEAD_DIM), F32)
    for g in range(3):
        w = jnp.exp(m_ref[g] - m_all)
        num = num + w * n_ref[g]
        den = den + w * l_ref[g]
    o_ref[0] = (num / den).astype(o_ref.dtype)


def dilated_attention(proj3, t5_bias, nh):
    b, t, _ = proj3.shape
    assert t % DIL_TQ == 0
    tabs = [_dil_bias_rows(t5_bias, g, nh) for g in range(3)]
    qspec = lambda g: pl.BlockSpec((1, DIL_TQ, HEAD_DIM), lambda h, bi, i: (bi, i, g * 3 * nh + h))
    bspec = lambda g: pl.BlockSpec((1, 1, _dil_dims(g)[2]), lambda h, bi, i: (h, 0, 0))
    scratch = []
    for g in range(3):
        _, _, _, front, back = _dil_dims(g)
        assert g == 0 or (front % PERM_BLOCK == 0 and back % PERM_BLOCK == 0)
        scratch += [pltpu.VMEM((front + t + back, HEAD_DIM), BF16)] * 2
    scratch += [pltpu.VMEM(_dil_dims(g)[1:3], F32) for g in range(3)]
    scratch += [pltpu.VMEM((3, DIL_TQ, HEAD_DIM), F32)] * 3
    scratch.append(pltpu.SemaphoreType.DMA((6,)))
    return pl.pallas_call(
        functools.partial(_dil_kernel, t=t, nh=nh),
        grid=(nh, b, t // DIL_TQ),
        in_specs=[qspec(0), qspec(1), qspec(2), bspec(0), bspec(1), bspec(2),
                  pl.BlockSpec(memory_space=pl.ANY)],
        out_specs=pl.BlockSpec((1, DIL_TQ, HEAD_DIM), lambda h, bi, i: (bi, i, h)),
        out_shape=jax.ShapeDtypeStruct((b, t, nh * HEAD_DIM), BF16),
        scratch_shapes=scratch,
        compiler_params=_cparams(("arbitrary", "arbitrary", "arbitrary")),
        name="dilated_attention",
    )(proj3, proj3, proj3, tabs[0], tabs[1], tabs[2], proj3)


ROUTER_LANES = 128


def _pack_pair(lo, hi):
    lo_bits = lax.bitcast_convert_type(lo.astype(BF16).astype(F32), jnp.uint32)
    hi_bits = lax.bitcast_convert_type(hi.astype(BF16).astype(F32), jnp.uint32)
    return lax.shift_right_logical(lo_bits, jnp.uint32(16)) | (hi_bits & jnp.uint32(0xFFFF0000))


def _unpack_pair(word):
    lo = lax.bitcast_convert_type(lax.shift_left(word, jnp.uint32(16)), F32)
    hi = lax.bitcast_convert_type(word & jnp.uint32(0xFFFF0000), F32)
    return lo, hi


def _router_kernel(x_ref, g_ref, wcat_ref, b_ref, o_ref, hp_ref, cnt_ref, carry_ref):
    step = pl.program_id(0)
    half = x_ref.shape[1] // 2
    x = x_ref[...]
    ms = jnp.mean(x * x, axis=-1, keepdims=True)
    h = x * lax.rsqrt(ms + RMS_EPS) * g_ref[...]
    hp_ref[...] = _pack_pair(h[:, :half], h[:, half:])
    h_hi = h.astype(BF16)
    h_lo = (h - h_hi.astype(F32)).astype(BF16)
    hi_both = jnp.dot(h_hi, wcat_ref[...], preferred_element_type=F32)
    logits = (hi_both[:, :ROUTER_LANES] + hi_both[:, ROUTER_LANES:]
              + jnp.dot(h_lo, wcat_ref[:, :ROUTER_LANES], preferred_element_type=F32)) + b_ref[...]

    lane = lax.broadcasted_iota(jnp.int32, logits.shape, 1)
    big = jnp.int32(ROUTER_LANES)

    def first_argmax(vals):
        top = jnp.max(vals, axis=-1, keepdims=True)
        return top, jnp.min(jnp.where(vals == top, lane, big), axis=-1, keepdims=True)

    gl = jnp.where(lane < N_GROUPS, logits, NEG)
    g_top, g_idx = first_argmax(gl)
    g_val = 1.0 / jnp.sum(jnp.exp(gl - g_top), axis=-1, keepdims=True)
    in_group = (lane >= N_GROUPS) & (lane < N_GROUPS + N_EXPERTS) & (
        lax.shift_right_arithmetic(lane - N_GROUPS, jnp.int32(3)) == g_idx)
    el = jnp.where(in_group, logits, NEG)
    v1, i1 = first_argmax(el)
    el2 = jnp.where(lane == i1, NEG, el)
    v2, i2 = first_argmax(el2)
    e21 = jnp.exp(v2 - v1)
    gate1 = g_val / (1.0 + e21)
    gate2 = g_val * e21 / (1.0 + e21)

    @pl.when(step == 0)
    def _():
        carry_ref[...] = jnp.zeros(carry_ref.shape, F32)

    e1, e2 = i1 - N_GROUPS, i2 - N_GROUPS
    oh1 = (lane == e1).astype(F32)
    oh2 = (lane == e2).astype(F32)
    both = oh1 + oh2
    tm = x.shape[0]
    tri = (lax.broadcasted_iota(jnp.int32, (tm, tm), 1) < lax.broadcasted_iota(jnp.int32, (tm, tm), 0))
    before = jnp.dot(tri.astype(BF16), both.astype(BF16), preferred_element_type=F32) + carry_ref[...]
    rank1 = jnp.sum(oh1 * before, axis=-1, keepdims=True)
    rank2 = jnp.sum(oh2 * before, axis=-1, keepdims=True)
    carry_ref[...] = carry_ref[...] + jnp.sum(both, axis=0, keepdims=True)
    cnt_ref[...] = jnp.broadcast_to(carry_ref[...], cnt_ref.shape)

    out = jnp.where(lane == 0, gate1, 0.0)
    out = jnp.where(lane == 1, gate2, out)
    out = jnp.where(lane == 2, e1.astype(F32), out)
    out = jnp.where(lane == 3, e2.astype(F32), out)
    out = jnp.where(lane == 4, rank1, out)
    out = jnp.where(lane == 5, rank2, out)
    o_ref[...] = out


def router(x2, gain, w_group, b_group, w_router, b_router):
    n, d = x2.shape
    wcat = jnp.concatenate([w_group, w_router], axis=1).astype(F32)
    wcat = jnp.pad(wcat, ((0, 0), (0, ROUTER_LANES - wcat.shape[1])))
    w_hi = wcat.astype(BF16)
    w_lo = (wcat - w_hi.astype(F32)).astype(BF16)
    bias = jnp.concatenate([b_group.reshape(-1), b_router.reshape(-1)]).astype(F32)
    bias = jnp.pad(bias, (0, ROUTER_LANES - bias.shape[0])).reshape(1, ROUTER_LANES)
    tm = ROUTER_TM
    return pl.pallas_call(
        _router_kernel,
        grid=(n // tm,),
        in_specs=[pl.BlockSpec((tm, d), lambda i: (i, 0)),
                  pl.BlockSpec((1, d), lambda i: (0, 0)),
                  pl.BlockSpec((d, 2 * ROUTER_LANES), lambda i: (0, 0)),
                  pl.BlockSpec((1, ROUTER_LANES), lambda i: (0, 0))],
        out_specs=[pl.BlockSpec((tm, ROUTER_LANES), lambda i: (i, 0)),
                   pl.BlockSpec((tm, d // 2), lambda i: (i, 0)),
                   pl.BlockSpec((8, ROUTER_LANES), lambda i: (0, 0))],
        out_shape=[jax.ShapeDtypeStruct((n, ROUTER_LANES), F32),
                   jax.ShapeDtypeStruct((n, d // 2), jnp.uint32),
                   jax.ShapeDtypeStruct((8, ROUTER_LANES), F32)],
        scratch_shapes=[pltpu.VMEM((1, ROUTER_LANES), F32)],
        compiler_params=_cparams(("arbitrary",)),
        name="moe_router",
    )(x2, gain.reshape(1, d), jnp.concatenate([w_hi, w_lo], axis=1), bias)


def dispatch(route, counts, n):
    n_blocks = n * TOP_K // MOE_TM + N_EXPERTS
    counts = counts[0, :N_EXPERTS].astype(jnp.int32)
    pcounts = (counts + MOE_TM - 1) // MOE_TM * MOE_TM
    pends = jnp.cumsum(pcounts)
    pstarts = pends - pcounts
    expert = route[:, 2:2 + TOP_K].astype(jnp.int32)
    rank = route[:, 2 + TOP_K:2 + 2 * TOP_K].astype(jnp.int32)
    ids = jnp.arange(N_EXPERTS, dtype=jnp.int32)
    base = jnp.sum(jnp.where(expert[..., None] == ids, pstarts, 0), axis=-1)
    slot_of = (base + rank).reshape(-1)
    blk_start = jnp.arange(n_blocks, dtype=jnp.int32) * MOE_TM
    block_e = jnp.minimum(jnp.sum(blk_start[:, None] >= pends[None, :], axis=-1), N_EXPERTS - 1)
    block_e = block_e.astype(jnp.int32)
    n_on = (pends[-1] // MOE_TM).astype(jnp.int32)
    pad_from = pstarts + counts
    blk_ids = jnp.arange(n_blocks, dtype=jnp.int32)
    first = ((blk_ids == 0) | (block_e != jnp.roll(block_e, 1))) & (blk_ids < n_on)
    later = jnp.where((ids[None, :] > ids[:, None]) & (counts[None, :] > 0), ids[None, :], N_EXPERTS)
    next_nonempty = jnp.min(later, axis=-1)
    next_e = jnp.sum(jnp.where(block_e[:, None] == ids, next_nonempty, 0), axis=-1)
    next_e = jnp.where(next_e < N_EXPERTS, next_e, -1).astype(jnp.int32)
    return slot_of, block_e, n_on.reshape(1), first.astype(jnp.int32), next_e, pad_from, pends


def _scatter_kernel(slot_ref, padfrom_ref, pend_ref, hp_ref, xs_hbm, zero_ref, sem, zsem):
    i = pl.program_id(0)
    tm = SCAT_TM
    n_blocks = xs_hbm.shape[0] // MOE_TM

    def pad_rows(fn):
        for e in range(N_EXPERTS):
            def body(r, carry):
                fn(pltpu.make_async_copy(zero_ref.at[pl.ds(0, 1), :], xs_hbm.at[pl.ds(r, 1), :], zsem))
                return carry
            lax.fori_loop(padfrom_ref[e], pend_ref[e], body, 0)

        def blk(b, carry):
            start = pl.multiple_of(b * MOE_TM, MOE_TM)
            fn(pltpu.make_async_copy(zero_ref, xs_hbm.at[pl.ds(start, MOE_TM), :], zsem))
            return carry
        lax.fori_loop(pend_ref[N_EXPERTS - 1] // MOE_TM, n_blocks, blk, 0)

    @pl.when(i == 0)
    def _():
        zero_ref[...] = jnp.zeros(zero_ref.shape, zero_ref.dtype)
        pad_rows(lambda cp: cp.start())

    def body(s, carry):
        for k in range(TOP_K):
            row = slot_ref[(i * tm + s) * TOP_K + k]
            pltpu.make_async_copy(hp_ref.at[pl.ds(s, 1), :], xs_hbm.at[pl.ds(row, 1), :], sem).start()
        return carry
    lax.fori_loop(0, tm, body, 0, unroll=8)
    for k in range(TOP_K):
        pltpu.make_async_copy(hp_ref, xs_hbm.at[pl.ds(0, tm), :], sem).wait()

    @pl.when(i == pl.num_programs(0) - 1)
    def _():
        pad_rows(lambda cp: cp.wait())


def scatter_rows(hp, slot_of, pad_from, pends, n_rows):
    n, w = hp.shape
    grid_spec = pltpu.PrefetchScalarGridSpec(
        num_scalar_prefetch=3,
        grid=(n // SCAT_TM,),
        in_specs=[pl.BlockSpec((SCAT_TM, w), lambda i, slots, pf, pe: (i, 0))],
        out_specs=pl.BlockSpec(memory_space=pl.ANY),
        scratch_shapes=[pltpu.VMEM((MOE_TM, w), hp.dtype),
                        pltpu.SemaphoreType.DMA(()), pltpu.SemaphoreType.DMA(())],
    )
    return pl.pallas_call(
        _scatter_kernel,
        grid_spec=grid_spec,
        out_shape=jax.ShapeDtypeStruct((n_rows, w), hp.dtype),
        compiler_params=_cparams(("arbitrary",)),
        name="moe_scatter",
    )(slot_of, pad_from, pends, hp)


def _expert_kernel(be_ref, non_ref, first_ref, next_ref, xs_ref, wg_hbm, wu_hbm, wd_hbm, o_ref,
                   stage_g, stage_u, stage_d, wg_ref, wu_ref, wd_ref, sem, *, layer):
    b = pl.program_id(0)
    on = b < non_ref[0]
    staged = ((wg_hbm, stage_g), (wu_hbm, stage_u), (wd_hbm, stage_d))

    def weight_copies(e):
        return [pltpu.make_async_copy(w_hbm.at[layer, e], stage, sem.at[k])
                for k, (w_hbm, stage) in enumerate(staged)]

    @pl.when(b == 0)
    def _():
        for cp in weight_copies(be_ref[0]):
            cp.start()

    @pl.when(first_ref[b] > 0)
    def _():
        for cp in weight_copies(be_ref[b]):
            cp.wait()
        for (_, stage), w_ref in zip(staged, (wg_ref, wu_ref, wd_ref)):
            w_ref[...] = stage[...].astype(BF16)

        @pl.when(next_ref[b] >= 0)
        def _():
            for cp in weight_copies(next_ref[b]):
                cp.start()

    @pl.when(jnp.logical_not(on))
    def _():
        o_ref[...] = jnp.zeros(o_ref.shape, o_ref.dtype)

    @pl.when(on)
    def _():
        lo, hi = _unpack_pair(xs_ref[...])
        lo, hi = lo.astype(BF16), hi.astype(BF16)
        half = lo.shape[1]

        def proj(w_ref):
            return (jnp.dot(lo, w_ref[:half, :], preferred_element_type=F32)
                    + jnp.dot(hi, w_ref[half:, :], preferred_element_type=F32))

        g = proj(wg_ref)
        u = proj(wu_ref)
        hdn = (g * jax.nn.sigmoid(g) * u).astype(BF16)
        y = jnp.dot(hdn, wd_ref[...], preferred_element_type=F32)
        o_ref[...] = _pack_pair(y[:, :half], y[:, half:])


def expert_blocks(xs, block_e, n_on, first, next_e, layer, w_gate, w_up, w_down):
    n_rows, half = xs.shape
    _, _, d, hid = w_gate.shape
    tm = MOE_TM
    row = lambda b, *_: (b, 0)
    grid_spec = pltpu.PrefetchScalarGridSpec(
        num_scalar_prefetch=4,
        grid=(n_rows // tm,),
        in_specs=[pl.BlockSpec((tm, half), row)] + [pl.BlockSpec(memory_space=pl.ANY)] * 3,
        out_specs=pl.BlockSpec((tm, half), row),
        scratch_shapes=[pltpu.VMEM((d, hid), F32), pltpu.VMEM((d, hid), F32), pltpu.VMEM((hid, d), F32),
                        pltpu.VMEM((d, hid), BF16), pltpu.VMEM((d, hid), BF16), pltpu.VMEM((hid, d), BF16),
                        pltpu.SemaphoreType.DMA((3,))],
    )
    return pl.pallas_call(
        functools.partial(_expert_kernel, layer=layer),
        grid_spec=grid_spec,
        out_shape=jax.ShapeDtypeStruct((n_rows, half), xs.dtype),
        compiler_params=_cparams(("arbitrary",)),
        name="moe_experts",
    )(block_e, n_on, first, next_e, xs, w_gate, w_up, w_down)


def _combine_kernel(slot_ref, x_ref, r_ref, gain_ref, y_hbm, *rest, emit_x):
    if emit_x:
        xo_ref, ho_ref, ybuf, sem = rest
    else:
        ho_ref, ybuf, sem = rest
        xo_ref = None
    i = pl.program_id(0)
    ni = pl.num_programs(0)
    slot = i % 2
    tm = COMB_TM

    def gather(blk, slot_):
        def body(s, carry):
            for k in range(TOP_K):
                row = slot_ref[(blk * tm + s) * TOP_K + k]
                pltpu.make_async_copy(y_hbm.at[pl.ds(row, 1), :], ybuf.at[slot_, k, pl.ds(s, 1), :],
                                      sem.at[slot_]).start()
            return carry
        lax.fori_loop(0, tm, body, 0, unroll=8)

    @pl.when(i == 0)
    def _():
        gather(0, 0)

    @pl.when(i + 1 < ni)
    def _():
        gather(i + 1, 1 - slot)

    for k in range(TOP_K):
        pltpu.make_async_copy(y_hbm.at[pl.ds(0, tm), :], ybuf.at[slot, k], sem.at[slot]).wait()

    half = x_ref.shape[1] // 2
    g0, g1 = r_ref[:, 0:1], r_ref[:, 1:2]
    y0_lo, y0_hi = _unpack_pair(ybuf[slot, 0])
    y1_lo, y1_hi = _unpack_pair(ybuf[slot, 1])
    x_lo = x_ref[:, :half] + (g0 * y0_lo + g1 * y1_lo)
    x_hi = x_ref[:, half:] + (g0 * y0_hi + g1 * y1_hi)
    if emit_x:
        xo_ref[:, :half] = x_lo
        xo_ref[:, half:] = x_hi
    ssq = jnp.sum(x_lo * x_lo, axis=-1, keepdims=True) + jnp.sum(x_hi * x_hi, axis=-1, keepdims=True)
    inv = lax.rsqrt(ssq / x_ref.shape[1] + RMS_EPS)
    ho_ref[:, :half] = (x_lo * inv * gain_ref[:, :half]).astype(ho_ref.dtype)
    ho_ref[:, half:] = (x_hi * inv * gain_ref[:, half:]).astype(ho_ref.dtype)


def combine(x2, route, y_rows, slot_of, gain, emit_x, h_dtype):
    n, d = x2.shape
    tm = COMB_TM
    row = lambda i, slots: (i, 0)
    out_shape = [jax.ShapeDtypeStruct((n, d), h_dtype)]
    out_specs = [pl.BlockSpec((tm, d), row)]
    if emit_x:
        out_shape.insert(0, jax.ShapeDtypeStruct((n, d), F32))
        out_specs.insert(0, pl.BlockSpec((tm, d), row))
    grid_spec = pltpu.PrefetchScalarGridSpec(
        num_scalar_prefetch=1,
        grid=(n // tm,),
        in_specs=[pl.BlockSpec((tm, d), row),
                  pl.BlockSpec((tm, ROUTER_LANES), row),
                  pl.BlockSpec((1, d), lambda i, slots: (0, 0)),
                  pl.BlockSpec(memory_space=pl.ANY)],
        out_specs=out_specs,
        scratch_shapes=[pltpu.VMEM((2, TOP_K, tm, d // 2), y_rows.dtype), pltpu.SemaphoreType.DMA((2,))],
    )
    return pl.pallas_call(
        functools.partial(_combine_kernel, emit_x=emit_x),
        grid_spec=grid_spec,
        out_shape=out_shape,
        compiler_params=_cparams(("arbitrary",)),
        name="moe_combine",
    )(slot_of, x2, route, gain.reshape(1, d), y_rows)


def moe_layer(x2, layer, ffn_gain, next_gain, emit_x, h_dtype, w_group, b_group, w_router, b_router,
              w_gate, w_up, w_down):
    n = x2.shape[0]
    route, hp, counts = router(x2, ffn_gain, w_group, b_group, w_router, b_router)
    slot_of, block_e, n_on, first, next_e, pad_from, pends = dispatch(route, counts, n)
    xs = scatter_rows(hp, slot_of, pad_from, pends, block_e.shape[0] * MOE_TM)
    y_rows = expert_blocks(xs, block_e, n_on, first, next_e, layer, w_gate, w_up, w_down)
    return combine(x2, route, y_rows, slot_of, next_gain, emit_x, h_dtype)


def kernel(x, rms_mix, rms_ffn, rms_final, w_in_ab, conv_dw_w, conv_dw_b, conv_ln_g, conv_ln_b, nat_rpb,
           w_out_ab, w_in_c, w_out_c, t5_bias, moe_w_group, moe_b_group, moe_w_router, moe_b_router,
           moe_w_gate, moe_w_up, moe_w_down):
    b, t, d = x.shape
    n = b * t
    x2 = x.reshape(n, d)
    conv_ch = conv_dw_w.shape[-1]
    nat_heads = nat_rpb.shape[1]
    dil_heads = w_out_c.shape[1] // HEAD_DIM

    h = rmsnorm(x2, rms_mix[0], BF16)
    nat_w = nat_heads * HEAD_DIM
    proj = matmul([h], w_in_ab[0], BF16, q_cols=(w_in_ab.shape[-1], 2 * conv_ch, 2 * conv_ch + nat_w),
                  name="in_proj_ab").reshape(b, t, -1)
    a_out = conformer_conv(proj, conv_dw_w[0], conv_dw_b[0], conv_ln_g[0], conv_ln_b[0])
    b_out = neighbourhood_attention(proj, nat_rpb[0], 2 * conv_ch // HEAD_DIM)
    x2 = matmul([a_out.reshape(n, -1), b_out.reshape(n, -1)], w_out_ab[0], F32, residual=x2,
                name="out_proj_ab")
    wg, wu, wd = moe_w_gate, moe_w_up, moe_w_down
    x2, h = moe_layer(x2, 0, rms_ffn[0], rms_mix[1], True, BF16, moe_w_group[0], moe_b_group[0],
                      moe_w_router[0], moe_b_router[0], wg, wu, wd)

    dil_w = dil_heads * HEAD_DIM
    proj = matmul([h], w_in_c[0], BF16, perm_dils=tuple(dil for _, dil in DIL_CONFIGS),
                  q_cols=(3 * dil_w, 0, dil_w), name="in_proj_c").reshape(b, t, -1)
    o = dilated_attention(proj, t5_bias, dil_heads)
    x2 = matmul([o.reshape(n, -1)], w_out_c[0], F32, residual=x2, name="out_proj_c")
    (y,) = moe_layer(x2, 1, rms_ffn[1], rms_final, False, x.dtype, moe_w_group[1], moe_b_group[1],
                     moe_w_router[1], moe_b_router[1], wg, wu, wd)
    return y.reshape(b, t, d)
```

```python
import functools
import math

import jax
import jax.numpy as jnp
from jax import lax
from jax.experimental import pallas as pl
from jax.experimental.pallas import tpu as pltpu

F32 = jnp.float32
BF16 = jnp.bfloat16

HEAD_DIM = 128
CONV_WIDTH = 31
CONV_PAD = CONV_WIDTH // 2
NAT_WIN_ROWS = 8
NAT_WIN_COLS = 16
GRID_W = 64
DIL_CONFIGS = ((128, 1), (512, 4), (2048, 16))
T5_BUCKETS = 32
T5_MAX_DIST = 1024
N_GROUPS = 4
EXPERTS_PER_GROUP = 8
N_EXPERTS = N_GROUPS * EXPERTS_PER_GROUP
TOP_K = 2
RMS_EPS = 1e-6
LN_EPS = 1e-5
NEG = -1e30

VMEM_LIMIT = 56 * 1024 * 1024
VMEM_LIMIT_STAGED = 60 * 1024 * 1024
SUBLANES = 8
BF16_SUBLANES = 16

MM_TM = 1024
MM_TN = 512
MM_STAGED_TN = 1024
NORM_TM = 256
CONV_TT = 128
CONV_RB = 32
CONV_CB = 512
NAT_QROWS = 8
NAT_SUBROWS = 4
NAT_KROWS = 12
DIL_TQ = 1024
DIL_MQ = 256
DIL_HALF = 64
PERM_BLOCK = 256
MOE_TM = 256
SCAT_TM = 512
COMB_TM = 128
ROUTER_TM = 256


def _cparams(sem, vmem_limit=VMEM_LIMIT):
    return pltpu.CompilerParams(dimension_semantics=sem, vmem_limit_bytes=vmem_limit)


def _rmsnorm_kernel(x_ref, g_ref, o_ref):
    x = x_ref[...]
    ms = jnp.mean(x * x, axis=-1, keepdims=True)
    o_ref[...] = (x * lax.rsqrt(ms + RMS_EPS) * g_ref[...]).astype(o_ref.dtype)


def rmsnorm(x2, g, out_dtype):
    n, d = x2.shape
    return pl.pallas_call(
        _rmsnorm_kernel,
        grid=(n // NORM_TM,),
        in_specs=[pl.BlockSpec((NORM_TM, d), lambda i: (i, 0)),
                  pl.BlockSpec((1, d), lambda i: (0, 0))],
        out_specs=pl.BlockSpec((NORM_TM, d), lambda i: (i, 0)),
        out_shape=jax.ShapeDtypeStruct((n, d), out_dtype),
        compiler_params=_cparams(("arbitrary",)),
        name="rmsnorm",
    )(x2, g.reshape(1, d))


def _row_perm_matrix(dil):
    n = PERM_BLOCK // dil
    out = jnp.arange(PERM_BLOCK)
    src = dil * (out % n) + out // n
    return (src[:, None] == jnp.arange(PERM_BLOCK)[None, :]).astype(BF16)


def _q_factor(q_tiles):
    if q_tiles is None:
        return None
    period, lo, hi = q_tiles
    phase = lax.rem(pl.program_id(0), period)
    return jnp.where((phase >= lo) & (phase < hi), jnp.float32(HEAD_DIM ** -0.5), jnp.float32(1.0))


def _mm_store(out, o_ref, p_refs, perm_dils, tiles_per_group):
    if not p_refs:
        o_ref[...] = out
        return

    group = pl.program_id(0) // tiles_per_group
    p_iter = iter(p_refs)
    for g, dil in enumerate(perm_dils):
        p_ref = next(p_iter) if dil > 1 else None

        @pl.when(group == g)
        def _(p_ref=p_ref):
            if p_ref is None:
                o_ref[...] = out
            else:
                for blk in range(out.shape[0] // PERM_BLOCK):
                    rows = slice(blk * PERM_BLOCK, (blk + 1) * PERM_BLOCK)
                    o_ref[rows, :] = jnp.dot(p_ref[...], out[rows, :],
                                             preferred_element_type=F32).astype(o_ref.dtype)


def _mm_kernel(*refs, n_x, has_res, perm_dils, tiles_per_group, q_tiles):
    x_refs = refs[:n_x]
    w_refs = refs[n_x:2 * n_x]
    pos = 2 * n_x
    r_ref = refs[pos] if has_res else None
    pos += int(has_res)
    n_perm = sum(d > 1 for d in perm_dils)
    p_refs = refs[pos:pos + n_perm]
    pos += n_perm
    o_ref, wbf_ref = refs[pos], refs[pos + 1]

    @pl.when(pl.program_id(1) == 0)
    def _():
        factor = _q_factor(q_tiles)
        off = 0
        for w_ref in w_refs:
            kk = w_ref.shape[0]
            w = w_ref[...] if factor is None else w_ref[...] * factor
            wbf_ref[off:off + kk, :] = w.astype(BF16)
            off += kk

    acc = None
    off = 0
    for x_ref in x_refs:
        kk = x_ref.shape[1]
        part = jnp.dot(x_ref[...], wbf_ref[off:off + kk, :], preferred_element_type=F32)
        acc = part if acc is None else acc + part
        off += kk
    if has_res:
        acc = acc + r_ref[...]
    _mm_store(acc.astype(o_ref.dtype), o_ref, p_refs, perm_dils, tiles_per_group)


def _mm_staged_kernel(x_ref, w_hbm, *refs, perm_dils, tiles_per_group, q_tiles):
    n_perm = sum(d > 1 for d in perm_dils)
    p_refs = refs[:n_perm]
    o_ref, stage_ref, wbf_ref, sem = refs[n_perm:]
    j, i = pl.program_id(0), pl.program_id(1)
    tn = wbf_ref.shape[1]

    def fetch(jj):
        return pltpu.make_async_copy(w_hbm.at[:, pl.ds(pl.multiple_of(jj * tn, tn), tn)], stage_ref, sem)

    @pl.when((j == 0) & (i == 0))
    def _():
        fetch(0).start()

    @pl.when(i == 0)
    def _():
        fetch(j).wait()
        factor = _q_factor(q_tiles)
        w = stage_ref[...] if factor is None else stage_ref[...] * factor
        wbf_ref[...] = w.astype(BF16)

        @pl.when(j + 1 < pl.num_programs(0))
        def _():
            fetch(j + 1).start()

    acc = jnp.dot(x_ref[...], wbf_ref[...], preferred_element_type=F32)
    _mm_store(acc.astype(o_ref.dtype), o_ref, p_refs, perm_dils, tiles_per_group)


def matmul(xs, w, out_dtype, residual=None, perm_dils=(), q_cols=None, name="matmul"):
    m = xs[0].shape[0]
    ks = [x.shape[1] for x in xs]
    ktot, nc = w.shape
    assert sum(ks) == ktot and all(k == ks[0] for k in ks)
    staged = len(xs) == 1 and residual is None
    tm, tn = min(MM_TM, m), min(MM_STAGED_TN if staged else MM_TN, nc)
    assert m % tm == 0 and nc % tn == 0
    in_specs = [pl.BlockSpec((tm, k), lambda j, i: (i, 0)) for k in ks]
    if staged:
        in_specs.append(pl.BlockSpec(memory_space=pl.ANY))
        args = [xs[0], w]
    else:
        in_specs += [pl.BlockSpec((k, tn), functools.partial(lambda j, i, p: (p, j), p=p))
                     for p, k in enumerate(ks)]
        args = list(xs) + [w] * len(xs)
    if residual is not None:
        in_specs.append(pl.BlockSpec((tm, tn), lambda j, i: (i, j)))
        args.append(residual)
    tiles_per_group = 0
    if perm_dils:
        assert out_dtype == BF16 and tm % PERM_BLOCK == 0 and (nc // tn) % len(perm_dils) == 0
        tiles_per_group = (nc // tn) // len(perm_dils)
        for dil in perm_dils:
            if dil > 1:
                in_specs.append(pl.BlockSpec((PERM_BLOCK, PERM_BLOCK), lambda j, i: (0, 0)))
                args.append(_row_perm_matrix(dil))
    q_tiles = None
    if q_cols is not None:
        assert all(c % tn == 0 for c in q_cols)
        q_tiles = tuple(c // tn for c in q_cols)
    common = dict(perm_dils=tuple(perm_dils), tiles_per_group=tiles_per_group, q_tiles=q_tiles)
    if staged:
        body = functools.partial(_mm_staged_kernel, **common)
        scratch = [pltpu.VMEM((ktot, tn), F32), pltpu.VMEM((ktot, tn), BF16), pltpu.SemaphoreType.DMA(())]
    else:
        body = functools.partial(_mm_kernel, n_x=len(xs), has_res=residual is not None, **common)
        scratch = [pltpu.VMEM((ktot, tn), BF16)]
    return pl.pallas_call(
        body,
        grid=(nc // tn, m // tm),
        in_specs=in_specs,
        out_specs=pl.BlockSpec((tm, tn), lambda j, i: (i, j)),
        out_shape=jax.ShapeDtypeStruct((m, nc), out_dtype),
        scratch_shapes=scratch,
        compiler_params=_cparams(("arbitrary", "arbitrary"), VMEM_LIMIT_STAGED if staged else VMEM_LIMIT),
        name=name,
    )(*args)


def _conv_kernel(ac_ref, gc_ref, ap_ref, gp_ref, an_ref, gn_ref, w_ref, b_ref, lg_ref, lb_ref,
                 o_ref, buf_ref, cv_ref, *, tt, ch):
    i = pl.program_id(1)
    last = pl.num_programs(1) - 1
    halo = BF16_SUBLANES

    def glu(a_ref, g_ref):
        return a_ref[0].astype(F32) * jax.nn.sigmoid(g_ref[0].astype(F32))

    buf_ref[0, 0:halo, :] = jnp.where(i > 0, glu(ap_ref, gp_ref), 0.0)
    buf_ref[0, halo:halo + tt, :] = glu(ac_ref, gc_ref)
    buf_ref[0, halo + tt:2 * halo + tt, :] = jnp.where(i < last, glu(an_ref, gn_ref), 0.0)
    span = tt + 2 * halo - SUBLANES
    for s in range(1, SUBLANES):
        buf_ref[s, 0:span, :] = buf_ref[0, s:s + span, :]

    base = halo - CONV_PAD
    for cb in range(ch // CONV_CB):
        cs = slice(cb * CONV_CB, (cb + 1) * CONV_CB)
        for rb in range(tt // CONV_RB):
            accs = [jnp.zeros((SUBLANES, CONV_CB), F32) for _ in range(CONV_RB // SUBLANES)]
            for j in range(CONV_WIDTH):
                shift = (base + j) % SUBLANES
                r0 = rb * CONV_RB + base + j - shift
                wj = w_ref[j, :, cs]
                for k in range(len(accs)):
                    rk = r0 + k * SUBLANES
                    accs[k] = accs[k] + wj * buf_ref[shift, rk:rk + SUBLANES, cs]
            for k, acc in enumerate(accs):
                rk = rb * CONV_RB + k * SUBLANES
                cv_ref[rk:rk + SUBLANES, cs] = acc + b_ref[:, cs]

    a = cv_ref[...]
    mu = jnp.mean(a, axis=-1, keepdims=True)
    ctr = a - mu
    var = jnp.mean(ctr * ctr, axis=-1, keepdims=True)
    y = ctr * lax.rsqrt(var + LN_EPS) * lg_ref[...] + lb_ref[...]
    o_ref[0] = (y * jax.nn.sigmoid(y)).astype(o_ref.dtype)


def conformer_conv(proj3, w_dw, b_dw, ln_g, ln_b):
    b, t, _ = proj3.shape
    ch = w_dw.shape[1]
    tt, halo = CONV_TT, BF16_SUBLANES
    hb = tt // halo
    n_hb = t // halo
    cur = lambda col: pl.BlockSpec((1, tt, ch), lambda bi, i: (bi, i, col))
    prv = lambda col: pl.BlockSpec((1, halo, ch), lambda bi, i: (bi, jnp.maximum(i * hb - 1, 0), col))
    nxt = lambda col: pl.BlockSpec((1, halo, ch), lambda bi, i: (bi, jnp.minimum((i + 1) * hb, n_hb - 1), col))
    vec = lambda rows: pl.BlockSpec((rows, ch), lambda bi, i: (0, 0))
    return pl.pallas_call(
        functools.partial(_conv_kernel, tt=tt, ch=ch),
        grid=(b, t // tt),
        in_specs=[cur(0), cur(1), prv(0), prv(1), nxt(0), nxt(1),
                  pl.BlockSpec((CONV_WIDTH, SUBLANES, ch), lambda bi, i: (0, 0, 0)), vec(1), vec(1), vec(1)],
        out_specs=pl.BlockSpec((1, tt, ch), lambda bi, i: (bi, i, 0)),
        out_shape=jax.ShapeDtypeStruct((b, t, ch), BF16),
        scratch_shapes=[pltpu.VMEM((SUBLANES, tt + 2 * halo, ch), F32), pltpu.VMEM((tt, ch), F32)],
        compiler_params=_cparams(("arbitrary", "arbitrary")),
        name="conformer_conv",
    )(proj3, proj3, proj3, proj3, proj3, proj3,
      jnp.broadcast_to(w_dw[:, None, :], (CONV_WIDTH, SUBLANES, ch)), b_dw.reshape(1, ch), ln_g.reshape(1, ch),
      ln_b.reshape(1, ch))


def _nat_bias_table(rpb):
    h = rpb.shape[0]
    kc = NAT_WIN_COLS
    cols = jnp.arange(GRID_W)
    col_start = jnp.clip(cols - kc // 2, 0, GRID_W - kc)
    col_mask = (cols[None, :] >= col_start[:, None]) & (cols[None, :] < col_start[:, None] + kc)
    dc_idx = jnp.clip(cols[None, :] - cols[:, None], -(kc - 1), kc - 1) + (kc - 1)
    bc = jnp.where(col_mask[None, None], rpb[:, :, dc_idx].astype(F32), NEG)
    blank = jnp.full((h, 1, GRID_W, GRID_W), NEG, F32)
    bc = jnp.concatenate([blank, bc, blank], axis=1)
    return jnp.concatenate([bc[:, :-1], bc[:, 1:]], axis=-1)


def _nat_kernel(q_ref, k_ref, v_ref, tab_ref, o_ref, bias_ref, *, rows):
    blk = pl.program_id(2)
    r_base = blk * NAT_QROWS
    n_sub = NAT_QROWS // NAT_SUBROWS
    tq = NAT_SUBROWS * GRID_W
    ws = [jnp.clip(r_base + j * NAT_SUBROWS - NAT_WIN_ROWS // 2, 0, rows - NAT_KROWS) for j in range(n_sub)]
    lane = lax.broadcasted_iota(jnp.int32, (GRID_W, 2 * GRID_W), 1)

    @pl.when((blk <= 1) | (blk == pl.num_programs(2) - 1))
    def _():
        for j in range(n_sub):
            for qr in range(NAT_SUBROWS):
                r = r_base + j * NAT_SUBROWS + qr
                r0 = jnp.clip(r - NAT_WIN_ROWS // 2, 0, rows - NAT_WIN_ROWS)
                for p in range(NAT_KROWS // 2):
                    kr = ws[j] + 2 * p
                    entry = jnp.clip(kr - r + NAT_WIN_ROWS, 0, 2 * NAT_WIN_ROWS - 1)
                    ok0 = ((kr >= r0) & (kr < r0 + NAT_WIN_ROWS)).astype(jnp.int32)
                    ok1 = ((kr + 1 >= r0) & (kr + 1 < r0 + NAT_WIN_ROWS)).astype(jnp.int32)
                    ok = jnp.where(lane < GRID_W, ok0, ok1) > 0
                    bias_ref[j, qr * GRID_W:(qr + 1) * GRID_W, p * 2 * GRID_W:(p + 1) * 2 * GRID_W] = (
                        jnp.where(ok, tab_ref[0, entry], NEG))

    def window(ref, j):
        return ref[0, pl.ds(pl.multiple_of(ws[j] * GRID_W, 128), NAT_KROWS * GRID_W), :]

    scores = [lax.dot_general(q_ref[0, j * tq:(j + 1) * tq, :], window(k_ref, j), (((1,), (1,)), ((), ())),
                              preferred_element_type=F32) + bias_ref[j]
              for j in range(n_sub)]
    stats = []
    for s in scores:
        m = jnp.max(s, axis=-1, keepdims=True)
        p = jnp.exp(s - m)
        stats.append((jnp.sum(p, axis=-1, keepdims=True), p.astype(BF16)))
    for j, (den, p) in enumerate(stats):
        o = jnp.dot(p, window(v_ref, j), preferred_element_type=F32)
        o_ref[0, j * tq:(j + 1) * tq, :] = (o / den).astype(o_ref.dtype)


def neighbourhood_attention(proj3, rpb, col0):
    b, t, _ = proj3.shape
    nh = rpb.shape[0]
    rows = t // GRID_W
    assert rows >= NAT_KROWS and rows % NAT_QROWS == 0
    tq = NAT_QROWS * GRID_W
    tab = _nat_bias_table(rpb)
    return pl.pallas_call(
        functools.partial(_nat_kernel, rows=rows),
        grid=(nh, b, rows // NAT_QROWS),
        in_specs=[pl.BlockSpec((1, tq, HEAD_DIM), lambda h, bi, i: (bi, i, col0 + h)),
                  pl.BlockSpec((1, t, HEAD_DIM), lambda h, bi, i: (bi, 0, col0 + nh + h)),
                  pl.BlockSpec((1, t, HEAD_DIM), lambda h, bi, i: (bi, 0, col0 + 2 * nh + h)),
                  pl.BlockSpec((1, 2 * NAT_WIN_ROWS, GRID_W, 2 * GRID_W), lambda h, bi, i: (h, 0, 0, 0))],
        out_specs=pl.BlockSpec((1, tq, HEAD_DIM), lambda h, bi, i: (bi, i, h)),
        out_shape=jax.ShapeDtypeStruct((b, t, nh * HEAD_DIM), BF16),
        scratch_shapes=[pltpu.VMEM((NAT_QROWS // NAT_SUBROWS, NAT_SUBROWS * GRID_W, NAT_KROWS * GRID_W), F32)],
        compiler_params=_cparams(("arbitrary", "arbitrary", "arbitrary")),
        name="neighbourhood_attention",
    )(proj3, proj3, proj3, tab)


def _t5_bucket(rel):
    nb = T5_BUCKETS // 2
    max_exact = nb // 2
    n = jnp.abs(rel)
    sign = jnp.where(rel > 0, nb, 0)
    nf = jnp.maximum(n, 1).astype(F32)
    large = max_exact + (jnp.log(nf / max_exact) / math.log(T5_MAX_DIST / max_exact)
                         * (nb - max_exact)).astype(jnp.int32)
    large = jnp.minimum(large, nb - 1)
    return sign + jnp.where(n < max_exact, n, large)


def _dil_dims(g):
    win, dil = DIL_CONFIGS[g]
    assert win == 2 * dil * DIL_HALF
    mq = min(DIL_MQ, DIL_TQ // dil)
    wk = -(-(mq + 2 * DIL_HALF) // 128) * 128
    front = dil * DIL_HALF
    back = dil * (wk - mq - DIL_HALF)
    return dil, mq, wk, front, back


def _dil_bias_rows(t5_bias, g, nh):
    dil, mq, wk, _, _ = _dil_dims(g)
    delta = jnp.arange(wk) - DIL_HALF
    on = delta <= DIL_HALF
    vals = t5_bias[:, g * nh:(g + 1) * nh][_t5_bucket(delta * dil)].astype(F32)
    return jnp.where(on[None], vals.T, NEG)[:, None, :]


def _dil_kernel(q0_ref, q1_ref, q2_ref, u0_ref, u1_ref, u2_ref, proj_ref, o_ref,
                k0, v0, k1, v1, k2, v2, b0_ref, b1_ref, b2_ref, m_ref, l_ref, n_ref, sem, *, t, nh):
    h, bi, i = pl.program_id(0), pl.program_id(1), pl.program_id(2)
    kv = ((k0, v0), (k1, v1), (k2, v2))

    @pl.when((bi == 0) & (i == 0))
    def _():
        for u_ref, b_ref in ((u0_ref, b0_ref), (u1_ref, b1_ref), (u2_ref, b2_ref)):
            rows = jnp.broadcast_to(u_ref[0], b_ref.shape)
            b_ref[...] = pltpu.roll(rows, 0, 1, stride=1, stride_axis=0)

    def src(g, which):
        col = pl.multiple_of(((g * 3 + 1 + which) * nh + h) * HEAD_DIM, HEAD_DIM)
        return proj_ref.at[bi, :, pl.ds(col, HEAD_DIM)]

    @pl.when(i == 0)
    def _():
        cps = []
        for g in range(3):
            _, _, _, front, back = _dil_dims(g)
            for w in range(2):
                cps.append(pltpu.make_async_copy(src(g, w), kv[g][w].at[pl.ds(front, t), :], sem.at[g * 2 + w]))
                cps[-1].start()
                kv[g][w][0:front, :] = jnp.zeros((front, HEAD_DIM), BF16)
                kv[g][w][front + t:front + t + back, :] = jnp.zeros((back, HEAD_DIM), BF16)
        for cp in cps:
            cp.wait()

    t0 = pl.multiple_of(i * DIL_TQ, DIL_TQ)

    def class_rows(ref, first_block_row, n_blocks, dil, r):
        n = PERM_BLOCK // dil
        parts = []
        for blk in range(n_blocks):
            start = first_block_row + blk * PERM_BLOCK + r * n
            if not isinstance(start, int):
                start = pl.multiple_of(start, n)
            parts.append(ref[pl.ds(start, n), :])
        return jnp.concatenate(parts, axis=0)

    def group_tiles(g, q_ref, b_ref):
        dil, mq, wk, front, _ = _dil_dims(g)
        if dil == 1:
            tiles = [(q_ref[0, j * mq:(j + 1) * mq, :], t0 + j * mq, 0, slice(j * mq, (j + 1) * mq))
                     for j in range(DIL_TQ // mq)]
        else:
            assert dil * mq == DIL_TQ
            tiles = [(class_rows(q_ref.at[0], 0, DIL_TQ // PERM_BLOCK, dil, r), t0, r,
                      pl.ds(r, mq, stride=dil)) for r in range(dil)]

        def keys_values(which, row0, r):
            if dil == 1:
                return kv[g][which][pl.ds(pl.multiple_of(row0, 128), wk), :]
            return class_rows(kv[g][which], row0, wk * dil // PERM_BLOCK, dil, r)

        scores = []
        for q, row0, r, _ in tiles:
            s = lax.dot_general(q, keys_values(0, row0, r), (((1,), (1,)), ((), ())),
                                preferred_element_type=F32)
            kpos = (row0 - front + r) + dil * lax.broadcasted_iota(jnp.int32, (1, wk), 1)
            kmask = jnp.where((kpos >= 0) & (kpos < t), 0.0, NEG)
            scores.append(s + b_ref[...] + kmask)
        stats = []
        for s in scores:
            m = jnp.max(s, axis=-1, keepdims=True)
            p = jnp.exp(s - m)
            stats.append((m, jnp.sum(p, axis=-1, keepdims=True), p.astype(BF16)))
        for (_, row0, r, rows), (m, den, p) in zip(tiles, stats):
            n_ref[g, rows, :] = jnp.dot(p, keys_values(1, row0, r), preferred_element_type=F32)
            m_ref[g, rows, :] = jnp.broadcast_to(m, (mq, HEAD_DIM))
            l_ref[g, rows, :] = jnp.broadcast_to(den, (mq, HEAD_DIM))

    for g, q_ref, b_ref in ((0, q0_ref, b0_ref), (1, q1_ref, b1_ref), (2, q2_ref, b2_ref)):
        group_tiles(g, q_ref, b_ref)

    m_all = jnp.maximum(jnp.maximum(m_ref[0], m_ref[1]), m_ref[2])
    num = jnp.zeros((DIL_TQ, HEAD_DIM), F32)
    den = jnp.zeros((DIL_TQ, HEAD_DIM), F32)
    for g in range(3):
        w = jnp.exp(m_ref[g] - m_all)
        num = num + w * n_ref[g]
        den = den + w * l_ref[g]
    o_ref[0] = (num / den).astype(o_ref.dtype)


def dilated_attention(proj3, t5_bias, nh):
    b, t, _ = proj3.shape
    assert t % DIL_TQ == 0
    tabs = [_dil_bias_rows(t5_bias, g, nh) for g in range(3)]
    qspec = lambda g: pl.BlockSpec((1, DIL_TQ, HEAD_DIM), lambda h, bi, i: (bi, i, g * 3 * nh + h))
    bspec = lambda g: pl.BlockSpec((1, 1, _dil_dims(g)[2]), lambda h, bi, i: (h, 0, 0))
    scratch = []
    for g in range(3):
        _, _, _, front, back = _dil_dims(g)
        assert g == 0 or (front % PERM_BLOCK == 0 and back % PERM_BLOCK == 0)
        scratch += [pltpu.VMEM((front + t + back, HEAD_DIM), BF16)] * 2
    scratch += [pltpu.VMEM(_dil_dims(g)[1:3], F32) for g in range(3)]
    scratch += [pltpu.VMEM((3, DIL_TQ, HEAD_DIM), F32)] * 3
    scratch.append(pltpu.SemaphoreType.DMA((6,)))
    return pl.pallas_call(
        functools.partial(_dil_kernel, t=t, nh=nh),
        grid=(nh, b, t // DIL_TQ),
        in_specs=[qspec(0), qspec(1), qspec(2), bspec(0), bspec(1), bspec(2),
                  pl.BlockSpec(memory_space=pl.ANY)],
        out_specs=pl.BlockSpec((1, DIL_TQ, HEAD_DIM), lambda h, bi, i: (bi, i, h)),
        out_shape=jax.ShapeDtypeStruct((b, t, nh * HEAD_DIM), BF16),
        scratch_shapes=scratch,
        compiler_params=_cparams(("arbitrary", "arbitrary", "arbitrary")),
        name="dilated_attention",
    )(proj3, proj3, proj3, tabs[0], tabs[1], tabs[2], proj3)


ROUTER_LANES = 128


def _pack_rounded(lo, hi):
    lo_bits = lax.bitcast_convert_type(lo, jnp.uint32)
    hi_bits = lax.bitcast_convert_type(hi, jnp.uint32)
    return lax.shift_right_logical(lo_bits, jnp.uint32(16)) | (hi_bits & jnp.uint32(0xFFFF0000))


def _pack_pair(lo, hi):
    return _pack_rounded(lo.astype(BF16).astype(F32), hi.astype(BF16).astype(F32))


def _unpack_pair(word):
    lo = lax.bitcast_convert_type(lax.shift_left(word, jnp.uint32(16)), F32)
    hi = lax.bitcast_convert_type(word & jnp.uint32(0xFFFF0000), F32)
    return lo, hi


def _router_kernel(x_ref, g_ref, wcat_ref, b_ref, o_ref, hp_ref, cnt_ref, carry_ref):
    step = pl.program_id(0)
    half = x_ref.shape[1] // 2
    x = x_ref[...]
    ms = jnp.mean(x * x, axis=-1, keepdims=True)
    h = x * lax.rsqrt(ms + RMS_EPS) * g_ref[...]
    h_hi = h.astype(BF16)
    h_hi32 = h_hi.astype(F32)
    h_lo = (h - h_hi32).astype(BF16)
    hp_ref[...] = _pack_rounded(h_hi32[:, :half], h_hi32[:, half:])
    hi_both = jnp.dot(h_hi, wcat_ref[...], preferred_element_type=F32)
    logits = (hi_both[:, :ROUTER_LANES] + hi_both[:, ROUTER_LANES:]
              + jnp.dot(h_lo, wcat_ref[:, :ROUTER_LANES], preferred_element_type=F32)) + b_ref[...]

    lane = lax.broadcasted_iota(jnp.int32, logits.shape, 1)
    big = jnp.int32(ROUTER_LANES)

    def first_argmax(vals):
        top = jnp.max(vals, axis=-1, keepdims=True)
        return top, jnp.min(jnp.where(vals == top, lane, big), axis=-1, keepdims=True)

    gl = jnp.where(lane < N_GROUPS, logits, NEG)
    g_top, g_idx = first_argmax(gl)
    g_val = 1.0 / jnp.sum(jnp.exp(gl - g_top), axis=-1, keepdims=True)
    in_group = (lane >= N_GROUPS) & (lane < N_GROUPS + N_EXPERTS) & (
        lax.shift_right_arithmetic(lane - N_GROUPS, jnp.int32(3)) == g_idx)
    el = jnp.where(in_group, logits, NEG)
    v1, i1 = first_argmax(el)
    el2 = jnp.where(lane == i1, NEG, el)
    v2, i2 = first_argmax(el2)
    e21 = jnp.exp(v2 - v1)
    gate1 = g_val / (1.0 + e21)
    gate2 = g_val * e21 / (1.0 + e21)

    @pl.when(step == 0)
    def _():
        carry_ref[...] = jnp.zeros(carry_ref.shape, F32)

    e1, e2 = i1 - N_GROUPS, i2 - N_GROUPS
    oh1 = (lane == e1).astype(F32)
    oh2 = (lane == e2).astype(F32)
    both = oh1 + oh2
    tm = x.shape[0]
    tri = (lax.broadcasted_iota(jnp.int32, (tm, tm), 1) < lax.broadcasted_iota(jnp.int32, (tm, tm), 0))
    before = jnp.dot(tri.astype(BF16), both.astype(BF16), preferred_element_type=F32) + carry_ref[...]
    rank1 = jnp.sum(oh1 * before, axis=-1, keepdims=True)
    rank2 = jnp.sum(oh2 * before, axis=-1, keepdims=True)
    carry_ref[...] = carry_ref[...] + jnp.sum(both, axis=0, keepdims=True)
    cnt_ref[...] = jnp.broadcast_to(carry_ref[...], cnt_ref.shape)

    out = jnp.where(lane == 0, gate1, 0.0)
    out = jnp.where(lane == 1, gate2, out)
    out = jnp.where(lane == 2, e1.astype(F32), out)
    out = jnp.where(lane == 3, e2.astype(F32), out)
    out = jnp.where(lane == 4, rank1, out)
    out = jnp.where(lane == 5, rank2, out)
    o_ref[...] = out


def router(x2, gain, w_group, b_group, w_router, b_router):
    n, d = x2.shape
    wcat = jnp.concatenate([w_group, w_router], axis=1).astype(F32)
    wcat = jnp.pad(wcat, ((0, 0), (0, ROUTER_LANES - wcat.shape[1])))
    w_hi = wcat.astype(BF16)
    w_lo = (wcat - w_hi.astype(F32)).astype(BF16)
    bias = jnp.concatenate([b_group.reshape(-1), b_router.reshape(-1)]).astype(F32)
    bias = jnp.pad(bias, (0, ROUTER_LANES - bias.shape[0])).reshape(1, ROUTER_LANES)
    tm = ROUTER_TM
    return pl.pallas_call(
        _router_kernel,
        grid=(n // tm,),
        in_specs=[pl.BlockSpec((tm, d), lambda i: (i, 0)),
                  pl.BlockSpec((1, d), lambda i: (0, 0)),
                  pl.BlockSpec((d, 2 * ROUTER_LANES), lambda i: (0, 0)),
                  pl.BlockSpec((1, ROUTER_LANES), lambda i: (0, 0))],
        out_specs=[pl.BlockSpec((tm, ROUTER_LANES), lambda i: (i, 0)),
                   pl.BlockSpec((tm, d // 2), lambda i: (i, 0)),
                   pl.BlockSpec((8, ROUTER_LANES), lambda i: (0, 0))],
        out_shape=[jax.ShapeDtypeStruct((n, ROUTER_LANES), F32),
                   jax.ShapeDtypeStruct((n, d // 2), jnp.uint32),
                   jax.ShapeDtypeStruct((8, ROUTER_LANES), F32)],
        scratch_shapes=[pltpu.VMEM((1, ROUTER_LANES), F32)],
        compiler_params=_cparams(("arbitrary",)),
        name="moe_router",
    )(x2, gain.reshape(1, d), jnp.concatenate([w_hi, w_lo], axis=1), bias)


def dispatch(route, counts, n):
    n_blocks = n * TOP_K // MOE_TM + N_EXPERTS
    counts = counts[0, :N_EXPERTS].astype(jnp.int32)
    pcounts = (counts + MOE_TM - 1) // MOE_TM * MOE_TM
    pends = jnp.cumsum(pcounts)
    pstarts = pends - pcounts
    expert = route[:, 2:2 + TOP_K].astype(jnp.int32)
    rank = route[:, 2 + TOP_K:2 + 2 * TOP_K].astype(jnp.int32)
    ids = jnp.arange(N_EXPERTS, dtype=jnp.int32)
    base = jnp.sum(jnp.where(expert[..., None] == ids, pstarts, 0), axis=-1)
    slot_of = (base + rank).reshape(-1)
    blk_start = jnp.arange(n_blocks, dtype=jnp.int32) * MOE_TM
    block_e = jnp.minimum(jnp.sum(blk_start[:, None] >= pends[None, :], axis=-1), N_EXPERTS - 1)
    block_e = block_e.astype(jnp.int32)
    n_on = (pends[-1] // MOE_TM).astype(jnp.int32)
    pad_from = pstarts + counts
    blk_ids = jnp.arange(n_blocks, dtype=jnp.int32)
    first = ((blk_ids == 0) | (block_e != jnp.roll(block_e, 1))) & (blk_ids < n_on)
    later = jnp.where((ids[None, :] > ids[:, None]) & (counts[None, :] > 0), ids[None, :], N_EXPERTS)
    next_nonempty = jnp.min(later, axis=-1)
    next_e = jnp.sum(jnp.where(block_e[:, None] == ids, next_nonempty, 0), axis=-1)
    next_e = jnp.where(next_e < N_EXPERTS, next_e, -1).astype(jnp.int32)
    return slot_of, block_e, n_on.reshape(1), first.astype(jnp.int32), next_e, pad_from, pends


def _scatter_kernel(slot_ref, padfrom_ref, pend_ref, hp_ref, xs_hbm, zero_ref, sem, zsem):
    i = pl.program_id(0)
    tm = SCAT_TM
    n_blocks = xs_hbm.shape[0] // MOE_TM

    def pad_rows(fn):
        for e in range(N_EXPERTS):
            def body(r, carry):
                fn(pltpu.make_async_copy(zero_ref.at[pl.ds(0, 1), :], xs_hbm.at[pl.ds(r, 1), :], zsem))
                return carry
            lax.fori_loop(padfrom_ref[e], pend_ref[e], body, 0)

        def blk(b, carry):
            start = pl.multiple_of(b * MOE_TM, MOE_TM)
            fn(pltpu.make_async_copy(zero_ref, xs_hbm.at[pl.ds(start, MOE_TM), :], zsem))
            return carry
        lax.fori_loop(pend_ref[N_EXPERTS - 1] // MOE_TM, n_blocks, blk, 0)

    @pl.when(i == 0)
    def _():
        zero_ref[...] = jnp.zeros(zero_ref.shape, zero_ref.dtype)
        pad_rows(lambda cp: cp.start())

    def body(s, carry):
        for k in range(TOP_K):
            row = slot_ref[(i * tm + s) * TOP_K + k]
            pltpu.make_async_copy(hp_ref.at[pl.ds(s, 1), :], xs_hbm.at[pl.ds(row, 1), :], sem).start(priority=k)
        return carry
    lax.fori_loop(0, tm, body, 0, unroll=8)
    for k in range(TOP_K):
        pltpu.make_async_copy(hp_ref, xs_hbm.at[pl.ds(0, tm), :], sem).wait()

    @pl.when(i == pl.num_programs(0) - 1)
    def _():
        pad_rows(lambda cp: cp.wait())


def scatter_rows(hp, slot_of, pad_from, pends, n_rows):
    n, w = hp.shape
    grid_spec = pltpu.PrefetchScalarGridSpec(
        num_scalar_prefetch=3,
        grid=(n // SCAT_TM,),
        in_specs=[pl.BlockSpec((SCAT_TM, w), lambda i, slots, pf, pe: (i, 0))],
        out_specs=pl.BlockSpec(memory_space=pl.ANY),
        scratch_shapes=[pltpu.VMEM((MOE_TM, w), hp.dtype),
                        pltpu.SemaphoreType.DMA(()), pltpu.SemaphoreType.DMA(())],
    )
    return pl.pallas_call(
        _scatter_kernel,
        grid_spec=grid_spec,
        out_shape=jax.ShapeDtypeStruct((n_rows, w), hp.dtype),
        compiler_params=_cparams(("arbitrary",)),
        name="moe_scatter",
    )(slot_of, pad_from, pends, hp)


def _expert_kernel(be_ref, non_ref, first_ref, next_ref, xs_ref, wg_hbm, wu_hbm, wd_hbm, o_ref,
                   stage_g, stage_u, stage_d, wg_ref, wu_ref, wd_ref, sem, *, layer):
    b = pl.program_id(0)
    on = b < non_ref[0]
    staged = ((wg_hbm, stage_g), (wu_hbm, stage_u), (wd_hbm, stage_d))

    def weight_copies(e):
        return [pltpu.make_async_copy(w_hbm.at[layer, e], stage, sem.at[k])
                for k, (w_hbm, stage) in enumerate(staged)]

    @pl.when(b == 0)
    def _():
        for cp in weight_copies(be_ref[0]):
            cp.start()

    @pl.when(first_ref[b] > 0)
    def _():
        for cp in weight_copies(be_ref[b]):
            cp.wait()
        for (_, stage), w_ref in zip(staged, (wg_ref, wu_ref, wd_ref)):
            w_ref[...] = stage[...].astype(BF16)

        @pl.when(next_ref[b] >= 0)
        def _():
            for cp in weight_copies(next_ref[b]):
                cp.start()

    @pl.when(jnp.logical_not(on))
    def _():
        o_ref[...] = jnp.zeros(o_ref.shape, o_ref.dtype)

    @pl.when(on)
    def _():
        lo, hi = _unpack_pair(xs_ref[...])
        lo, hi = lo.astype(BF16), hi.astype(BF16)
        half = lo.shape[1]

        def proj(w_ref):
            return (jnp.dot(lo, w_ref[:half, :], preferred_element_type=F32)
                    + jnp.dot(hi, w_ref[half:, :], preferred_element_type=F32))

        g = proj(wg_ref)
        u = proj(wu_ref)
        hdn = (g * jax.nn.sigmoid(g) * u).astype(BF16)
        y = jnp.dot(hdn, wd_ref[...], preferred_element_type=F32)
        o_ref[...] = _pack_pair(y[:, :half], y[:, half:])


def expert_blocks(xs, block_e, n_on, first, next_e, layer, w_gate, w_up, w_down):
    n_rows, half = xs.shape
    _, _, d, hid = w_gate.shape
    tm = MOE_TM
    row = lambda b, *_: (b, 0)
    grid_spec = pltpu.PrefetchScalarGridSpec(
        num_scalar_prefetch=4,
        grid=(n_rows // tm,),
        in_specs=[pl.BlockSpec((tm, half), row)] + [pl.BlockSpec(memory_space=pl.ANY)] * 3,
        out_specs=pl.BlockSpec((tm, half), row),
        scratch_shapes=[pltpu.VMEM((d, hid), F32), pltpu.VMEM((d, hid), F32), pltpu.VMEM((hid, d), F32),
                        pltpu.VMEM((d, hid), BF16), pltpu.VMEM((d, hid), BF16), pltpu.VMEM((hid, d), BF16),
                        pltpu.SemaphoreType.DMA((3,))],
    )
    return pl.pallas_call(
        functools.partial(_expert_kernel, layer=layer),
        grid_spec=grid_spec,
        out_shape=jax.ShapeDtypeStruct((n_rows, half), xs.dtype),
        compiler_params=_cparams(("arbitrary",)),
        name="moe_experts",
    )(block_e, n_on, first, next_e, xs, w_gate, w_up, w_down)


def _combine_kernel(slot_ref, x_ref, r_ref, gain_ref, y_hbm, *rest, emit_x):
    if emit_x:
        xo_ref, ho_ref, ybuf, sem = rest
    else:
        ho_ref, ybuf, sem = rest
        xo_ref = None
    i = pl.program_id(0)
    ni = pl.num_programs(0)
    slot = i % 2
    tm = COMB_TM

    def gather(blk, slot_):
        def body(s, carry):
            for k in range(TOP_K):
                row = slot_ref[(blk * tm + s) * TOP_K + k]
                pltpu.make_async_copy(y_hbm.at[pl.ds(row, 1), :], ybuf.at[slot_, k, pl.ds(s, 1), :],
                                      sem.at[slot_]).start()
            return carry
        lax.fori_loop(0, tm, body, 0, unroll=8)

    @pl.when(i == 0)
    def _():
        gather(0, 0)

    @pl.when(i + 1 < ni)
    def _():
        gather(i + 1, 1 - slot)

    for k in range(TOP_K):
        pltpu.make_async_copy(y_hbm.at[pl.ds(0, tm), :], ybuf.at[slot, k], sem.at[slot]).wait()

    half = x_ref.shape[1] // 2
    g0, g1 = r_ref[:, 0:1], r_ref[:, 1:2]
    y0_lo, y0_hi = _unpack_pair(ybuf[slot, 0])
    y1_lo, y1_hi = _unpack_pair(ybuf[slot, 1])
    x_lo = x_ref[:, :half] + (g0 * y0_lo + g1 * y1_lo)
    x_hi = x_ref[:, half:] + (g0 * y0_hi + g1 * y1_hi)
    if emit_x:
        xo_ref[:, :half] = x_lo
        xo_ref[:, half:] = x_hi
    ssq = jnp.sum(x_lo * x_lo, axis=-1, keepdims=True) + jnp.sum(x_hi * x_hi, axis=-1, keepdims=True)
    inv = lax.rsqrt(ssq / x_ref.shape[1] + RMS_EPS)
    ho_ref[:, :half] = (x_lo * inv * gain_ref[:, :half]).astype(ho_ref.dtype)
    ho_ref[:, half:] = (x_hi * inv * gain_ref[:, half:]).astype(ho_ref.dtype)


def combine(x2, route, y_rows, slot_of, gain, emit_x, h_dtype):
    n, d = x2.shape
    tm = COMB_TM
    row = lambda i, slots: (i, 0)
    out_shape = [jax.ShapeDtypeStruct((n, d), h_dtype)]
    out_specs = [pl.BlockSpec((tm, d), row)]
    if emit_x:
        out_shape.insert(0, jax.ShapeDtypeStruct((n, d), F32))
        out_specs.insert(0, pl.BlockSpec((tm, d), row))
    grid_spec = pltpu.PrefetchScalarGridSpec(
        num_scalar_prefetch=1,
        grid=(n // tm,),
        in_specs=[pl.BlockSpec((tm, d), row),
                  pl.BlockSpec((tm, ROUTER_LANES), row),
                  pl.BlockSpec((1, d), lambda i, slots: (0, 0)),
                  pl.BlockSpec(memory_space=pl.ANY)],
        out_specs=out_specs,
        scratch_shapes=[pltpu.VMEM((2, TOP_K, tm, d // 2), y_rows.dtype), pltpu.SemaphoreType.DMA((2,))],
    )
    return pl.pallas_call(
        functools.partial(_combine_kernel, emit_x=emit_x),
        grid_spec=grid_spec,
        out_shape=out_shape,
        compiler_params=_cparams(("arbitrary",)),
        name="moe_combine",
    )(slot_of, x2, route, gain.reshape(1, d), y_rows)


def moe_layer(x2, layer, ffn_gain, next_gain, emit_x, h_dtype, w_group, b_group, w_router, b_router,
              w_gate, w_up, w_down):
    n = x2.shape[0]
    route, hp, counts = router(x2, ffn_gain, w_group, b_group, w_router, b_router)
    slot_of, block_e, n_on, first, next_e, pad_from, pends = dispatch(route, counts, n)
    xs = scatter_rows(hp, slot_of, pad_from, pends, block_e.shape[0] * MOE_TM)
    y_rows = expert_blocks(xs, block_e, n_on, first, next_e, layer, w_gate, w_up, w_down)
    return combine(x2, route, y_rows, slot_of, next_gain, emit_x, h_dtype)


def kernel(x, rms_mix, rms_ffn, rms_final, w_in_ab, conv_dw_w, conv_dw_b, conv_ln_g, conv_ln_b, nat_rpb,
           w_out_ab, w_in_c, w_out_c, t5_bias, moe_w_group, moe_b_group, moe_w_router, moe_b_router,
           moe_w_gate, moe_w_up, moe_w_down):
    b, t, d = x.shape
    n = b * t
    x2 = x.reshape(n, d)
    conv_ch = conv_dw_w.shape[-1]
    nat_heads = nat_rpb.shape[1]
    dil_heads = w_out_c.shape[1] // HEAD_DIM

    h = rmsnorm(x2, rms_mix[0], BF16)
    nat_w = nat_heads * HEAD_DIM
    proj = matmul([h], w_in_ab[0], BF16, q_cols=(w_in_ab.shape[-1], 2 * conv_ch, 2 * conv_ch + nat_w),
                  name="in_proj_ab").reshape(b, t, -1)
    a_out = conformer_conv(proj, conv_dw_w[0], conv_dw_b[0], conv_ln_g[0], conv_ln_b[0])
    b_out = neighbourhood_attention(proj, nat_rpb[0], 2 * conv_ch // HEAD_DIM)
    x2 = matmul([a_out.reshape(n, -1), b_out.reshape(n, -1)], w_out_ab[0], F32, residual=x2,
                name="out_proj_ab")
    wg, wu, wd = moe_w_gate, moe_w_up, moe_w_down
    x2, h = moe_layer(x2, 0, rms_ffn[0], rms_mix[1], True, BF16, moe_w_group[0], moe_b_group[0],
                      moe_w_router[0], moe_b_router[0], wg, wu, wd)

    dil_w = dil_heads * HEAD_DIM
    proj = matmul([h], w_in_c[0], BF16, perm_dils=tuple(dil for _, dil in DIL_CONFIGS),
                  q_cols=(3 * dil_w, 0, dil_w), name="in_proj_c").reshape(b, t, -1)
    o = dilated_attention(proj, t5_bias, dil_heads)
    x2 = matmul([o.reshape(n, -1)], w_out_c[0], F32, residual=x2, name="out_proj_c")
    (y,) = moe_layer(x2, 1, rms_ffn[1], rms_final, False, x.dtype, moe_w_group[1], moe_b_group[1],
                     moe_w_router[1], moe_b_router[1], wg, wu, wd)
    return y.reshape(b, t, d)
```

```python
import functools
import math

import jax
import jax.numpy as jnp
from jax import lax
from jax.experimental import pallas as pl
from jax.experimental.pallas import tpu as pltpu

F32 = jnp.float32
BF16 = jnp.bfloat16

HEAD_DIM = 128
CONV_WIDTH = 31
CONV_PAD = CONV_WIDTH // 2
NAT_WIN_ROWS = 8
NAT_WIN_COLS = 16
GRID_W = 64
DIL_CONFIGS = ((128, 1), (512, 4), (2048, 16))
T5_BUCKETS = 32
T5_MAX_DIST = 1024
N_GROUPS = 4
EXPERTS_PER_GROUP = 8
N_EXPERTS = N_GROUPS * EXPERTS_PER_GROUP
TOP_K = 2
RMS_EPS = 1e-6
LN_EPS = 1e-5
NEG = -1e30

VMEM_LIMIT = 56 * 1024 * 1024
VMEM_LIMIT_STAGED = 60 * 1024 * 1024
SUBLANES = 8
BF16_SUBLANES = 16

MM_TM = 1024
MM_TN = 512
MM_STAGED_TN = 1024
NORM_TM = 256
CONV_TT = 128
CONV_RB = 32
CONV_CB = 512
NAT_QROWS = 8
NAT_SUBROWS = 4
NAT_KROWS = 12
DIL_TQ = 1024
DIL_MQ = 256
DIL_HALF = 64
PERM_BLOCK = 256
MOE_TM = 256
SCAT_TM = 512
COMB_TM = 128
ROUTER_TM = 256


def _cparams(sem, vmem_limit=VMEM_LIMIT):
    return pltpu.CompilerParams(dimension_semantics=sem, vmem_limit_bytes=vmem_limit)


def _rmsnorm_kernel(x_ref, g_ref, o_ref):
    x = x_ref[...]
    ms = jnp.mean(x * x, axis=-1, keepdims=True)
    o_ref[...] = (x * lax.rsqrt(ms + RMS_EPS) * g_ref[...]).astype(o_ref.dtype)


def rmsnorm(x2, g, out_dtype):
    n, d = x2.shape
    return pl.pallas_call(
        _rmsnorm_kernel,
        grid=(n // NORM_TM,),
        in_specs=[pl.BlockSpec((NORM_TM, d), lambda i: (i, 0)),
                  pl.BlockSpec((1, d), lambda i: (0, 0))],
        out_specs=pl.BlockSpec((NORM_TM, d), lambda i: (i, 0)),
        out_shape=jax.ShapeDtypeStruct((n, d), out_dtype),
        compiler_params=_cparams(("arbitrary",)),
        name="rmsnorm",
    )(x2, g.reshape(1, d))


def _row_perm_matrix(dil):
    n = PERM_BLOCK // dil
    out = jnp.arange(PERM_BLOCK)
    src = dil * (out % n) + out // n
    return (src[:, None] == jnp.arange(PERM_BLOCK)[None, :]).astype(BF16)


def _q_factor(q_tiles):
    if q_tiles is None:
        return None
    period, lo, hi = q_tiles
    phase = lax.rem(pl.program_id(0), period)
    return jnp.where((phase >= lo) & (phase < hi), jnp.float32(HEAD_DIM ** -0.5), jnp.float32(1.0))


def _mm_store(out, o_ref, p_refs, perm_dils, tiles_per_group):
    if not p_refs:
        o_ref[...] = out
        return

    group = pl.program_id(0) // tiles_per_group
    p_iter = iter(p_refs)
    for g, dil in enumerate(perm_dils):
        p_ref = next(p_iter) if dil > 1 else None

        @pl.when(group == g)
        def _(p_ref=p_ref):
            if p_ref is None:
                o_ref[...] = out
            else:
                for blk in range(out.shape[0] // PERM_BLOCK):
                    rows = slice(blk * PERM_BLOCK, (blk + 1) * PERM_BLOCK)
                    o_ref[rows, :] = jnp.dot(p_ref[...], out[rows, :],
                                             preferred_element_type=F32).astype(o_ref.dtype)


def _mm_kernel(*refs, n_x, has_res, perm_dils, tiles_per_group, q_tiles):
    x_refs = refs[:n_x]
    w_refs = refs[n_x:2 * n_x]
    pos = 2 * n_x
    r_ref = refs[pos] if has_res else None
    pos += int(has_res)
    n_perm = sum(d > 1 for d in perm_dils)
    p_refs = refs[pos:pos + n_perm]
    pos += n_perm
    o_ref, wbf_ref = refs[pos], refs[pos + 1]

    @pl.when(pl.program_id(1) == 0)
    def _():
        factor = _q_factor(q_tiles)
        off = 0
        for w_ref in w_refs:
            kk = w_ref.shape[0]
            w = w_ref[...] if factor is None else w_ref[...] * factor
            wbf_ref[off:off + kk, :] = w.astype(BF16)
            off += kk

    acc = None
    off = 0
    for x_ref in x_refs:
        kk = x_ref.shape[1]
        part = jnp.dot(x_ref[...], wbf_ref[off:off + kk, :], preferred_element_type=F32)
        acc = part if acc is None else acc + part
        off += kk
    if has_res:
        acc = acc + r_ref[...]
    _mm_store(acc.astype(o_ref.dtype), o_ref, p_refs, perm_dils, tiles_per_group)


def _mm_staged_kernel(x_ref, w_hbm, *refs, has_res, perm_dils, tiles_per_group, q_tiles):
    r_ref = refs[0] if has_res else None
    refs = refs[int(has_res):]
    n_perm = sum(d > 1 for d in perm_dils)
    p_refs = refs[:n_perm]
    o_ref, stage_ref, wbf_ref, sem = refs[n_perm:]
    j, i = pl.program_id(0), pl.program_id(1)
    tn = wbf_ref.shape[1]

    def fetch(jj):
        return pltpu.make_async_copy(w_hbm.at[:, pl.ds(pl.multiple_of(jj * tn, tn), tn)], stage_ref, sem)

    @pl.when((j == 0) & (i == 0))
    def _():
        fetch(0).start()

    @pl.when(i == 0)
    def _():
        fetch(j).wait()
        factor = _q_factor(q_tiles)
        w = stage_ref[...] if factor is None else stage_ref[...] * factor
        wbf_ref[...] = w.astype(BF16)

        @pl.when(j + 1 < pl.num_programs(0))
        def _():
            fetch(j + 1).start()

    acc = jnp.dot(x_ref[...], wbf_ref[...], preferred_element_type=F32)
    if has_res:
        acc = acc + r_ref[...]
    _mm_store(acc.astype(o_ref.dtype), o_ref, p_refs, perm_dils, tiles_per_group)


def matmul(xs, w, out_dtype, residual=None, perm_dils=(), q_cols=None, name="matmul"):
    m = xs[0].shape[0]
    ks = [x.shape[1] for x in xs]
    ktot, nc = w.shape
    assert sum(ks) == ktot and all(k == ks[0] for k in ks)
    staged = len(xs) == 1
    tm, tn = min(MM_TM, m), min(MM_STAGED_TN if staged else MM_TN, nc)
    assert m % tm == 0 and nc % tn == 0
    in_specs = [pl.BlockSpec((tm, k), lambda j, i: (i, 0)) for k in ks]
    if staged:
        in_specs.append(pl.BlockSpec(memory_space=pl.ANY))
        args = [xs[0], w]
    else:
        in_specs += [pl.BlockSpec((k, tn), functools.partial(lambda j, i, p: (p, j), p=p))
                     for p, k in enumerate(ks)]
        args = list(xs) + [w] * len(xs)
    if residual is not None:
        in_specs.append(pl.BlockSpec((tm, tn), lambda j, i: (i, j)))
        args.append(residual)
    tiles_per_group = 0
    if perm_dils:
        assert out_dtype == BF16 and tm % PERM_BLOCK == 0 and (nc // tn) % len(perm_dils) == 0
        tiles_per_group = (nc // tn) // len(perm_dils)
        for dil in perm_dils:
            if dil > 1:
                in_specs.append(pl.BlockSpec((PERM_BLOCK, PERM_BLOCK), lambda j, i: (0, 0)))
                args.append(_row_perm_matrix(dil))
    q_tiles = None
    if q_cols is not None:
        assert all(c % tn == 0 for c in q_cols)
        q_tiles = tuple(c // tn for c in q_cols)
    common = dict(perm_dils=tuple(perm_dils), tiles_per_group=tiles_per_group, q_tiles=q_tiles)
    if staged:
        body = functools.partial(_mm_staged_kernel, has_res=residual is not None, **common)
        scratch = [pltpu.VMEM((ktot, tn), F32), pltpu.VMEM((ktot, tn), BF16), pltpu.SemaphoreType.DMA(())]
    else:
        body = functools.partial(_mm_kernel, n_x=len(xs), has_res=residual is not None, **common)
        scratch = [pltpu.VMEM((ktot, tn), BF16)]
    return pl.pallas_call(
        body,
        grid=(nc // tn, m // tm),
        in_specs=in_specs,
        out_specs=pl.BlockSpec((tm, tn), lambda j, i: (i, j)),
        out_shape=jax.ShapeDtypeStruct((m, nc), out_dtype),
        scratch_shapes=scratch,
        compiler_params=_cparams(("arbitrary", "arbitrary"), VMEM_LIMIT_STAGED if staged else VMEM_LIMIT),
        name=name,
    )(*args)


def _conv_kernel(ac_ref, gc_ref, ap_ref, gp_ref, an_ref, gn_ref, w_ref, b_ref, lg_ref, lb_ref,
                 o_ref, buf_ref, cv_ref, *, tt, ch):
    i = pl.program_id(1)
    last = pl.num_programs(1) - 1
    halo = BF16_SUBLANES

    def glu(a_ref, g_ref):
        return a_ref[0].astype(F32) * jax.nn.sigmoid(g_ref[0].astype(F32))

    buf_ref[0, 0:halo, :] = jnp.where(i > 0, glu(ap_ref, gp_ref), 0.0)
    buf_ref[0, halo:halo + tt, :] = glu(ac_ref, gc_ref)
    buf_ref[0, halo + tt:2 * halo + tt, :] = jnp.where(i < last, glu(an_ref, gn_ref), 0.0)
    span = tt + 2 * halo - SUBLANES
    for s in range(1, SUBLANES):
        buf_ref[s, 0:span, :] = buf_ref[0, s:s + span, :]

    base = halo - CONV_PAD
    for cb in range(ch // CONV_CB):
        cs = slice(cb * CONV_CB, (cb + 1) * CONV_CB)
        for rb in range(tt // CONV_RB):
            accs = [jnp.zeros((SUBLANES, CONV_CB), F32) for _ in range(CONV_RB // SUBLANES)]
            for j in range(CONV_WIDTH):
                shift = (base + j) % SUBLANES
                r0 = rb * CONV_RB + base + j - shift
                wj = w_ref[j, :, cs]
                for k in range(len(accs)):
                    rk = r0 + k * SUBLANES
                    accs[k] = accs[k] + wj * buf_ref[shift, rk:rk + SUBLANES, cs]
            for k, acc in enumerate(accs):
                rk = rb * CONV_RB + k * SUBLANES
                cv_ref[rk:rk + SUBLANES, cs] = acc + b_ref[:, cs]

    a = cv_ref[...]
    mu = jnp.mean(a, axis=-1, keepdims=True)
    ctr = a - mu
    var = jnp.mean(ctr * ctr, axis=-1, keepdims=True)
    y = ctr * lax.rsqrt(var + LN_EPS) * lg_ref[...] + lb_ref[...]
    o_ref[0] = (y * jax.nn.sigmoid(y)).astype(o_ref.dtype)


def conformer_conv(proj3, w_dw, b_dw, ln_g, ln_b):
    b, t, _ = proj3.shape
    ch = w_dw.shape[1]
    tt, halo = CONV_TT, BF16_SUBLANES
    hb = tt // halo
    n_hb = t // halo
    cur = lambda col: pl.BlockSpec((1, tt, ch), lambda bi, i: (bi, i, col))
    prv = lambda col: pl.BlockSpec((1, halo, ch), lambda bi, i: (bi, jnp.maximum(i * hb - 1, 0), col))
    nxt = lambda col: pl.BlockSpec((1, halo, ch), lambda bi, i: (bi, jnp.minimum((i + 1) * hb, n_hb - 1), col))
    vec = lambda rows: pl.BlockSpec((rows, ch), lambda bi, i: (0, 0))
    return pl.pallas_call(
        functools.partial(_conv_kernel, tt=tt, ch=ch),
        grid=(b, t // tt),
        in_specs=[cur(0), cur(1), prv(0), prv(1), nxt(0), nxt(1),
                  pl.BlockSpec((CONV_WIDTH, SUBLANES, ch), lambda bi, i: (0, 0, 0)), vec(1), vec(1), vec(1)],
        out_specs=pl.BlockSpec((1, tt, ch), lambda bi, i: (bi, i, 0)),
        out_shape=jax.ShapeDtypeStruct((b, t, ch), BF16),
        scratch_shapes=[pltpu.VMEM((SUBLANES, tt + 2 * halo, ch), F32), pltpu.VMEM((tt, ch), F32)],
        compiler_params=_cparams(("arbitrary", "arbitrary")),
        name="conformer_conv",
    )(proj3, proj3, proj3, proj3, proj3, proj3,
      jnp.broadcast_to(w_dw[:, None, :], (CONV_WIDTH, SUBLANES, ch)), b_dw.reshape(1, ch), ln_g.reshape(1, ch),
      ln_b.reshape(1, ch))


def _nat_bias_table(rpb):
    h = rpb.shape[0]
    kc = NAT_WIN_COLS
    cols = jnp.arange(GRID_W)
    col_start = jnp.clip(cols - kc // 2, 0, GRID_W - kc)
    col_mask = (cols[None, :] >= col_start[:, None]) & (cols[None, :] < col_start[:, None] + kc)
    dc_idx = jnp.clip(cols[None, :] - cols[:, None], -(kc - 1), kc - 1) + (kc - 1)
    bc = jnp.where(col_mask[None, None], rpb[:, :, dc_idx].astype(F32), NEG)
    blank = jnp.full((h, 1, GRID_W, GRID_W), NEG, F32)
    bc = jnp.concatenate([blank, bc, blank], axis=1)
    return jnp.concatenate([bc[:, :-1], bc[:, 1:]], axis=-1)


def _nat_kernel(q_ref, k_ref, v_ref, tab_ref, o_ref, bias_ref, *, rows):
    blk = pl.program_id(2)
    r_base = blk * NAT_QROWS
    n_sub = NAT_QROWS // NAT_SUBROWS
    tq = NAT_SUBROWS * GRID_W
    ws = [jnp.clip(r_base + j * NAT_SUBROWS - NAT_WIN_ROWS // 2, 0, rows - NAT_KROWS) for j in range(n_sub)]
    lane = lax.broadcasted_iota(jnp.int32, (GRID_W, 2 * GRID_W), 1)

    @pl.when((blk <= 1) | (blk == pl.num_programs(2) - 1))
    def _():
        for j in range(n_sub):
            for qr in range(NAT_SUBROWS):
                r = r_base + j * NAT_SUBROWS + qr
                r0 = jnp.clip(r - NAT_WIN_ROWS // 2, 0, rows - NAT_WIN_ROWS)
                for p in range(NAT_KROWS // 2):
                    kr = ws[j] + 2 * p
                    entry = jnp.clip(kr - r + NAT_WIN_ROWS, 0, 2 * NAT_WIN_ROWS - 1)
                    ok0 = ((kr >= r0) & (kr < r0 + NAT_WIN_ROWS)).astype(jnp.int32)
                    ok1 = ((kr + 1 >= r0) & (kr + 1 < r0 + NAT_WIN_ROWS)).astype(jnp.int32)
                    ok = jnp.where(lane < GRID_W, ok0, ok1) > 0
                    bias_ref[j, qr * GRID_W:(qr + 1) * GRID_W, p * 2 * GRID_W:(p + 1) * 2 * GRID_W] = (
                        jnp.where(ok, tab_ref[0, entry], NEG))

    def window(ref, j):
        return ref[0, pl.ds(pl.multiple_of(ws[j] * GRID_W, 128), NAT_KROWS * GRID_W), :]

    scores = [lax.dot_general(q_ref[0, j * tq:(j + 1) * tq, :], window(k_ref, j), (((1,), (1,)), ((), ())),
                              preferred_element_type=F32) + bias_ref[j]
              for j in range(n_sub)]
    stats = []
    for s in scores:
        m = jnp.max(s, axis=-1, keepdims=True)
        p = jnp.exp(s - m)
        stats.append((jnp.sum(p, axis=-1, keepdims=True), p.astype(BF16)))
    for j, (den, p) in enumerate(stats):
        o = jnp.dot(p, window(v_ref, j), preferred_element_type=F32)
        o_ref[0, j * tq:(j + 1) * tq, :] = (o / den).astype(o_ref.dtype)


def neighbourhood_attention(proj3, rpb, col0):
    b, t, _ = proj3.shape
    nh = rpb.shape[0]
    rows = t // GRID_W
    assert rows >= NAT_KROWS and rows % NAT_QROWS == 0
    tq = NAT_QROWS * GRID_W
    tab = _nat_bias_table(rpb)
    return pl.pallas_call(
        functools.partial(_nat_kernel, rows=rows),
        grid=(nh, b, rows // NAT_QROWS),
        in_specs=[pl.BlockSpec((1, tq, HEAD_DIM), lambda h, bi, i: (bi, i, col0 + h)),
                  pl.BlockSpec((1, t, HEAD_DIM), lambda h, bi, i: (bi, 0, col0 + nh + h)),
                  pl.BlockSpec((1, t, HEAD_DIM), lambda h, bi, i: (bi, 0, col0 + 2 * nh + h)),
                  pl.BlockSpec((1, 2 * NAT_WIN_ROWS, GRID_W, 2 * GRID_W), lambda h, bi, i: (h, 0, 0, 0))],
        out_specs=pl.BlockSpec((1, tq, HEAD_DIM), lambda h, bi, i: (bi, i, h)),
        out_shape=jax.ShapeDtypeStruct((b, t, nh * HEAD_DIM), BF16),
        scratch_shapes=[pltpu.VMEM((NAT_QROWS // NAT_SUBROWS, NAT_SUBROWS * GRID_W, NAT_KROWS * GRID_W), F32)],
        compiler_params=_cparams(("arbitrary", "arbitrary", "arbitrary")),
        name="neighbourhood_attention",
    )(proj3, proj3, proj3, tab)


def _t5_bucket(rel):
    nb = T5_BUCKETS // 2
    max_exact = nb // 2
    n = jnp.abs(rel)
    sign = jnp.where(rel > 0, nb, 0)
    nf = jnp.maximum(n, 1).astype(F32)
    large = max_exact + (jnp.log(nf / max_exact) / math.log(T5_MAX_DIST / max_exact)
                         * (nb - max_exact)).astype(jnp.int32)
    large = jnp.minimum(large, nb - 1)
    return sign + jnp.where(n < max_exact, n, large)


def _dil_dims(g):
    win, dil = DIL_CONFIGS[g]
    assert win == 2 * dil * DIL_HALF
    mq = min(DIL_MQ, DIL_TQ // dil)
    wk = -(-(mq + 2 * DIL_HALF) // 128) * 128
    front = dil * DIL_HALF
    back = dil * (wk - mq - DIL_HALF)
    return dil, mq, wk, front, back


def _dil_bias_rows(t5_bias, g, nh):
    dil, mq, wk, _, _ = _dil_dims(g)
    delta = jnp.arange(wk) - DIL_HALF
    on = delta <= DIL_HALF
    vals = t5_bias[:, g * nh:(g + 1) * nh][_t5_bucket(delta * dil)].astype(F32)
    return jnp.where(on[None], vals.T, NEG)[:, None, :]


def _dil_kernel(q0_ref, q1_ref, q2_ref, u0_ref, u1_ref, u2_ref, proj_ref, o_ref,
                k0, v0, k1, v1, k2, v2, b0_ref, b1_ref, b2_ref, m_ref, l_ref, n_ref, sem, *, t, nh):
    h, bi, i = pl.program_id(0), pl.program_id(1), pl.program_id(2)
    kv = ((k0, v0), (k1, v1), (k2, v2))

    @pl.when((bi == 0) & (i == 0))
    def _():
        for u_ref, b_ref in ((u0_ref, b0_ref), (u1_ref, b1_ref), (u2_ref, b2_ref)):
            rows = jnp.broadcast_to(u_ref[0], b_ref.shape)
            b_ref[...] = pltpu.roll(rows, 0, 1, stride=1, stride_axis=0)

    n_b = pl.num_programs(1)
    pair = h * n_b + bi
    slot = lax.rem(pair, 2)

    def kv_copies(pair_, slot_):
        h_, b_ = pair_ // n_b, lax.rem(pair_, n_b)
        cps = []
        for g in range(3):
            front = _dil_dims(g)[3]
            for w in range(2):
                col = pl.multiple_of(((g * 3 + 1 + w) * nh + h_) * HEAD_DIM, HEAD_DIM)
                cps.append(pltpu.make_async_copy(proj_ref.at[b_, :, pl.ds(col, HEAD_DIM)],
                                                 kv[g][w].at[slot_, pl.ds(front, t), :],
                                                 sem.at[slot_, g * 2 + w]))
        return cps

    @pl.when(i == 0)
    def _():
        @pl.when(pair == 0)
        def _():
            for cp in kv_copies(pair, slot):
                cp.start()
            for g in range(3):
                _, _, _, front, back = _dil_dims(g)
                for w in range(2):
                    for s in range(2):
                        kv[g][w][s, 0:front, :] = jnp.zeros((front, HEAD_DIM), BF16)
                        kv[g][w][s, front + t:front + t + back, :] = jnp.zeros((back, HEAD_DIM), BF16)

        for cp in kv_copies(pair, slot):
            cp.wait()

        @pl.when(pair + 1 < pl.num_programs(0) * n_b)
        def _():
            for cp in kv_copies(pair + 1, 1 - slot):
                cp.start()

    t0 = pl.multiple_of(i * DIL_TQ, DIL_TQ)

    def class_rows(ref, first_block_row, n_blocks, dil, r):
        n = PERM_BLOCK // dil
        parts = []
        for blk in range(n_blocks):
            start = first_block_row + blk * PERM_BLOCK + r * n
            if not isinstance(start, int):
                start = pl.multiple_of(start, n)
            parts.append(ref[pl.ds(start, n), :])
        return jnp.concatenate(parts, axis=0)

    def group_tiles(g, q_ref, b_ref):
        dil, mq, wk, front, _ = _dil_dims(g)
        if dil == 1:
            tiles = [(q_ref[0, j * mq:(j + 1) * mq, :], t0 + j * mq, 0, slice(j * mq, (j + 1) * mq))
                     for j in range(DIL_TQ // mq)]
        else:
            assert dil * mq == DIL_TQ
            tiles = [(class_rows(q_ref.at[0], 0, DIL_TQ // PERM_BLOCK, dil, r), t0, r,
                      pl.ds(r, mq, stride=dil)) for r in range(dil)]

        def keys_values(which, row0, r):
            if dil == 1:
                return kv[g][which][slot, pl.ds(pl.multiple_of(row0, 128), wk), :]
            return class_rows(kv[g][which].at[slot], row0, wk * dil // PERM_BLOCK, dil, r)

        scores = []
        for q, row0, r, _ in tiles:
            s = lax.dot_general(q, keys_values(0, row0, r), (((1,), (1,)), ((), ())),
                                preferred_element_type=F32)
            kpos = (row0 - front + r) + dil * lax.broadcasted_iota(jnp.int32, (1, wk), 1)
            kmask = jnp.where((kpos >= 0) & (kpos < t), 0.0, NEG)
            scores.append(s + b_ref[...] + kmask)
        stats = []
        for s in scores:
            m = jnp.max(s, axis=-1, keepdims=True)
            p = jnp.exp(s - m)
            stats.append((m, jnp.sum(p, axis=-1, keepdims=True), p.astype(BF16)))
        for (_, row0, r, rows), (m, den, p) in zip(tiles, stats):
            n_ref[g, rows, :] = jnp.dot(p, keys_values(1, row0, r), preferred_element_type=F32)
            m_ref[g, rows, :] = jnp.broadcast_to(m, (mq, HEAD_DIM))
            l_ref[g, rows, :] = jnp.broadcast_to(den, (mq, HEAD_DIM))

    for g, q_ref, b_ref in ((0, q0_ref, b0_ref), (1, q1_ref, b1_ref), (2, q2_ref, b2_ref)):
        group_tiles(g, q_ref, b_ref)

    m_all = jnp.maximum(jnp.maximum(m_ref[0], m_ref[1]), m_ref[2])
    num = jnp.zeros((DIL_TQ, HEAD_DIM), F32)
    den = jnp.zeros((DIL_TQ, HEAD_DIM), F32)
    for g in range(3):
        w = jnp.exp(m_ref[g] - m_all)
        num = num + w * n_ref[g]
        den = den + w * l_ref[g]
    o_ref[0] = (num / den).astype(o_ref.dtype)


def dilated_attention(proj3, t5_bias, nh):
    b, t, _ = proj3.shape
    assert t % DIL_TQ == 0
    tabs = [_dil_bias_rows(t5_bias, g, nh) for g in range(3)]
    qspec = lambda g: pl.BlockSpec((1, DIL_TQ, HEAD_DIM), lambda h, bi, i: (bi, i, g * 3 * nh + h))
    bspec = lambda g: pl.BlockSpec((1, 1, _dil_dims(g)[2]), lambda h, bi, i: (h, 0, 0))
    scratch = []
    for g in range(3):
        _, _, _, front, back = _dil_dims(g)
        assert g == 0 or (front % PERM_BLOCK == 0 and back % PERM_BLOCK == 0)
        scratch += [pltpu.VMEM((2, front + t + back, HEAD_DIM), BF16)] * 2
    scratch += [pltpu.VMEM(_dil_dims(g)[1:3], F32) for g in range(3)]
    scratch += [pltpu.VMEM((3, DIL_TQ, HEAD_DIM), F32)] * 3
    scratch.append(pltpu.SemaphoreType.DMA((2, 6)))
    return pl.pallas_call(
        functools.partial(_dil_kernel, t=t, nh=nh),
        grid=(nh, b, t // DIL_TQ),
        in_specs=[qspec(0), qspec(1), qspec(2), bspec(0), bspec(1), bspec(2),
                  pl.BlockSpec(memory_space=pl.ANY)],
        out_specs=pl.BlockSpec((1, DIL_TQ, HEAD_DIM), lambda h, bi, i: (bi, i, h)),
        out_shape=jax.ShapeDtypeStruct((b, t, nh * HEAD_DIM), BF16),
        scratch_shapes=scratch,
        compiler_params=_cparams(("arbitrary", "arbitrary", "arbitrary")),
        name="dilated_attention",
    )(proj3, proj3, proj3, tabs[0], tabs[1], tabs[2], proj3)


ROUTER_LANES = 128


def _pack_rounded(lo, hi):
    lo_bits = lax.bitcast_convert_type(lo, jnp.uint32)
    hi_bits = lax.bitcast_convert_type(hi, jnp.uint32)
    return lax.shift_right_logical(lo_bits, jnp.uint32(16)) | (hi_bits & jnp.uint32(0xFFFF0000))


def _pack_pair(lo, hi):
    return _pack_rounded(lo.astype(BF16).astype(F32), hi.astype(BF16).astype(F32))


def _unpack_pair(word):
    lo = lax.bitcast_convert_type(lax.shift_left(word, jnp.uint32(16)), F32)
    hi = lax.bitcast_convert_type(word & jnp.uint32(0xFFFF0000), F32)
    return lo, hi


def _router_kernel(x_ref, g_ref, wcat_ref, b_ref, o_ref, hp_ref, cnt_ref, carry_ref):
    step = pl.program_id(0)
    half = x_ref.shape[1] // 2
    x = x_ref[...]
    ms = jnp.mean(x * x, axis=-1, keepdims=True)
    h = x * lax.rsqrt(ms + RMS_EPS) * g_ref[...]
    h_hi = h.astype(BF16)
    h_hi32 = h_hi.astype(F32)
    h_lo = (h - h_hi32).astype(BF16)
    hp_ref[...] = _pack_rounded(h_hi32[:, :half], h_hi32[:, half:])
    hi_both = jnp.dot(h_hi, wcat_ref[...], preferred_element_type=F32)
    logits = (hi_both[:, :ROUTER_LANES] + hi_both[:, ROUTER_LANES:]
              + jnp.dot(h_lo, wcat_ref[:, :ROUTER_LANES], preferred_element_type=F32)) + b_ref[...]

    lane = lax.broadcasted_iota(jnp.int32, logits.shape, 1)
    big = jnp.int32(ROUTER_LANES)

    def first_argmax(vals):
        top = jnp.max(vals, axis=-1, keepdims=True)
        return top, jnp.min(jnp.where(vals == top, lane, big), axis=-1, keepdims=True)

    gl = jnp.where(lane < N_GROUPS, logits, NEG)
    g_top, g_idx = first_argmax(gl)
    g_val = 1.0 / jnp.sum(jnp.exp(gl - g_top), axis=-1, keepdims=True)
    in_group = (lane >= N_GROUPS) & (lane < N_GROUPS + N_EXPERTS) & (
        lax.shift_right_arithmetic(lane - N_GROUPS, jnp.int32(3)) == g_idx)
    el = jnp.where(in_group, logits, NEG)
    v1, i1 = first_argmax(el)
    el2 = jnp.where(lane == i1, NEG, el)
    v2, i2 = first_argmax(el2)
    e21 = jnp.exp(v2 - v1)
    gate1 = g_val / (1.0 + e21)
    gate2 = g_val * e21 / (1.0 + e21)

    @pl.when(step == 0)
    def _():
        carry_ref[...] = jnp.zeros(carry_ref.shape, F32)

    e1, e2 = i1 - N_GROUPS, i2 - N_GROUPS
    oh1 = (lane == e1).astype(F32)
    oh2 = (lane == e2).astype(F32)
    both = oh1 + oh2
    tm = x.shape[0]
    tri = (lax.broadcasted_iota(jnp.int32, (tm, tm), 1) < lax.broadcasted_iota(jnp.int32, (tm, tm), 0))
    before = jnp.dot(tri.astype(BF16), both.astype(BF16), preferred_element_type=F32) + carry_ref[...]
    rank1 = jnp.sum(oh1 * before, axis=-1, keepdims=True)
    rank2 = jnp.sum(oh2 * before, axis=-1, keepdims=True)
    carry_ref[...] = carry_ref[...] + jnp.sum(both, axis=0, keepdims=True)
    cnt_ref[...] = jnp.broadcast_to(carry_ref[...], cnt_ref.shape)

    out = jnp.where(lane == 0, gate1, 0.0)
    out = jnp.where(lane == 1, gate2, out)
    out = jnp.where(lane == 2, e1.astype(F32), out)
    out = jnp.where(lane == 3, e2.astype(F32), out)
    out = jnp.where(lane == 4, rank1, out)
    out = jnp.where(lane == 5, rank2, out)
    o_ref[...] = out


def router(x2, gain, w_group, b_group, w_router, b_router):
    n, d = x2.shape
    wcat = jnp.concatenate([w_group, w_router], axis=1).astype(F32)
    wcat = jnp.pad(wcat, ((0, 0), (0, ROUTER_LANES - wcat.shape[1])))
    w_hi = wcat.astype(BF16)
    w_lo = (wcat - w_hi.astype(F32)).astype(BF16)
    bias = jnp.concatenate([b_group.reshape(-1), b_router.reshape(-1)]).astype(F32)
    bias = jnp.pad(bias, (0, ROUTER_LANES - bias.shape[0])).reshape(1, ROUTER_LANES)
    tm = ROUTER_TM
    return pl.pallas_call(
        _router_kernel,
        grid=(n // tm,),
        in_specs=[pl.BlockSpec((tm, d), lambda i: (i, 0)),
                  pl.BlockSpec((1, d), lambda i: (0, 0)),
                  pl.BlockSpec((d, 2 * ROUTER_LANES), lambda i: (0, 0)),
                  pl.BlockSpec((1, ROUTER_LANES), lambda i: (0, 0))],
        out_specs=[pl.BlockSpec((tm, ROUTER_LANES), lambda i: (i, 0)),
                   pl.BlockSpec((tm, d // 2), lambda i: (i, 0)),
                   pl.BlockSpec((8, ROUTER_LANES), lambda i: (0, 0))],
        out_shape=[jax.ShapeDtypeStruct((n, ROUTER_LANES), F32),
                   jax.ShapeDtypeStruct((n, d // 2), jnp.uint32),
                   jax.ShapeDtypeStruct((8, ROUTER_LANES), F32)],
        scratch_shapes=[pltpu.VMEM((1, ROUTER_LANES), F32)],
        compiler_params=_cparams(("arbitrary",)),
        name="moe_router",
    )(x2, gain.reshape(1, d), jnp.concatenate([w_hi, w_lo], axis=1), bias)


def dispatch(route, counts, n):
    n_blocks = n * TOP_K // MOE_TM + N_EXPERTS
    counts = counts[0, :N_EXPERTS].astype(jnp.int32)
    pcounts = (counts + MOE_TM - 1) // MOE_TM * MOE_TM
    pends = jnp.cumsum(pcounts)
    pstarts = pends - pcounts
    expert = route[:, 2:2 + TOP_K].astype(jnp.int32)
    rank = route[:, 2 + TOP_K:2 + 2 * TOP_K].astype(jnp.int32)
    ids = jnp.arange(N_EXPERTS, dtype=jnp.int32)
    base = jnp.sum(jnp.where(expert[..., None] == ids, pstarts, 0), axis=-1)
    slot_of = (base + rank).reshape(-1)
    blk_start = jnp.arange(n_blocks, dtype=jnp.int32) * MOE_TM
    block_e = jnp.minimum(jnp.sum(blk_start[:, None] >= pends[None, :], axis=-1), N_EXPERTS - 1)
    block_e = block_e.astype(jnp.int32)
    n_on = (pends[-1] // MOE_TM).astype(jnp.int32)
    pad_from = pstarts + counts
    blk_ids = jnp.arange(n_blocks, dtype=jnp.int32)
    first = ((blk_ids == 0) | (block_e != jnp.roll(block_e, 1))) & (blk_ids < n_on)
    later = jnp.where((ids[None, :] > ids[:, None]) & (counts[None, :] > 0), ids[None, :], N_EXPERTS)
    next_nonempty = jnp.min(later, axis=-1)
    next_e = jnp.sum(jnp.where(block_e[:, None] == ids, next_nonempty, 0), axis=-1)
    next_e = jnp.where(next_e < N_EXPERTS, next_e, -1).astype(jnp.int32)
    return slot_of, block_e, n_on.reshape(1), first.astype(jnp.int32), next_e, pad_from, pends


def _scatter_kernel(slot_ref, padfrom_ref, pend_ref, hp_ref, xs_hbm, zero_ref, sem, zsem):
    i = pl.program_id(0)
    tm = SCAT_TM
    n_blocks = xs_hbm.shape[0] // MOE_TM

    def pad_rows(fn):
        for e in range(N_EXPERTS):
            def body(r, carry):
                fn(pltpu.make_async_copy(zero_ref.at[pl.ds(0, 1), :], xs_hbm.at[pl.ds(r, 1), :], zsem))
                return carry
            lax.fori_loop(padfrom_ref[e], pend_ref[e], body, 0)

        def blk(b, carry):
            start = pl.multiple_of(b * MOE_TM, MOE_TM)
            fn(pltpu.make_async_copy(zero_ref, xs_hbm.at[pl.ds(start, MOE_TM), :], zsem))
            return carry
        lax.fori_loop(pend_ref[N_EXPERTS - 1] // MOE_TM, n_blocks, blk, 0)

    @pl.when(i == 0)
    def _():
        zero_ref[...] = jnp.zeros(zero_ref.shape, zero_ref.dtype)
        pad_rows(lambda cp: cp.start())

    def body(s, carry):
        for k in range(TOP_K):
            row = slot_ref[(i * tm + s) * TOP_K + k]
            pltpu.make_async_copy(hp_ref.at[pl.ds(s, 1), :], xs_hbm.at[pl.ds(row, 1), :], sem).start(priority=k)
        return carry
    lax.fori_loop(0, tm, body, 0, unroll=8)
    for k in range(TOP_K):
        pltpu.make_async_copy(hp_ref, xs_hbm.at[pl.ds(0, tm), :], sem).wait()

    @pl.when(i == pl.num_programs(0) - 1)
    def _():
        pad_rows(lambda cp: cp.wait())


def scatter_rows(hp, slot_of, pad_from, pends, n_rows):
    n, w = hp.shape
    grid_spec = pltpu.PrefetchScalarGridSpec(
        num_scalar_prefetch=3,
        grid=(n // SCAT_TM,),
        in_specs=[pl.BlockSpec((SCAT_TM, w), lambda i, slots, pf, pe: (i, 0))],
        out_specs=pl.BlockSpec(memory_space=pl.ANY),
        scratch_shapes=[pltpu.VMEM((MOE_TM, w), hp.dtype),
                        pltpu.SemaphoreType.DMA(()), pltpu.SemaphoreType.DMA(())],
    )
    return pl.pallas_call(
        _scatter_kernel,
        grid_spec=grid_spec,
        out_shape=jax.ShapeDtypeStruct((n_rows, w), hp.dtype),
        compiler_params=_cparams(("arbitrary",)),
        name="moe_scatter",
    )(slot_of, pad_from, pends, hp)


def _expert_kernel(be_ref, non_ref, first_ref, next_ref, xs_ref, wg_hbm, wu_hbm, wd_hbm, o_ref,
                   stage_g, stage_u, stage_d, wg_ref, wu_ref, wd_ref, sem, *, layer):
    b = pl.program_id(0)
    on = b < non_ref[0]
    staged = ((wg_hbm, stage_g), (wu_hbm, stage_u), (wd_hbm, stage_d))

    def weight_copies(e):
        return [pltpu.make_async_copy(w_hbm.at[layer, e], stage, sem.at[k])
                for k, (w_hbm, stage) in enumerate(staged)]

    @pl.when(b == 0)
    def _():
        for cp in weight_copies(be_ref[0]):
            cp.start()

    @pl.when(first_ref[b] > 0)
    def _():
        for cp in weight_copies(be_ref[b]):
            cp.wait()
        for (_, stage), w_ref in zip(staged, (wg_ref, wu_ref, wd_ref)):
            w_ref[...] = stage[...].astype(BF16)

        @pl.when(next_ref[b] >= 0)
        def _():
            for cp in weight_copies(next_ref[b]):
                cp.start()

    @pl.when(jnp.logical_not(on))
    def _():
        o_ref[...] = jnp.zeros(o_ref.shape, o_ref.dtype)

    @pl.when(on)
    def _():
        lo, hi = _unpack_pair(xs_ref[...])
        lo, hi = lo.astype(BF16), hi.astype(BF16)
        half = lo.shape[1]

        def proj(w_ref):
            return (jnp.dot(lo, w_ref[:half, :], preferred_element_type=F32)
                    + jnp.dot(hi, w_ref[half:, :], preferred_element_type=F32))

        g = proj(wg_ref)
        u = proj(wu_ref)
        hdn = (g * jax.nn.sigmoid(g) * u).astype(BF16)
        y = jnp.dot(hdn, wd_ref[...], preferred_element_type=F32)
        o_ref[...] = _pack_pair(y[:, :half], y[:, half:])


def expert_blocks(xs, block_e, n_on, first, next_e, layer, w_gate, w_up, w_down):
    n_rows, half = xs.shape
    _, _, d, hid = w_gate.shape
    tm = MOE_TM
    row = lambda b, *_: (b, 0)
    grid_spec = pltpu.PrefetchScalarGridSpec(
        num_scalar_prefetch=4,
        grid=(n_rows // tm,),
        in_specs=[pl.BlockSpec((tm, half), row)] + [pl.BlockSpec(memory_space=pl.ANY)] * 3,
        out_specs=pl.BlockSpec((tm, half), row),
        scratch_shapes=[pltpu.VMEM((d, hid), F32), pltpu.VMEM((d, hid), F32), pltpu.VMEM((hid, d), F32),
                        pltpu.VMEM((d, hid), BF16), pltpu.VMEM((d, hid), BF16), pltpu.VMEM((hid, d), BF16),
                        pltpu.SemaphoreType.DMA((3,))],
    )
    return pl.pallas_call(
        functools.partial(_expert_kernel, layer=layer),
        grid_spec=grid_spec,
        out_shape=jax.ShapeDtypeStruct((n_rows, half), xs.dtype),
        compiler_params=_cparams(("arbitrary",)),
        name="moe_experts",
    )(block_e, n_on, first, next_e, xs, w_gate, w_up, w_down)


def _combine_kernel(slot_ref, x_ref, r_ref, gain_ref, y_hbm, *rest, emit_x):
    if emit_x:
        xo_ref, ho_ref, ybuf, sem = rest
    else:
        ho_ref, ybuf, sem = rest
        xo_ref = None
    i = pl.program_id(0)
    ni = pl.num_programs(0)
    slot = i % 2
    tm = COMB_TM

    def gather(blk, slot_):
        def body(s, carry):
            for k in range(TOP_K):
                row = slot_ref[(blk * tm + s) * TOP_K + k]
                pltpu.make_async_copy(y_hbm.at[pl.ds(row, 1), :], ybuf.at[slot_, k, pl.ds(s, 1), :],
                                      sem.at[slot_]).start()
            return carry
        lax.fori_loop(0, tm, body, 0, unroll=8)

    @pl.when(i == 0)
    def _():
        gather(0, 0)

    @pl.when(i + 1 < ni)
    def _():
        gather(i + 1, 1 - slot)

    for k in range(TOP_K):
        pltpu.make_async_copy(y_hbm.at[pl.ds(0, tm), :], ybuf.at[slot, k], sem.at[slot]).wait()

    half = x_ref.shape[1] // 2
    g0, g1 = r_ref[:, 0:1], r_ref[:, 1:2]
    y0_lo, y0_hi = _unpack_pair(ybuf[slot, 0])
    y1_lo, y1_hi = _unpack_pair(ybuf[slot, 1])
    x_lo = x_ref[:, :half] + (g0 * y0_lo + g1 * y1_lo)
    x_hi = x_ref[:, half:] + (g0 * y0_hi + g1 * y1_hi)
    if emit_x:
        xo_ref[:, :half] = x_lo
        xo_ref[:, half:] = x_hi
    ssq = jnp.sum(x_lo * x_lo, axis=-1, keepdims=True) + jnp.sum(x_hi * x_hi, axis=-1, keepdims=True)
    inv = lax.rsqrt(ssq / x_ref.shape[1] + RMS_EPS)
    ho_ref[:, :half] = (x_lo * inv * gain_ref[:, :half]).astype(ho_ref.dtype)
    ho_ref[:, half:] = (x_hi * inv * gain_ref[:, half:]).astype(ho_ref.dtype)


def combine(x2, route, y_rows, slot_of, gain, emit_x, h_dtype):
    n, d = x2.shape
    tm = COMB_TM
    row = lambda i, slots: (i, 0)
    out_shape = [jax.ShapeDtypeStruct((n, d), h_dtype)]
    out_specs = [pl.BlockSpec((tm, d), row)]
    if emit_x:
        out_shape.insert(0, jax.ShapeDtypeStruct((n, d), F32))
        out_specs.insert(0, pl.BlockSpec((tm, d), row))
    grid_spec = pltpu.PrefetchScalarGridSpec(
        num_scalar_prefetch=1,
        grid=(n // tm,),
        in_specs=[pl.BlockSpec((tm, d), row),
                  pl.BlockSpec((tm, ROUTER_LANES), row),
                  pl.BlockSpec((1, d), lambda i, slots: (0, 0)),
                  pl.BlockSpec(memory_space=pl.ANY)],
        out_specs=out_specs,
        scratch_shapes=[pltpu.VMEM((2, TOP_K, tm, d // 2), y_rows.dtype), pltpu.SemaphoreType.DMA((2,))],
    )
    return pl.pallas_call(
        functools.partial(_combine_kernel, emit_x=emit_x),
        grid_spec=grid_spec,
        out_shape=out_shape,
        compiler_params=_cparams(("arbitrary",)),
        name="moe_combine",
    )(slot_of, x2, route, gain.reshape(1, d), y_rows)


def moe_layer(x2, layer, ffn_gain, next_gain, emit_x, h_dtype, w_group, b_group, w_router, b_router,
              w_gate, w_up, w_down):
    n = x2.shape[0]
    route, hp, counts = router(x2, ffn_gain, w_group, b_group, w_router, b_router)
    slot_of, block_e, n_on, first, next_e, pad_from, pends = dispatch(route, counts, n)
    xs = scatter_rows(hp, slot_of, pad_from, pends, block_e.shape[0] * MOE_TM)
    y_rows = expert_blocks(xs, block_e, n_on, first, next_e, layer, w_gate, w_up, w_down)
    return combine(x2, route, y_rows, slot_of, next_gain, emit_x, h_dtype)


def kernel(x, rms_mix, rms_ffn, rms_final, w_in_ab, conv_dw_w, conv_dw_b, conv_ln_g, conv_ln_b, nat_rpb,
           w_out_ab, w_in_c, w_out_c, t5_bias, moe_w_group, moe_b_group, moe_w_router, moe_b_router,
           moe_w_gate, moe_w_up, moe_w_down):
    b, t, d = x.shape
    n = b * t
    x2 = x.reshape(n, d)
    conv_ch = conv_dw_w.shape[-1]
    nat_heads = nat_rpb.shape[1]
    dil_heads = w_out_c.shape[1] // HEAD_DIM

    h = rmsnorm(x2, rms_mix[0], BF16)
    nat_w = nat_heads * HEAD_DIM
    proj = matmul([h], w_in_ab[0], BF16, q_cols=(w_in_ab.shape[-1], 2 * conv_ch, 2 * conv_ch + nat_w),
                  name="in_proj_ab").reshape(b, t, -1)
    a_out = conformer_conv(proj, conv_dw_w[0], conv_dw_b[0], conv_ln_g[0], conv_ln_b[0])
    b_out = neighbourhood_attention(proj, nat_rpb[0], 2 * conv_ch // HEAD_DIM)
    x2 = matmul([a_out.reshape(n, -1), b_out.reshape(n, -1)], w_out_ab[0], F32, residual=x2,
                name="out_proj_ab")
    wg, wu, wd = moe_w_gate, moe_w_up, moe_w_down
    x2, h = moe_layer(x2, 0, rms_ffn[0], rms_mix[1], True, BF16, moe_w_group[0], moe_b_group[0],
                      moe_w_router[0], moe_b_router[0], wg, wu, wd)

    dil_w = dil_heads * HEAD_DIM
    proj = matmul([h], w_in_c[0], BF16, perm_dils=tuple(dil for _, dil in DIL_CONFIGS),
                  q_cols=(3 * dil_w, 0, dil_w), name="in_proj_c").reshape(b, t, -1)
    o = dilated_attention(proj, t5_bias, dil_heads)
    x2 = matmul([o.reshape(n, -1)], w_out_c[0], F32, residual=x2, name="out_proj_c")
    (y,) = moe_layer(x2, 1, rms_ffn[1], rms_final, False, x.dtype, moe_w_group[1], moe_b_group[1],
                     moe_w_router[1], moe_b_router[1], wg, wu, wd)
    return y.reshape(b, t, d)
```

```python
import functools
import math

import jax
import jax.numpy as jnp
from jax import lax
from jax.experimental import pallas as pl
from jax.experimental.pallas import tpu as pltpu

F32 = jnp.float32
BF16 = jnp.bfloat16

HEAD_DIM = 128
CONV_WIDTH = 31
CONV_PAD = CONV_WIDTH // 2
NAT_WIN_ROWS = 8
NAT_WIN_COLS = 16
GRID_W = 64
DIL_CONFIGS = ((128, 1), (512, 4), (2048, 16))
T5_BUCKETS = 32
T5_MAX_DIST = 1024
N_GROUPS = 4
EXPERTS_PER_GROUP = 8
N_EXPERTS = N_GROUPS * EXPERTS_PER_GROUP
TOP_K = 2
RMS_EPS = 1e-6
LN_EPS = 1e-5
NEG = -1e30

VMEM_LIMIT = 56 * 1024 * 1024
VMEM_LIMIT_STAGED = 60 * 1024 * 1024
SUBLANES = 8
BF16_SUBLANES = 16

MM_TM = 1024
MM_TN = 512
MM_STAGED_TN = 1024
NORM_TM = 512
CONV_TT = 128
CONV_RB = 32
CONV_CB = 512
NAT_QROWS = 8
NAT_SUBROWS = 4
NAT_KROWS = 12
DIL_TQ = 1024
DIL_MQ = 256
DIL_HALF = 64
PERM_BLOCK = 256
MOE_TM = 256
SCAT_TM = 1024
COMB_TM = 256
ROUTER_TM = 512


def _cparams(sem, vmem_limit=VMEM_LIMIT):
    return pltpu.CompilerParams(dimension_semantics=sem, vmem_limit_bytes=vmem_limit)


def _rmsnorm_kernel(x_ref, g_ref, o_ref):
    x = x_ref[...]
    ms = jnp.mean(x * x, axis=-1, keepdims=True)
    o_ref[...] = (x * lax.rsqrt(ms + RMS_EPS) * g_ref[...]).astype(o_ref.dtype)


def rmsnorm(x2, g, out_dtype):
    n, d = x2.shape
    return pl.pallas_call(
        _rmsnorm_kernel,
        grid=(n // NORM_TM,),
        in_specs=[pl.BlockSpec((NORM_TM, d), lambda i: (i, 0)),
                  pl.BlockSpec((1, d), lambda i: (0, 0))],
        out_specs=pl.BlockSpec((NORM_TM, d), lambda i: (i, 0)),
        out_shape=jax.ShapeDtypeStruct((n, d), out_dtype),
        compiler_params=_cparams(("arbitrary",)),
        name="rmsnorm",
    )(x2, g.reshape(1, d))


def _row_perm_matrix(dil):
    n = PERM_BLOCK // dil
    out = jnp.arange(PERM_BLOCK)
    src = dil * (out % n) + out // n
    return (src[:, None] == jnp.arange(PERM_BLOCK)[None, :]).astype(BF16)


def _q_factor(q_tiles):
    if q_tiles is None:
        return None
    period, lo, hi = q_tiles
    phase = lax.rem(pl.program_id(0), period)
    return jnp.where((phase >= lo) & (phase < hi), jnp.float32(HEAD_DIM ** -0.5), jnp.float32(1.0))


def _mm_store(out, o_ref, p_refs, perm_dils, tiles_per_group):
    if not p_refs:
        o_ref[...] = out
        return

    group = pl.program_id(0) // tiles_per_group
    p_iter = iter(p_refs)
    for g, dil in enumerate(perm_dils):
        p_ref = next(p_iter) if dil > 1 else None

        @pl.when(group == g)
        def _(p_ref=p_ref):
            if p_ref is None:
                o_ref[...] = out
            else:
                for blk in range(out.shape[0] // PERM_BLOCK):
                    rows = slice(blk * PERM_BLOCK, (blk + 1) * PERM_BLOCK)
                    o_ref[rows, :] = jnp.dot(p_ref[...], out[rows, :],
                                             preferred_element_type=F32).astype(o_ref.dtype)


def _mm_kernel(*refs, n_x, has_res, perm_dils, tiles_per_group, q_tiles):
    x_refs = refs[:n_x]
    w_refs = refs[n_x:2 * n_x]
    pos = 2 * n_x
    r_ref = refs[pos] if has_res else None
    pos += int(has_res)
    n_perm = sum(d > 1 for d in perm_dils)
    p_refs = refs[pos:pos + n_perm]
    pos += n_perm
    o_ref, wbf_ref = refs[pos], refs[pos + 1]

    @pl.when(pl.program_id(1) == 0)
    def _():
        factor = _q_factor(q_tiles)
        off = 0
        for w_ref in w_refs:
            kk = w_ref.shape[0]
            w = w_ref[...] if factor is None else w_ref[...] * factor
            wbf_ref[off:off + kk, :] = w.astype(BF16)
            off += kk

    acc = None
    off = 0
    for x_ref in x_refs:
        kk = x_ref.shape[1]
        part = jnp.dot(x_ref[...], wbf_ref[off:off + kk, :], preferred_element_type=F32)
        acc = part if acc is None else acc + part
        off += kk
    if has_res:
        acc = acc + r_ref[...]
    _mm_store(acc.astype(o_ref.dtype), o_ref, p_refs, perm_dils, tiles_per_group)


def _mm_staged_kernel(x_ref, w_hbm, *refs, has_res, perm_dils, tiles_per_group, q_tiles):
    r_ref = refs[0] if has_res else None
    refs = refs[int(has_res):]
    n_perm = sum(d > 1 for d in perm_dils)
    p_refs = refs[:n_perm]
    o_ref, stage_ref, wbf_ref, sem = refs[n_perm:]
    j, i = pl.program_id(0), pl.program_id(1)
    tn = wbf_ref.shape[1]

    def fetch(jj):
        return pltpu.make_async_copy(w_hbm.at[:, pl.ds(pl.multiple_of(jj * tn, tn), tn)], stage_ref, sem)

    @pl.when((j == 0) & (i == 0))
    def _():
        fetch(0).start()

    @pl.when(i == 0)
    def _():
        fetch(j).wait()
        factor = _q_factor(q_tiles)
        w = stage_ref[...] if factor is None else stage_ref[...] * factor
        wbf_ref[...] = w.astype(BF16)

        @pl.when(j + 1 < pl.num_programs(0))
        def _():
            fetch(j + 1).start()

    acc = jnp.dot(x_ref[...], wbf_ref[...], preferred_element_type=F32)
    if has_res:
        acc = acc + r_ref[...]
    _mm_store(acc.astype(o_ref.dtype), o_ref, p_refs, perm_dils, tiles_per_group)


def matmul(xs, w, out_dtype, residual=None, perm_dils=(), q_cols=None, name="matmul"):
    m = xs[0].shape[0]
    ks = [x.shape[1] for x in xs]
    ktot, nc = w.shape
    assert sum(ks) == ktot and all(k == ks[0] for k in ks)
    staged = len(xs) == 1
    tm, tn = min(MM_TM, m), min(MM_STAGED_TN if staged else MM_TN, nc)
    assert m % tm == 0 and nc % tn == 0
    in_specs = [pl.BlockSpec((tm, k), lambda j, i: (i, 0)) for k in ks]
    if staged:
        in_specs.append(pl.BlockSpec(memory_space=pl.ANY))
        args = [xs[0], w]
    else:
        in_specs += [pl.BlockSpec((k, tn), functools.partial(lambda j, i, p: (p, j), p=p))
                     for p, k in enumerate(ks)]
        args = list(xs) + [w] * len(xs)
    if residual is not None:
        in_specs.append(pl.BlockSpec((tm, tn), lambda j, i: (i, j)))
        args.append(residual)
    tiles_per_group = 0
    if perm_dils:
        assert out_dtype == BF16 and tm % PERM_BLOCK == 0 and (nc // tn) % len(perm_dils) == 0
        tiles_per_group = (nc // tn) // len(perm_dils)
        for dil in perm_dils:
            if dil > 1:
                in_specs.append(pl.BlockSpec((PERM_BLOCK, PERM_BLOCK), lambda j, i: (0, 0)))
                args.append(_row_perm_matrix(dil))
    q_tiles = None
    if q_cols is not None:
        assert all(c % tn == 0 for c in q_cols)
        q_tiles = tuple(c // tn for c in q_cols)
    common = dict(perm_dils=tuple(perm_dils), tiles_per_group=tiles_per_group, q_tiles=q_tiles)
    if staged:
        body = functools.partial(_mm_staged_kernel, has_res=residual is not None, **common)
        scratch = [pltpu.VMEM((ktot, tn), F32), pltpu.VMEM((ktot, tn), BF16), pltpu.SemaphoreType.DMA(())]
    else:
        body = functools.partial(_mm_kernel, n_x=len(xs), has_res=residual is not None, **common)
        scratch = [pltpu.VMEM((ktot, tn), BF16)]
    return pl.pallas_call(
        body,
        grid=(nc // tn, m // tm),
        in_specs=in_specs,
        out_specs=pl.BlockSpec((tm, tn), lambda j, i: (i, j)),
        out_shape=jax.ShapeDtypeStruct((m, nc), out_dtype),
        scratch_shapes=scratch,
        compiler_params=_cparams(("arbitrary", "arbitrary"), VMEM_LIMIT_STAGED if staged else VMEM_LIMIT),
        name=name,
    )(*args)


def _conv_kernel(ac_ref, gc_ref, ap_ref, gp_ref, an_ref, gn_ref, w_ref, b_ref, lg_ref, lb_ref,
                 o_ref, buf_ref, cv_ref, *, tt, ch):
    i = pl.program_id(1)
    last = pl.num_programs(1) - 1
    halo = BF16_SUBLANES

    def glu(a_ref, g_ref):
        return a_ref[0].astype(F32) * jax.nn.sigmoid(g_ref[0].astype(F32))

    buf_ref[0, 0:halo, :] = jnp.where(i > 0, glu(ap_ref, gp_ref), 0.0)
    buf_ref[0, halo:halo + tt, :] = glu(ac_ref, gc_ref)
    buf_ref[0, halo + tt:2 * halo + tt, :] = jnp.where(i < last, glu(an_ref, gn_ref), 0.0)
    span = tt + 2 * halo - SUBLANES
    for s in range(1, SUBLANES):
        buf_ref[s, 0:span, :] = buf_ref[0, s:s + span, :]

    base = halo - CONV_PAD
    for cb in range(ch // CONV_CB):
        cs = slice(cb * CONV_CB, (cb + 1) * CONV_CB)
        for rb in range(tt // CONV_RB):
            accs = [jnp.zeros((SUBLANES, CONV_CB), F32) for _ in range(CONV_RB // SUBLANES)]
            for j in range(CONV_WIDTH):
                shift = (base + j) % SUBLANES
                r0 = rb * CONV_RB + base + j - shift
                wj = w_ref[j, :, cs]
                for k in range(len(accs)):
                    rk = r0 + k * SUBLANES
                    accs[k] = accs[k] + wj * buf_ref[shift, rk:rk + SUBLANES, cs]
            for k, acc in enumerate(accs):
                rk = rb * CONV_RB + k * SUBLANES
                cv_ref[rk:rk + SUBLANES, cs] = acc + b_ref[:, cs]

    a = cv_ref[...]
    mu = jnp.mean(a, axis=-1, keepdims=True)
    ctr = a - mu
    var = jnp.mean(ctr * ctr, axis=-1, keepdims=True)
    y = ctr * lax.rsqrt(var + LN_EPS) * lg_ref[...] + lb_ref[...]
    o_ref[0] = (y * jax.nn.sigmoid(y)).astype(o_ref.dtype)


def conformer_conv(proj3, w_dw, b_dw, ln_g, ln_b):
    b, t, _ = proj3.shape
    ch = w_dw.shape[1]
    tt, halo = CONV_TT, BF16_SUBLANES
    hb = tt // halo
    n_hb = t // halo
    cur = lambda col: pl.BlockSpec((1, tt, ch), lambda bi, i: (bi, i, col))
    prv = lambda col: pl.BlockSpec((1, halo, ch), lambda bi, i: (bi, jnp.maximum(i * hb - 1, 0), col))
    nxt = lambda col: pl.BlockSpec((1, halo, ch), lambda bi, i: (bi, jnp.minimum((i + 1) * hb, n_hb - 1), col))
    vec = lambda rows: pl.BlockSpec((rows, ch), lambda bi, i: (0, 0))
    return pl.pallas_call(
        functools.partial(_conv_kernel, tt=tt, ch=ch),
        grid=(b, t // tt),
        in_specs=[cur(0), cur(1), prv(0), prv(1), nxt(0), nxt(1),
                  pl.BlockSpec((CONV_WIDTH, SUBLANES, ch), lambda bi, i: (0, 0, 0)), vec(1), vec(1), vec(1)],
        out_specs=pl.BlockSpec((1, tt, ch), lambda bi, i: (bi, i, 0)),
        out_shape=jax.ShapeDtypeStruct((b, t, ch), BF16),
        scratch_shapes=[pltpu.VMEM((SUBLANES, tt + 2 * halo, ch), F32), pltpu.VMEM((tt, ch), F32)],
        compiler_params=_cparams(("arbitrary", "arbitrary")),
        name="conformer_conv",
    )(proj3, proj3, proj3, proj3, proj3, proj3,
      jnp.broadcast_to(w_dw[:, None, :], (CONV_WIDTH, SUBLANES, ch)), b_dw.reshape(1, ch), ln_g.reshape(1, ch),
      ln_b.reshape(1, ch))


def _nat_bias_table(rpb):
    h = rpb.shape[0]
    kc = NAT_WIN_COLS
    cols = jnp.arange(GRID_W)
    col_start = jnp.clip(cols - kc // 2, 0, GRID_W - kc)
    col_mask = (cols[None, :] >= col_start[:, None]) & (cols[None, :] < col_start[:, None] + kc)
    dc_idx = jnp.clip(cols[None, :] - cols[:, None], -(kc - 1), kc - 1) + (kc - 1)
    bc = jnp.where(col_mask[None, None], rpb[:, :, dc_idx].astype(F32), NEG)
    blank = jnp.full((h, 1, GRID_W, GRID_W), NEG, F32)
    bc = jnp.concatenate([blank, bc, blank], axis=1)
    return jnp.concatenate([bc[:, :-1], bc[:, 1:]], axis=-1)


def _nat_kernel(q_ref, k_ref, v_ref, tab_ref, o_ref, bias_ref, *, rows):
    blk = pl.program_id(2)
    r_base = blk * NAT_QROWS
    n_sub = NAT_QROWS // NAT_SUBROWS
    tq = NAT_SUBROWS * GRID_W
    ws = [jnp.clip(r_base + j * NAT_SUBROWS - NAT_WIN_ROWS // 2, 0, rows - NAT_KROWS) for j in range(n_sub)]
    lane = lax.broadcasted_iota(jnp.int32, (GRID_W, 2 * GRID_W), 1)

    @pl.when((blk <= 1) | (blk == pl.num_programs(2) - 1))
    def _():
        for j in range(n_sub):
            for qr in range(NAT_SUBROWS):
                r = r_base + j * NAT_SUBROWS + qr
                r0 = jnp.clip(r - NAT_WIN_ROWS // 2, 0, rows - NAT_WIN_ROWS)
                for p in range(NAT_KROWS // 2):
                    kr = ws[j] + 2 * p
                    entry = jnp.clip(kr - r + NAT_WIN_ROWS, 0, 2 * NAT_WIN_ROWS - 1)
                    ok0 = ((kr >= r0) & (kr < r0 + NAT_WIN_ROWS)).astype(jnp.int32)
                    ok1 = ((kr + 1 >= r0) & (kr + 1 < r0 + NAT_WIN_ROWS)).astype(jnp.int32)
                    ok = jnp.where(lane < GRID_W, ok0, ok1) > 0
                    bias_ref[j, qr * GRID_W:(qr + 1) * GRID_W, p * 2 * GRID_W:(p + 1) * 2 * GRID_W] = (
                        jnp.where(ok, tab_ref[0, entry], NEG))

    def window(ref, j):
        return ref[0, pl.ds(pl.multiple_of(ws[j] * GRID_W, 128), NAT_KROWS * GRID_W), :]

    scores = [lax.dot_general(q_ref[0, j * tq:(j + 1) * tq, :], window(k_ref, j), (((1,), (1,)), ((), ())),
                              preferred_element_type=F32) + bias_ref[j]
              for j in range(n_sub)]
    stats = []
    for s in scores:
        m = jnp.max(s, axis=-1, keepdims=True)
        p = jnp.exp(s - m)
        stats.append((jnp.sum(p, axis=-1, keepdims=True), p.astype(BF16)))
    for j, (den, p) in enumerate(stats):
        o = jnp.dot(p, window(v_ref, j), preferred_element_type=F32)
        o_ref[0, j * tq:(j + 1) * tq, :] = (o / den).astype(o_ref.dtype)


def neighbourhood_attention(proj3, rpb, col0):
    b, t, _ = proj3.shape
    nh = rpb.shape[0]
    rows = t // GRID_W
    assert rows >= NAT_KROWS and rows % NAT_QROWS == 0
    tq = NAT_QROWS * GRID_W
    tab = _nat_bias_table(rpb)
    return pl.pallas_call(
        functools.partial(_nat_kernel, rows=rows),
        grid=(nh, b, rows // NAT_QROWS),
        in_specs=[pl.BlockSpec((1, tq, HEAD_DIM), lambda h, bi, i: (bi, i, col0 + h)),
                  pl.BlockSpec((1, t, HEAD_DIM), lambda h, bi, i: (bi, 0, col0 + nh + h)),
                  pl.BlockSpec((1, t, HEAD_DIM), lambda h, bi, i: (bi, 0, col0 + 2 * nh + h)),
                  pl.BlockSpec((1, 2 * NAT_WIN_ROWS, GRID_W, 2 * GRID_W), lambda h, bi, i: (h, 0, 0, 0))],
        out_specs=pl.BlockSpec((1, tq, HEAD_DIM), lambda h, bi, i: (bi, i, h)),
        out_shape=jax.ShapeDtypeStruct((b, t, nh * HEAD_DIM), BF16),
        scratch_shapes=[pltpu.VMEM((NAT_QROWS // NAT_SUBROWS, NAT_SUBROWS * GRID_W, NAT_KROWS * GRID_W), F32)],
        compiler_params=_cparams(("arbitrary", "arbitrary", "arbitrary")),
        name="neighbourhood_attention",
    )(proj3, proj3, proj3, tab)


def _t5_bucket(rel):
    nb = T5_BUCKETS // 2
    max_exact = nb // 2
    n = jnp.abs(rel)
    sign = jnp.where(rel > 0, nb, 0)
    nf = jnp.maximum(n, 1).astype(F32)
    large = max_exact + (jnp.log(nf / max_exact) / math.log(T5_MAX_DIST / max_exact)
                         * (nb - max_exact)).astype(jnp.int32)
    large = jnp.minimum(large, nb - 1)
    return sign + jnp.where(n < max_exact, n, large)


def _dil_dims(g):
    win, dil = DIL_CONFIGS[g]
    assert win == 2 * dil * DIL_HALF
    mq = min(DIL_MQ, DIL_TQ // dil)
    wk = -(-(mq + 2 * DIL_HALF) // 128) * 128
    front = dil * DIL_HALF
    back = dil * (wk - mq - DIL_HALF)
    return dil, mq, wk, front, back


def _dil_bias_rows(t5_bias, g, nh):
    dil, mq, wk, _, _ = _dil_dims(g)
    delta = jnp.arange(wk) - DIL_HALF
    on = delta <= DIL_HALF
    vals = t5_bias[:, g * nh:(g + 1) * nh][_t5_bucket(delta * dil)].astype(F32)
    return jnp.where(on[None], vals.T, NEG)[:, None, :]


def _dil_kernel(q0_ref, q1_ref, q2_ref, u0_ref, u1_ref, u2_ref, proj_ref, o_ref,
                k0, v0, k1, v1, k2, v2, b0_ref, b1_ref, b2_ref, m_ref, l_ref, n_ref, sem, *, t, nh):
    h, bi, i = pl.program_id(0), pl.program_id(1), pl.program_id(2)
    kv = ((k0, v0), (k1, v1), (k2, v2))

    @pl.when((bi == 0) & (i == 0))
    def _():
        for u_ref, b_ref in ((u0_ref, b0_ref), (u1_ref, b1_ref), (u2_ref, b2_ref)):
            rows = jnp.broadcast_to(u_ref[0], b_ref.shape)
            b_ref[...] = pltpu.roll(rows, 0, 1, stride=1, stride_axis=0)

    n_b = pl.num_programs(1)
    pair = h * n_b + bi
    slot = lax.rem(pair, 2)

    def kv_copies(pair_, slot_):
        h_, b_ = pair_ // n_b, lax.rem(pair_, n_b)
        cps = []
        for g in range(3):
            front = _dil_dims(g)[3]
            for w in range(2):
                col = pl.multiple_of(((g * 3 + 1 + w) * nh + h_) * HEAD_DIM, HEAD_DIM)
                cps.append(pltpu.make_async_copy(proj_ref.at[b_, :, pl.ds(col, HEAD_DIM)],
                                                 kv[g][w].at[slot_, pl.ds(front, t), :],
                                                 sem.at[slot_, g * 2 + w]))
        return cps

    @pl.when(i == 0)
    def _():
        @pl.when(pair == 0)
        def _():
            for cp in kv_copies(pair, slot):
                cp.start()
            for g in range(3):
                _, _, _, front, back = _dil_dims(g)
                for w in range(2):
                    for s in range(2):
                        kv[g][w][s, 0:front, :] = jnp.zeros((front, HEAD_DIM), BF16)
                        kv[g][w][s, front + t:front + t + back, :] = jnp.zeros((back, HEAD_DIM), BF16)

        for cp in kv_copies(pair, slot):
            cp.wait()

        @pl.when(pair + 1 < pl.num_programs(0) * n_b)
        def _():
            for cp in kv_copies(pair + 1, 1 - slot):
                cp.start()

    t0 = pl.multiple_of(i * DIL_TQ, DIL_TQ)

    def class_rows(ref, first_block_row, n_blocks, dil, r):
        n = PERM_BLOCK // dil
        parts = []
        for blk in range(n_blocks):
            start = first_block_row + blk * PERM_BLOCK + r * n
            if not isinstance(start, int):
                start = pl.multiple_of(start, n)
            parts.append(ref[pl.ds(start, n), :])
        return jnp.concatenate(parts, axis=0)

    def group_tiles(g, q_ref, b_ref):
        dil, mq, wk, front, _ = _dil_dims(g)
        if dil == 1:
            tiles = [(q_ref[0, j * mq:(j + 1) * mq, :], t0 + j * mq, 0, slice(j * mq, (j + 1) * mq))
                     for j in range(DIL_TQ // mq)]
        else:
            assert dil * mq == DIL_TQ
            tiles = [(class_rows(q_ref.at[0], 0, DIL_TQ // PERM_BLOCK, dil, r), t0, r,
                      pl.ds(r, mq, stride=dil)) for r in range(dil)]

        def keys_values(which, row0, r):
            if dil == 1:
                return kv[g][which][slot, pl.ds(pl.multiple_of(row0, 128), wk), :]
            return class_rows(kv[g][which].at[slot], row0, wk * dil // PERM_BLOCK, dil, r)

        scores = []
        for q, row0, r, _ in tiles:
            s = lax.dot_general(q, keys_values(0, row0, r), (((1,), (1,)), ((), ())),
                                preferred_element_type=F32)
            kpos = (row0 - front + r) + dil * lax.broadcasted_iota(jnp.int32, (1, wk), 1)
            kmask = jnp.where((kpos >= 0) & (kpos < t), 0.0, NEG)
            scores.append(s + b_ref[...] + kmask)
        stats = []
        for s in scores:
            m = jnp.max(s, axis=-1, keepdims=True)
            p = jnp.exp(s - m)
            stats.append((m, jnp.sum(p, axis=-1, keepdims=True), p.astype(BF16)))
        for (_, row0, r, rows), (m, den, p) in zip(tiles, stats):
            n_ref[g, rows, :] = jnp.dot(p, keys_values(1, row0, r), preferred_element_type=F32)
            m_ref[g, rows, :] = jnp.broadcast_to(m, (mq, HEAD_DIM))
            l_ref[g, rows, :] = jnp.broadcast_to(den, (mq, HEAD_DIM))

    for g, q_ref, b_ref in ((0, q0_ref, b0_ref), (1, q1_ref, b1_ref), (2, q2_ref, b2_ref)):
        group_tiles(g, q_ref, b_ref)

    m_all = jnp.maximum(jnp.maximum(m_ref[0], m_ref[1]), m_ref[2])
    num = jnp.zeros((DIL_TQ, HEAD_DIM), F32)
    den = jnp.zeros((DIL_TQ, HEAD_DIM), F32)
    for g in range(3):
        w = jnp.exp(m_ref[g] - m_all)
        num = num + w * n_ref[g]
        den = den + w * l_ref[g]
    o_ref[0] = (num / den).astype(o_ref.dtype)


def dilated_attention(proj3, t5_bias, nh):
    b, t, _ = proj3.shape
    assert t % DIL_TQ == 0
    tabs = [_dil_bias_rows(t5_bias, g, nh) for g in range(3)]
    qspec = lambda g: pl.BlockSpec((1, DIL_TQ, HEAD_DIM), lambda h, bi, i: (bi, i, g * 3 * nh + h))
    bspec = lambda g: pl.BlockSpec((1, 1, _dil_dims(g)[2]), lambda h, bi, i: (h, 0, 0))
    scratch = []
    for g in range(3):
        _, _, _, front, back = _dil_dims(g)
        assert g == 0 or (front % PERM_BLOCK == 0 and back % PERM_BLOCK == 0)
        scratch += [pltpu.VMEM((2, front + t + back, HEAD_DIM), BF16)] * 2
    scratch += [pltpu.VMEM(_dil_dims(g)[1:3], F32) for g in range(3)]
    scratch += [pltpu.VMEM((3, DIL_TQ, HEAD_DIM), F32)] * 3
    scratch.append(pltpu.SemaphoreType.DMA((2, 6)))
    return pl.pallas_call(
        functools.partial(_dil_kernel, t=t, nh=nh),
        grid=(nh, b, t // DIL_TQ),
        in_specs=[qspec(0), qspec(1), qspec(2), bspec(0), bspec(1), bspec(2),
                  pl.BlockSpec(memory_space=pl.ANY)],
        out_specs=pl.BlockSpec((1, DIL_TQ, HEAD_DIM), lambda h, bi, i: (bi, i, h)),
        out_shape=jax.ShapeDtypeStruct((b, t, nh * HEAD_DIM), BF16),
        scratch_shapes=scratch,
        compiler_params=_cparams(("arbitrary", "arbitrary", "arbitrary")),
        name="dilated_attention",
    )(proj3, proj3, proj3, tabs[0], tabs[1], tabs[2], proj3)


ROUTER_LANES = 128


def _pack_rounded(lo, hi):
    lo_bits = lax.bitcast_convert_type(lo, jnp.uint32)
    hi_bits = lax.bitcast_convert_type(hi, jnp.uint32)
    return lax.shift_right_logical(lo_bits, jnp.uint32(16)) | (hi_bits & jnp.uint32(0xFFFF0000))


def _pack_pair(lo, hi):
    return _pack_rounded(lo.astype(BF16).astype(F32), hi.astype(BF16).astype(F32))


def _unpack_pair(word):
    lo = lax.bitcast_convert_type(lax.shift_left(word, jnp.uint32(16)), F32)
    hi = lax.bitcast_convert_type(word & jnp.uint32(0xFFFF0000), F32)
    return lo, hi


def _router_kernel(x_ref, g_ref, wcat_ref, b_ref, o_ref, hp_ref, cnt_ref, carry_ref):
    step = pl.program_id(0)
    half = x_ref.shape[1] // 2
    x = x_ref[...]
    ms = jnp.mean(x * x, axis=-1, keepdims=True)
    h = x * lax.rsqrt(ms + RMS_EPS) * g_ref[...]
    h_hi = h.astype(BF16)
    h_hi32 = h_hi.astype(F32)
    h_lo = (h - h_hi32).astype(BF16)
    hp_ref[...] = _pack_rounded(h_hi32[:, :half], h_hi32[:, half:])
    hi_both = jnp.dot(h_hi, wcat_ref[...], preferred_element_type=F32)
    logits = (hi_both[:, :ROUTER_LANES] + hi_both[:, ROUTER_LANES:]
              + jnp.dot(h_lo, wcat_ref[:, :ROUTER_LANES], preferred_element_type=F32)) + b_ref[...]

    lane = lax.broadcasted_iota(jnp.int32, logits.shape, 1)
    big = jnp.int32(ROUTER_LANES)

    def first_argmax(vals):
        top = jnp.max(vals, axis=-1, keepdims=True)
        return top, jnp.min(jnp.where(vals == top, lane, big), axis=-1, keepdims=True)

    gl = jnp.where(lane < N_GROUPS, logits, NEG)
    g_top, g_idx = first_argmax(gl)
    g_val = 1.0 / jnp.sum(jnp.exp(gl - g_top), axis=-1, keepdims=True)
    in_group = (lane >= N_GROUPS) & (lane < N_GROUPS + N_EXPERTS) & (
        lax.shift_right_arithmetic(lane - N_GROUPS, jnp.int32(3)) == g_idx)
    el = jnp.where(in_group, logits, NEG)
    v1, i1 = first_argmax(el)
    el2 = jnp.where(lane == i1, NEG, el)
    v2, i2 = first_argmax(el2)
    e21 = jnp.exp(v2 - v1)
    gate1 = g_val / (1.0 + e21)
    gate2 = g_val * e21 / (1.0 + e21)

    @pl.when(step == 0)
    def _():
        carry_ref[...] = jnp.zeros(carry_ref.shape, F32)

    e1, e2 = i1 - N_GROUPS, i2 - N_GROUPS
    oh1 = (lane == e1).astype(F32)
    oh2 = (lane == e2).astype(F32)
    both = oh1 + oh2
    tm = x.shape[0]
    tri = (lax.broadcasted_iota(jnp.int32, (tm, tm), 1) < lax.broadcasted_iota(jnp.int32, (tm, tm), 0))
    before = jnp.dot(tri.astype(BF16), both.astype(BF16), preferred_element_type=F32) + carry_ref[...]
    rank1 = jnp.sum(oh1 * before, axis=-1, keepdims=True)
    rank2 = jnp.sum(oh2 * before, axis=-1, keepdims=True)
    carry_ref[...] = carry_ref[...] + jnp.sum(both, axis=0, keepdims=True)
    cnt_ref[...] = jnp.broadcast_to(carry_ref[...], cnt_ref.shape)

    out = jnp.where(lane == 0, gate1, 0.0)
    out = jnp.where(lane == 1, gate2, out)
    out = jnp.where(lane == 2, e1.astype(F32), out)
    out = jnp.where(lane == 3, e2.astype(F32), out)
    out = jnp.where(lane == 4, rank1, out)
    out = jnp.where(lane == 5, rank2, out)
    o_ref[...] = out


def router(x2, gain, w_group, b_group, w_router, b_router):
    n, d = x2.shape
    wcat = jnp.concatenate([w_group, w_router], axis=1).astype(F32)
    wcat = jnp.pad(wcat, ((0, 0), (0, ROUTER_LANES - wcat.shape[1])))
    w_hi = wcat.astype(BF16)
    w_lo = (wcat - w_hi.astype(F32)).astype(BF16)
    bias = jnp.concatenate([b_group.reshape(-1), b_router.reshape(-1)]).astype(F32)
    bias = jnp.pad(bias, (0, ROUTER_LANES - bias.shape[0])).reshape(1, ROUTER_LANES)
    tm = ROUTER_TM
    return pl.pallas_call(
        _router_kernel,
        grid=(n // tm,),
        in_specs=[pl.BlockSpec((tm, d), lambda i: (i, 0)),
                  pl.BlockSpec((1, d), lambda i: (0, 0)),
                  pl.BlockSpec((d, 2 * ROUTER_LANES), lambda i: (0, 0)),
                  pl.BlockSpec((1, ROUTER_LANES), lambda i: (0, 0))],
        out_specs=[pl.BlockSpec((tm, ROUTER_LANES), lambda i: (i, 0)),
                   pl.BlockSpec((tm, d // 2), lambda i: (i, 0)),
                   pl.BlockSpec((8, ROUTER_LANES), lambda i: (0, 0))],
        out_shape=[jax.ShapeDtypeStruct((n, ROUTER_LANES), F32),
                   jax.ShapeDtypeStruct((n, d // 2), jnp.uint32),
                   jax.ShapeDtypeStruct((8, ROUTER_LANES), F32)],
        scratch_shapes=[pltpu.VMEM((1, ROUTER_LANES), F32)],
        compiler_params=_cparams(("arbitrary",)),
        name="moe_router",
    )(x2, gain.reshape(1, d), jnp.concatenate([w_hi, w_lo], axis=1), bias)


def dispatch(route, counts, n):
    n_blocks = n * TOP_K // MOE_TM + N_EXPERTS
    counts = counts[0, :N_EXPERTS].astype(jnp.int32)
    pcounts = (counts + MOE_TM - 1) // MOE_TM * MOE_TM
    pends = jnp.cumsum(pcounts)
    pstarts = pends - pcounts
    expert = route[:, 2:2 + TOP_K].astype(jnp.int32)
    rank = route[:, 2 + TOP_K:2 + 2 * TOP_K].astype(jnp.int32)
    ids = jnp.arange(N_EXPERTS, dtype=jnp.int32)
    base = jnp.sum(jnp.where(expert[..., None] == ids, pstarts, 0), axis=-1)
    slot_of = (base + rank).reshape(-1)
    blk_start = jnp.arange(n_blocks, dtype=jnp.int32) * MOE_TM
    block_e = jnp.minimum(jnp.sum(blk_start[:, None] >= pends[None, :], axis=-1), N_EXPERTS - 1)
    block_e = block_e.astype(jnp.int32)
    n_on = (pends[-1] // MOE_TM).astype(jnp.int32)
    pad_from = pstarts + counts
    blk_ids = jnp.arange(n_blocks, dtype=jnp.int32)
    first = ((blk_ids == 0) | (block_e != jnp.roll(block_e, 1))) & (blk_ids < n_on)
    later = jnp.where((ids[None, :] > ids[:, None]) & (counts[None, :] > 0), ids[None, :], N_EXPERTS)
    next_nonempty = jnp.min(later, axis=-1)
    next_e = jnp.sum(jnp.where(block_e[:, None] == ids, next_nonempty, 0), axis=-1)
    next_e = jnp.where(next_e < N_EXPERTS, next_e, -1).astype(jnp.int32)
    return slot_of, block_e, n_on.reshape(1), first.astype(jnp.int32), next_e, pad_from, pends


def _scatter_kernel(slot_ref, padfrom_ref, pend_ref, hp_ref, xs_hbm, zero_ref, sem, zsem):
    i = pl.program_id(0)
    tm = SCAT_TM
    n_blocks = xs_hbm.shape[0] // MOE_TM

    def pad_rows(fn):
        for e in range(N_EXPERTS):
            def body(r, carry):
                fn(pltpu.make_async_copy(zero_ref.at[pl.ds(0, 1), :], xs_hbm.at[pl.ds(r, 1), :], zsem))
                return carry
            lax.fori_loop(padfrom_ref[e], pend_ref[e], body, 0)

        def blk(b, carry):
            start = pl.multiple_of(b * MOE_TM, MOE_TM)
            fn(pltpu.make_async_copy(zero_ref, xs_hbm.at[pl.ds(start, MOE_TM), :], zsem))
            return carry
        lax.fori_loop(pend_ref[N_EXPERTS - 1] // MOE_TM, n_blocks, blk, 0)

    @pl.when(i == 0)
    def _():
        zero_ref[...] = jnp.zeros(zero_ref.shape, zero_ref.dtype)
        pad_rows(lambda cp: cp.start())

    def body(s, carry):
        for k in range(TOP_K):
            row = slot_ref[(i * tm + s) * TOP_K + k]
            pltpu.make_async_copy(hp_ref.at[pl.ds(s, 1), :], xs_hbm.at[pl.ds(row, 1), :], sem).start()
        return carry
    lax.fori_loop(0, tm, body, 0, unroll=8)
    for k in range(TOP_K):
        pltpu.make_async_copy(hp_ref, xs_hbm.at[pl.ds(0, tm), :], sem).wait()

    @pl.when(i == pl.num_programs(0) - 1)
    def _():
        pad_rows(lambda cp: cp.wait())


def scatter_rows(hp, slot_of, pad_from, pends, n_rows):
    n, w = hp.shape
    grid_spec = pltpu.PrefetchScalarGridSpec(
        num_scalar_prefetch=3,
        grid=(n // SCAT_TM,),
        in_specs=[pl.BlockSpec((SCAT_TM, w), lambda i, slots, pf, pe: (i, 0))],
        out_specs=pl.BlockSpec(memory_space=pl.ANY),
        scratch_shapes=[pltpu.VMEM((MOE_TM, w), hp.dtype),
                        pltpu.SemaphoreType.DMA(()), pltpu.SemaphoreType.DMA(())],
    )
    return pl.pallas_call(
        _scatter_kernel,
        grid_spec=grid_spec,
        out_shape=jax.ShapeDtypeStruct((n_rows, w), hp.dtype),
        compiler_params=_cparams(("arbitrary",)),
        name="moe_scatter",
    )(slot_of, pad_from, pends, hp)


def _expert_kernel(be_ref, non_ref, first_ref, next_ref, xs_ref, wg_hbm, wu_hbm, wd_hbm, o_ref,
                   stage_g, stage_u, stage_d, wg_ref, wu_ref, wd_ref, sem, *, layer):
    b = pl.program_id(0)
    on = b < non_ref[0]
    staged = ((wg_hbm, stage_g), (wu_hbm, stage_u), (wd_hbm, stage_d))

    def weight_copies(e):
        return [pltpu.make_async_copy(w_hbm.at[layer, e], stage, sem.at[k])
                for k, (w_hbm, stage) in enumerate(staged)]

    @pl.when(b == 0)
    def _():
        for cp in weight_copies(be_ref[0]):
            cp.start()

    @pl.when(first_ref[b] > 0)
    def _():
        for cp in weight_copies(be_ref[b]):
            cp.wait()
        for (_, stage), w_ref in zip(staged, (wg_ref, wu_ref, wd_ref)):
            w_ref[...] = stage[...].astype(BF16)

        @pl.when(next_ref[b] >= 0)
        def _():
            for cp in weight_copies(next_ref[b]):
                cp.start()

    @pl.when(jnp.logical_not(on))
    def _():
        o_ref[...] = jnp.zeros(o_ref.shape, o_ref.dtype)

    @pl.when(on)
    def _():
        lo, hi = _unpack_pair(xs_ref[...])
        lo, hi = lo.astype(BF16), hi.astype(BF16)
        half = lo.shape[1]

        def proj(w_ref):
            return (jnp.dot(lo, w_ref[:half, :], preferred_element_type=F32)
                    + jnp.dot(hi, w_ref[half:, :], preferred_element_type=F32))

        g = proj(wg_ref)
        u = proj(wu_ref)
        hdn = (g * jax.nn.sigmoid(g) * u).astype(BF16)
        y = jnp.dot(hdn, wd_ref[...], preferred_element_type=F32)
        o_ref[...] = _pack_pair(y[:, :half], y[:, half:])


def expert_blocks(xs, block_e, n_on, first, next_e, layer, w_gate, w_up, w_down):
    n_rows, half = xs.shape
    _, _, d, hid = w_gate.shape
    tm = MOE_TM
    row = lambda b, *_: (b, 0)
    grid_spec = pltpu.PrefetchScalarGridSpec(
        num_scalar_prefetch=4,
        grid=(n_rows // tm,),
        in_specs=[pl.BlockSpec((tm, half), row)] + [pl.BlockSpec(memory_space=pl.ANY)] * 3,
        out_specs=pl.BlockSpec((tm, half), row),
        scratch_shapes=[pltpu.VMEM((d, hid), F32), pltpu.VMEM((d, hid), F32), pltpu.VMEM((hid, d), F32),
                        pltpu.VMEM((d, hid), BF16), pltpu.VMEM((d, hid), BF16), pltpu.VMEM((hid, d), BF16),
                        pltpu.SemaphoreType.DMA((3,))],
    )
    return pl.pallas_call(
        functools.partial(_expert_kernel, layer=layer),
        grid_spec=grid_spec,
        out_shape=jax.ShapeDtypeStruct((n_rows, half), xs.dtype),
        compiler_params=_cparams(("arbitrary",)),
        name="moe_experts",
    )(block_e, n_on, first, next_e, xs, w_gate, w_up, w_down)


def _combine_kernel(slot_ref, x_ref, r_ref, gain_ref, y_hbm, *rest, emit_x):
    if emit_x:
        xo_ref, ho_ref, ybuf, sem = rest
    else:
        ho_ref, ybuf, sem = rest
        xo_ref = None
    i = pl.program_id(0)
    ni = pl.num_programs(0)
    slot = i % 2
    tm = COMB_TM

    def gather(blk, slot_):
        def body(s, carry):
            for k in range(TOP_K):
                row = slot_ref[(blk * tm + s) * TOP_K + k]
                pltpu.make_async_copy(y_hbm.at[pl.ds(row, 1), :], ybuf.at[slot_, k, pl.ds(s, 1), :],
                                      sem.at[slot_]).start()
            return carry
        lax.fori_loop(0, tm, body, 0, unroll=8)

    @pl.when(i == 0)
    def _():
        gather(0, 0)

    @pl.when(i + 1 < ni)
    def _():
        gather(i + 1, 1 - slot)

    for k in range(TOP_K):
        pltpu.make_async_copy(y_hbm.at[pl.ds(0, tm), :], ybuf.at[slot, k], sem.at[slot]).wait()

    half = x_ref.shape[1] // 2
    g0, g1 = r_ref[:, 0:1], r_ref[:, 1:2]
    y0_lo, y0_hi = _unpack_pair(ybuf[slot, 0])
    y1_lo, y1_hi = _unpack_pair(ybuf[slot, 1])
    x_lo = x_ref[:, :half] + (g0 * y0_lo + g1 * y1_lo)
    x_hi = x_ref[:, half:] + (g0 * y0_hi + g1 * y1_hi)
    if emit_x:
        xo_ref[:, :half] = x_lo
        xo_ref[:, half:] = x_hi
    ssq = jnp.sum(x_lo * x_lo, axis=-1, keepdims=True) + jnp.sum(x_hi * x_hi, axis=-1, keepdims=True)
    inv = lax.rsqrt(ssq / x_ref.shape[1] + RMS_EPS)
    ho_ref[:, :half] = (x_lo * inv * gain_ref[:, :half]).astype(ho_ref.dtype)
    ho_ref[:, half:] = (x_hi * inv * gain_ref[:, half:]).astype(ho_ref.dtype)


def combine(x2, route, y_rows, slot_of, gain, emit_x, h_dtype):
    n, d = x2.shape
    tm = COMB_TM
    row = lambda i, slots: (i, 0)
    out_shape = [jax.ShapeDtypeStruct((n, d), h_dtype)]
    out_specs = [pl.BlockSpec((tm, d), row)]
    if emit_x:
        out_shape.insert(0, jax.ShapeDtypeStruct((n, d), F32))
        out_specs.insert(0, pl.BlockSpec((tm, d), row))
    grid_spec = pltpu.PrefetchScalarGridSpec(
        num_scalar_prefetch=1,
        grid=(n // tm,),
        in_specs=[pl.BlockSpec((tm, d), row),
                  pl.BlockSpec((tm, ROUTER_LANES), row),
                  pl.BlockSpec((1, d), lambda i, slots: (0, 0)),
                  pl.BlockSpec(memory_space=pl.ANY)],
        out_specs=out_specs,
        scratch_shapes=[pltpu.VMEM((2, TOP_K, tm, d // 2), y_rows.dtype), pltpu.SemaphoreType.DMA((2,))],
    )
    return pl.pallas_call(
        functools.partial(_combine_kernel, emit_x=emit_x),
        grid_spec=grid_spec,
        out_shape=out_shape,
        compiler_params=_cparams(("arbitrary",)),
        name="moe_combine",
    )(slot_of, x2, route, gain.reshape(1, d), y_rows)


def moe_layer(x2, layer, ffn_gain, next_gain, emit_x, h_dtype, w_group, b_group, w_router, b_router,
              w_gate, w_up, w_down):
    n = x2.shape[0]
    route, hp, counts = router(x2, ffn_gain, w_group, b_group, w_router, b_router)
    slot_of, block_e, n_on, first, next_e, pad_from, pends = dispatch(route, counts, n)
    xs = scatter_rows(hp, slot_of, pad_from, pends, block_e.shape[0] * MOE_TM)
    y_rows = expert_blocks(xs, block_e, n_on, first, next_e, layer, w_gate, w_up, w_down)
    return combine(x2, route, y_rows, slot_of, next_gain, emit_x, h_dtype)


def kernel(x, rms_mix, rms_ffn, rms_final, w_in_ab, conv_dw_w, conv_dw_b, conv_ln_g, conv_ln_b, nat_rpb,
           w_out_ab, w_in_c, w_out_c, t5_bias, moe_w_group, moe_b_group, moe_w_router, moe_b_router,
           moe_w_gate, moe_w_up, moe_w_down):
    b, t, d = x.shape
    n = b * t
    x2 = x.reshape(n, d)
    conv_ch = conv_dw_w.shape[-1]
    nat_heads = nat_rpb.shape[1]
    dil_heads = w_out_c.shape[1] // HEAD_DIM

    h = rmsnorm(x2, rms_mix[0], BF16)
    nat_w = nat_heads * HEAD_DIM
    proj = matmul([h], w_in_ab[0], BF16, q_cols=(w_in_ab.shape[-1], 2 * conv_ch, 2 * conv_ch + nat_w),
                  name="in_proj_ab").reshape(b, t, -1)
    a_out = conformer_conv(proj, conv_dw_w[0], conv_dw_b[0], conv_ln_g[0], conv_ln_b[0])
    b_out = neighbourhood_attention(proj, nat_rpb[0], 2 * conv_ch // HEAD_DIM)
    x2 = matmul([a_out.reshape(n, -1), b_out.reshape(n, -1)], w_out_ab[0], F32, residual=x2,
                name="out_proj_ab")
    wg, wu, wd = moe_w_gate, moe_w_up, moe_w_down
    x2, h = moe_layer(x2, 0, rms_ffn[0], rms_mix[1], True, BF16, moe_w_group[0], moe_b_group[0],
                      moe_w_router[0], moe_b_router[0], wg, wu, wd)

    dil_w = dil_heads * HEAD_DIM
    proj = matmul([h], w_in_c[0], BF16, perm_dils=tuple(dil for _, dil in DIL_CONFIGS),
                  q_cols=(3 * dil_w, 0, dil_w), name="in_proj_c").reshape(b, t, -1)
    o = dilated_attention(proj, t5_bias, dil_heads)
    x2 = matmul([o.reshape(n, -1)], w_out_c[0], F32, residual=x2, name="out_proj_c")
    (y,) = moe_layer(x2, 1, rms_ffn[1], rms_final, False, x.dtype, moe_w_group[1], moe_b_group[1],
                     moe_w_router[1], moe_b_router[1], wg, wu, wd)
    return y.reshape(b, t, d)
```
